```python
import math
import jax
import jax.numpy as jnp
from jax import lax
import numpy as np


D_MODEL = 1024
BATCH = 16
SEQ = 2048
DEPTH = 2

HEAD_DIM = 64
MIX_WIDTH = D_MODEL
N_MIXERS = 4
GROUP_WIDTH = MIX_WIDTH // N_MIXERS
A_HEADS = GROUP_WIDTH // HEAD_DIM
B_Q_HEADS = GROUP_WIDTH // HEAD_DIM
B_KV_HEADS = 2
D_Q_HEADS = GROUP_WIDTH // HEAD_DIM
D_KV_HEADS = 2
C_GROUPS = 4
C_CHUNK = 128
DILATED_CFGS = ((128, 1), (512, 4), (2048, 16))
DIL_BLOCK = 64
SWA_RADIUS = 128
SWA_BLOCK = 128
DENSE_BLOCK = 128
GRID_W = 64
ROPE_THETA = 10000.0
REL_BUCKETS = 32
REL_MAX_DIST = 1024
D_FF = 2816
CONV_WIDTH = 3
PLE_DIM = 256
EPS = 1e-6
NEG_INF = -1e30
ATTN_SCALE = HEAD_DIM ** -0.5

SPLIT_SIZES = (
    A_HEADS * HEAD_DIM, A_HEADS * HEAD_DIM, A_HEADS * HEAD_DIM,
    B_Q_HEADS * HEAD_DIM, B_KV_HEADS * HEAD_DIM, B_KV_HEADS * HEAD_DIM,
    GROUP_WIDTH, GROUP_WIDTH,
    D_Q_HEADS * HEAD_DIM, D_KV_HEADS * HEAD_DIM, D_KV_HEADS * HEAD_DIM,
)
IN_WIDTH = sum(SPLIT_SIZES)
SPLIT_POINTS = tuple(int(c) for c in np.cumsum(SPLIT_SIZES)[:-1])

kernel_name = "hybrid_parallel_mixer_encoder"


def rms_norm(x, g):
    xf = x.astype(jnp.float32)
    y = xf * lax.rsqrt(jnp.mean(xf * xf, axis=-1, keepdims=True) + EPS)
    return (y * g.astype(jnp.float32)).astype(x.dtype)


def layer_norm(x, g, b):
    xf = x.astype(jnp.float32)
    mu = jnp.mean(xf, axis=-1, keepdims=True)
    var = jnp.mean(jnp.square(xf - mu), axis=-1, keepdims=True)
    y = (xf - mu) * lax.rsqrt(var + EPS) * g.astype(jnp.float32) + b.astype(jnp.float32)
    return y.astype(x.dtype)


def split_heads(t):
    return t.reshape(t.shape[0], t.shape[1], -1, HEAD_DIM)


def t5_bucket(rel):
    nb = REL_BUCKETS // 2
    ret = jnp.where(rel > 0, nb, 0)
    n = jnp.abs(rel)
    max_exact = nb // 2
    nf = jnp.maximum(n, 1).astype(jnp.float32)
    large = max_exact + (jnp.log(nf / max_exact) / math.log(REL_MAX_DIST / max_exact)
                         * (nb - max_exact)).astype(jnp.int32)
    large = jnp.minimum(large, nb - 1)
    return ret + jnp.where(n < max_exact, n, large)


def rel_bias_pattern(table, block, radius, dil):
    kw = block + 2 * radius
    rel = (jnp.arange(kw)[None, :] - radius - jnp.arange(block)[:, None]) * dil
    return jnp.transpose(table.astype(jnp.float32)[t5_bucket(rel)], (2, 0, 1))


def banded_attention(q, k, v, radius, block, bias, sink=None):
    n, l, h, dh = q.shape
    g = k.shape[2]
    rep = h // g
    nblk = -(-l // block)
    lp = nblk * block
    kw = block + 2 * radius
    qp = jnp.pad(q, ((0, 0), (0, lp - l), (0, 0), (0, 0)))
    kp = jnp.pad(k, ((0, 0), (radius, lp - l + radius), (0, 0), (0, 0)))
    vp = jnp.pad(v, ((0, 0), (radius, lp - l + radius), (0, 0), (0, 0)))
    idx = jnp.arange(nblk)[:, None] * block + jnp.arange(kw)[None, :]
    kb = kp[:, idx]
    vb = vp[:, idx].astype(jnp.float32)
    qb = qp.reshape(n, nblk, block, g, rep, dh)
    s = jnp.einsum('nbqgrd,nbkgd->nbgrqk', qb, kb, preferred_element_type=jnp.float32) * ATTN_SCALE
    s = s + bias.reshape(g, rep, block, kw)
    key_pos = idx - radius
    key_ok = (key_pos >= 0) & (key_pos < l)
    rel = jnp.arange(kw)[None, :] - radius - jnp.arange(block)[:, None]
    mask = key_ok[:, None, :] & (jnp.abs(rel) <= radius)[None]
    s = jnp.where(mask[None, :, None, None], s, NEG_INF)
    m = jnp.max(s, axis=-1, keepdims=True)
    if sink is not None:
        sk = sink.astype(jnp.float32).reshape(g, rep, 1, 1)
        m = jnp.maximum(m, sk)
    pr = jnp.exp(s - m)
    den = jnp.sum(pr, axis=-1, keepdims=True)
    if sink is not None:
        den = den + jnp.exp(sk - m)
    o = jnp.einsum('nbgrqk,nbkgd->nbqgrd', pr / den, vb)
    o = o.reshape(n, lp, h, dh)[:, :l]
    lse = jnp.transpose((m + jnp.log(den))[..., 0], (0, 1, 4, 2, 3)).reshape(n, lp, h)[:, :l]
    return o.astype(q.dtype), lse


def dilated_attention(q, k, v, table):
    b, s, h, dh = q.shape
    outs, lses = [], []
    for window, dil in DILATED_CFGS:
        radius = window // (2 * dil)
        ls = s // dil

        def gather(t):
            return jnp.transpose(t.reshape(b, ls, dil, h, dh), (0, 2, 1, 3, 4)).reshape(b * dil, ls, h, dh)

        bias = rel_bias_pattern(table, DIL_BLOCK, radius, dil)
        o, lse = banded_attention(gather(q), gather(k), gather(v), radius, DIL_BLOCK, bias)
        outs.append(jnp.transpose(o.reshape(b, dil, ls, h, dh), (0, 2, 1, 3, 4)).reshape(b, s, h, dh))
        lses.append(jnp.transpose(lse.reshape(b, dil, ls, h), (0, 2, 1, 3)).reshape(b, s, h))
    w = jax.nn.softmax(jnp.stack(lses, axis=0), axis=0)
    o = jnp.sum(w[..., None] * jnp.stack(outs, axis=0).astype(jnp.float32), axis=0)
    return o.astype(q.dtype)


def rope_1d(x, pos):
    dim = x.shape[-1]
    inv = ROPE_THETA ** (-jnp.arange(0, dim, 2, dtype=jnp.float32) / dim)
    ang = pos.astype(jnp.float32)[:, None] * inv[None, :]
    cos = jnp.cos(ang)[:, None, :]
    sin = jnp.sin(ang)[:, None, :]
    xf = x.astype(jnp.float32)
    x1, x2 = xf[..., :dim // 2], xf[..., dim // 2:]
    return jnp.concatenate([x1 * cos - x2 * sin, x2 * cos + x1 * sin], axis=-1).astype(x.dtype)


def axial_rope(x, row, col):
    half = x.shape[-1] // 2
    return jnp.concatenate([rope_1d(x[..., :half], row), rope_1d(x[..., half:], col)], axis=-1)


def dense_gqa_blocks(q, k, v):
    b, s, h, dh = q.shape
    g = k.shape[2]
    rep = h // g
    nqb = s // DENSE_BLOCK
    qb = jnp.moveaxis(q.reshape(b, nqb, DENSE_BLOCK, g, rep, dh), 1, 0)
    vf = v.astype(jnp.float32)

    def one(qblk):
        sc = jnp.einsum('bqgrd,bkgd->bgrqk', qblk, k, preferred_element_type=jnp.float32) * ATTN_SCALE
        pr = jax.nn.softmax(sc, axis=-1)
        return jnp.einsum('bgrqk,bkgd->bqgrd', pr, vf)

    o = lax.map(one, qb)
    return jnp.moveaxis(o, 0, 1).reshape(b, s, h, dh).astype(q.dtype)


def spatial_gating(u, v, ln_g, ln_b, ws, bs):
    b, s, c = u.shape
    u = jax.nn.gelu(u)
    v = layer_norm(jax.nn.gelu(v), ln_g, ln_b)
    vc = v.reshape(b, s // C_CHUNK, C_CHUNK, C_GROUPS, c // C_GROUPS)
    mixed = jnp.einsum('gpq,bnqgc->bnpgc', ws, vc) + jnp.transpose(bs)[:, :, None]
    return u * mixed.reshape(b, s, c)


def depthwise_conv(h, w, bias):
    c = h.shape[-1]
    y = lax.conv_general_dilated(h, w[:, None, :].astype(h.dtype), window_strides=(1,),
                                 padding=((CONV_WIDTH // 2, CONV_WIDTH // 2),),
                                 dimension_numbers=('NWC', 'WIO', 'NWC'), feature_group_count=c)
    return y + bias


def conv_gated_ffn(x, w_up, conv_w, conv_b, w_down):
    h = depthwise_conv(x @ w_up, conv_w, conv_b)
    gt, up = jnp.split(h, 2, axis=-1)
    return (jax.nn.silu(gt) * up) @ w_down


def setup_inputs(seed: int = 0) -> dict:
    key = jax.random.key(seed)
    ks = jax.random.split(key, 24)

    def nrm(k, shape, scale):
        return jax.random.normal(k, shape, jnp.float32) * scale

    return {
        'x': nrm(ks[0], (BATCH, SEQ, D_MODEL), 1.0),
        'p': nrm(ks[1], (DEPTH, BATCH, SEQ, PLE_DIM), 1.0),
        'rel_bias': nrm(ks[2], (REL_BUCKETS, A_HEADS + B_Q_HEADS), 0.3),
        'ln_mix_g': 1.0 + nrm(ks[3], (DEPTH, D_MODEL), 0.02),
        'w_in': nrm(ks[4], (DEPTH, D_MODEL, IN_WIDTH), D_MODEL ** -0.5),
        'qk_gain': 1.0 + nrm(ks[5], (DEPTH, 3, 2, HEAD_DIM), 0.02),
        'sink': nrm(ks[6], (DEPTH, B_Q_HEADS), 0.5),
        'c_norm_g': 1.0 + nrm(ks[7], (DEPTH, GROUP_WIDTH), 0.02),
        'c_norm_b': nrm(ks[8], (DEPTH, GROUP_WIDTH), 0.02),
        'c_ws': nrm(ks[9], (DEPTH, C_GROUPS, C_CHUNK, C_CHUNK), C_CHUNK ** -0.5),
        'c_bs': 1.0 + nrm(ks[10], (DEPTH, C_GROUPS, C_CHUNK), 0.02),
        'out_gain': 1.0 + nrm(ks[11], (DEPTH, N_MIXERS, GROUP_WIDTH), 0.02),
        'w_out': nrm(ks[12], (DEPTH, MIX_WIDTH, D_MODEL), MIX_WIDTH ** -0.5),
        'ln_ffn_g': 1.0 + nrm(ks[13], (DEPTH, D_MODEL), 0.02),
        'w_up': nrm(ks[14], (DEPTH, D_MODEL, 2 * D_FF), D_MODEL ** -0.5),
        'conv_w': nrm(ks[15], (DEPTH, CONV_WIDTH, 2 * D_FF), CONV_WIDTH ** -0.5),
        'conv_b': nrm(ks[16], (DEPTH, 2 * D_FF), 0.02),
        'w_down': nrm(ks[17], (DEPTH, D_FF, D_MODEL), D_FF ** -0.5),
        'ln_ple_g': 1.0 + nrm(ks[18], (DEPTH, D_MODEL), 0.02),
        'w_ple_gate': nrm(ks[19], (DEPTH, D_MODEL, D_MODEL), D_MODEL ** -0.5),
        'w_ple_proj': nrm(ks[20], (DEPTH, PLE_DIM, D_MODEL), PLE_DIM ** -0.5),
    }


def reference(x, p, rel_bias, ln_mix_g, w_in, qk_gain, sink, c_norm_g, c_norm_b, c_ws, c_bs,
              out_gain, w_out, ln_ffn_g, w_up, conv_w, conv_b, w_down, ln_ple_g, w_ple_gate,
              w_ple_proj):
    b, s, _ = x.shape
    rows = s // GRID_W
    row_idx = jnp.repeat(jnp.arange(rows), GRID_W)
    col_idx = jnp.tile(jnp.arange(GRID_W), rows)
    table_a = rel_bias[:, :A_HEADS]
    table_b = rel_bias[:, A_HEADS:]
    bias_b = rel_bias_pattern(table_b, SWA_BLOCK, SWA_RADIUS, 1)
    for i in range(DEPTH):
        hn = rms_norm(x, ln_mix_g[i])
        proj = hn @ w_in[i]
        a_q, a_k, a_v, b_q, b_k, b_v, c_u, c_v, d_q, d_k, d_v = jnp.split(proj, SPLIT_POINTS, axis=-1)
        a_q = rms_norm(split_heads(a_q), qk_gain[i, 0, 0])
        a_k = rms_norm(split_heads(a_k), qk_gain[i, 0, 1])
        y_a = dilated_attention(a_q, a_k, split_heads(a_v), table_a).reshape(b, s, GROUP_WIDTH)
        b_q = rms_norm(split_heads(b_q), qk_gain[i, 1, 0])
        b_k = rms_norm(split_heads(b_k), qk_gain[i, 1, 1])
        y_b, _ = banded_attention(b_q, b_k, split_heads(b_v), SWA_RADIUS, SWA_BLOCK, bias_b, sink[i])
        y_b = y_b.reshape(b, s, GROUP_WIDTH)
        y_c = spatial_gating(c_u, c_v, c_norm_g[i], c_norm_b[i], c_ws[i], c_bs[i])
        d_q = axial_rope(rms_norm(split_heads(d_q), qk_gain[i, 2, 0]), row_idx, col_idx)
        d_k = axial_rope(rms_norm(split_heads(d_k), qk_gain[i, 2, 1]), row_idx, col_idx)
        y_d = dense_gqa_blocks(d_q, d_k, split_heads(d_v)).reshape(b, s, GROUP_WIDTH)
        groups = (y_a, y_b, y_c, y_d)
        mixed = jnp.concatenate([rms_norm(y, out_gain[i, m]) for m, y in enumerate(groups)], axis=-1)
        x = x + mixed @ w_out[i]
        x = x + conv_gated_ffn(rms_norm(x, ln_ffn_g[i]), w_up[i], conv_w[i], conv_b[i], w_down[i])
        gate = jax.nn.sigmoid(rms_norm(x, ln_ple_g[i]) @ w_ple_gate[i])
        x = x + (p[i] @ w_ple_proj[i]) * gate
    return x
```

```python
import functools
import math

import jax
import jax.numpy as jnp
import numpy as np
from jax import lax
from jax.experimental import pallas as pl
from jax.experimental.pallas import tpu as pltpu

D_MODEL = 1024
HEAD_DIM = 64
GROUP_WIDTH = 256
N_HEADS = GROUP_WIDTH // HEAD_DIM
C_GROUPS = 4
C_CHUNK = 128
DILATED_CFGS = ((128, 1), (512, 4), (2048, 16))
SWA_RADIUS = 128
GRID_W = 64
ROPE_THETA = 10000.0
REL_BUCKETS = 32
REL_MAX_DIST = 1024
D_FF = 2816
PLE_DIM = 256
EPS = 1e-6
NEG_INF = -1e30
ATTN_SCALE = HEAD_DIM ** -0.5

BF16 = jnp.bfloat16
F32 = jnp.float32

SEC_AQ, SEC_AK, SEC_AV, SEC_BQ, SEC_BK, SEC_BV, SEC_YC, SEC_DQ, SEC_DK, SEC_DV = range(10)
N_SEC = 10

VMEM_LIMIT = 56 * 1024 * 1024

KIN_TM = 512
BAND_TQ = 128
DENSE_TQ = 256
MIX_TM = 512
FFN_TM = 1024
FFN_TF = 256
HALO = 16


def _params(sem):
    return pltpu.CompilerParams(dimension_semantics=sem, vmem_limit_bytes=VMEM_LIMIT)


def _rms(t, gain):
    return t * lax.rsqrt(jnp.mean(t * t, axis=-1, keepdims=True) + EPS) * gain


def _head_id(shape):
    return lax.broadcasted_iota(jnp.int32, shape, 1) >> 6


def _proj_kernel(x_ref, g_ref, w_ref, vec_ref, cos_ref, sin_ref, bd_ref, wcat_ref, bsx_ref, o_ref):
    x = x_ref[...]
    hn = _rms(x, g_ref[...]).astype(BF16)
    bd = bd_ref[...]

    def proj(sec):
        return jnp.dot(hn, w_ref[:, sec * GROUP_WIDTH:(sec + 1) * GROUP_WIDTH],
                       preferred_element_type=F32)

    def store(sec, val):
        o_ref[:, sec * GROUP_WIDTH:(sec + 1) * GROUP_WIDTH] = val.astype(BF16)

    def head_rms(t, row):
        sq = t * t
        hi = sq.astype(BF16)
        lo = (sq - hi.astype(F32)).astype(BF16)
        ss = (jnp.dot(hi, bd, preferred_element_type=F32) + jnp.dot(lo, bd, preferred_element_type=F32))
        return t * lax.rsqrt(ss * (1.0 / HEAD_DIM) + EPS) * vec_ref[row:row + 1, :]

    def rope(t):
        halves = []
        for c0 in range(0, GROUP_WIDTH, 128):
            th = t[:, c0:c0 + 128]
            lane = lax.broadcasted_iota(jnp.int32, th.shape, 1)
            partner = jnp.where((lane & 31) < 16, pltpu.roll(th, 128 - 16, axis=1), pltpu.roll(th, 16, axis=1))
            halves.append(th * cos_ref[:, c0:c0 + 128] + partner * sin_ref[:, c0:c0 + 128])
        return jnp.concatenate(halves, axis=-1)

    store(SEC_AQ, head_rms(proj(0), 0))
    store(SEC_AK, head_rms(proj(1), 1))
    store(SEC_AV, proj(2))
    store(SEC_BQ, head_rms(proj(3), 2))
    store(SEC_BK, head_rms(proj(4), 3))
    store(SEC_BV, proj(5))
    store(SEC_DQ, rope(head_rms(proj(8), 4)))
    store(SEC_DK, rope(head_rms(proj(9), 5)))
    store(SEC_DV, proj(10))

    u = jax.nn.gelu(proj(6))
    v = jax.nn.gelu(proj(7))
    mu = jnp.mean(v, axis=-1, keepdims=True)
    vc = v - mu
    var = jnp.mean(vc * vc, axis=-1, keepdims=True)
    vn = (vc * lax.rsqrt(var + EPS) * vec_ref[6:7, :] + vec_ref[7:8, :]).astype(BF16)
    gid = _head_id((C_CHUNK, GROUP_WIDTH))
    zero = jnp.zeros((C_CHUNK, GROUP_WIDTH), BF16)
    for c in range(x.shape[0] // C_CHUNK):
        rows = slice(c * C_CHUNK, (c + 1) * C_CHUNK)
        vchunk = vn[rows]
        stacked = jnp.concatenate([jnp.where(gid == g, vchunk, zero) for g in range(C_GROUPS)], axis=0)
        mixed = jnp.dot(wcat_ref[...], stacked, preferred_element_type=F32) + bsx_ref[...]
        yc = _rms(u[rows] * mixed, vec_ref[8:9, :])
        o_ref[rows, SEC_YC * GROUP_WIDTH:(SEC_YC + 1) * GROUP_WIDTH] = yc.astype(BF16)


def _proj_call(x2d, g, w, vec, cos_t, sin_t, bd, wcat, bsx, seq):
    n = x2d.shape[0]
    tiles_per_seq = seq // KIN_TM
    const = lambda i: (0, 0)
    return pl.pallas_call(
        _proj_kernel,
        grid=(n // KIN_TM,),
        in_specs=[
            pl.BlockSpec((KIN_TM, D_MODEL), lambda i: (i, 0)),
            pl.BlockSpec((1, D_MODEL), const),
            pl.BlockSpec(w.shape, const),
            pl.BlockSpec(vec.shape, const),
            pl.BlockSpec((KIN_TM, GROUP_WIDTH), lambda i: (i % tiles_per_seq, 0)),
            pl.BlockSpec((KIN_TM, GROUP_WIDTH), lambda i: (i % tiles_per_seq, 0)),
            pl.BlockSpec(bd.shape, const),
            pl.BlockSpec(wcat.shape, const),
            pl.BlockSpec(bsx.shape, const),
        ],
        out_specs=pl.BlockSpec((KIN_TM, N_SEC * GROUP_WIDTH), lambda i: (i, 0)),
        out_shape=jax.ShapeDtypeStruct((n, N_SEC * GROUP_WIDTH), BF16),
        compiler_params=_params(("parallel",)),
        name="proj",
    )(x2d, g, w, vec, cos_t, sin_t, bd, wcat, bsx)


def _band_kernel(*refs, radius, length, kw, n_tiles, has_sink, emit_lse):
    if has_sink:
        q_ref, k_ref, v_ref, bias_ref, sink_ref, gain_ref = refs[:6]
        outs = refs[6:]
    else:
        q_ref, k_ref, v_ref, bias_ref = refs[:4]
        outs = refs[4:]
    j = pl.program_id(2)
    tq = q_ref.shape[0]
    if n_tiles == 1:
        ks = 0
        case = 0
    else:
        ks = pl.multiple_of(jnp.clip(j * tq - radius, 0, length - kw), 64)
        case = jnp.where(j == 0, 0, jnp.where(j == n_tiles - 1, 2, 1))
    q = q_ref[...]
    kwin = k_ref[pl.ds(ks, kw), :]
    vwin = v_ref[pl.ds(ks, kw), :]
    hid = _head_id((tq, GROUP_WIDTH))
    qzero = jnp.zeros_like(q)
    acc = jnp.zeros((tq, GROUP_WIDTH), F32)
    lse = jnp.zeros((tq, GROUP_WIDTH), F32)
    for h in range(N_HEADS):
        qh = jnp.where(hid == h, q, qzero)
        s = lax.dot_general(qh, kwin, (((1,), (1,)), ((), ())), preferred_element_type=F32)
        s = s + bias_ref[case, h]
        m = jnp.max(s, axis=-1, keepdims=True)
        if has_sink:
            m = jnp.maximum(m, sink_ref[h])
        p = jnp.exp(s - m)
        den = jnp.sum(p, axis=-1, keepdims=True)
        if has_sink:
            den = den + jnp.exp(sink_ref[h] - m)
        oh = jnp.dot(p.astype(BF16), vwin, preferred_element_type=F32) * (1.0 / den)
        acc = jnp.where(hid == h, oh, acc)
        if emit_lse:
            lse = jnp.where(hid == h, m + jnp.log(den), lse)
    if emit_lse:
        outs[0][...] = acc.astype(BF16)
        outs[1][...] = lse
    else:
        outs[0][...] = _rms(acc, gain_ref[...]).astype(BF16)


def _band_call(proj3d, sec_q, dil, radius, bias, sink=None, gain=None):
    b, length, _ = proj3d.shape
    tq = min(BAND_TQ, length)
    n_tiles = length // tq
    kw = min(tq + 2 * radius, length)
    has_sink = sink is not None
    qmap = lambda bi, r, j: (bi, j, r * N_SEC + sec_q)
    kmap = lambda bi, r, j: (bi, 0, r * N_SEC + sec_q + 1)
    vmap = lambda bi, r, j: (bi, 0, r * N_SEC + sec_q + 2)
    in_specs = [
        pl.BlockSpec((None, tq, GROUP_WIDTH), qmap),
        pl.BlockSpec((None, length, GROUP_WIDTH), kmap),
        pl.BlockSpec((None, length, GROUP_WIDTH), vmap),
        pl.BlockSpec(bias.shape, lambda bi, r, j: (0, 0, 0, 0)),
    ]
    args = [proj3d, proj3d, proj3d, bias]
    omap = lambda bi, r, j: (bi, j, r)
    oshape = (b, length, dil * GROUP_WIDTH)
    if has_sink:
        in_specs += [pl.BlockSpec(memory_space=pltpu.SMEM),
                     pl.BlockSpec((1, GROUP_WIDTH), lambda bi, r, j: (0, 0))]
        args += [sink, gain]
        out_specs = pl.BlockSpec((None, tq, GROUP_WIDTH), omap)
        out_shape = jax.ShapeDtypeStruct(oshape, BF16)
    else:
        out_specs = [pl.BlockSpec((None, tq, GROUP_WIDTH), omap)] * 2
        out_shape = [jax.ShapeDtypeStruct(oshape, BF16), jax.ShapeDtypeStruct(oshape, F32)]
    kern = functools.partial(_band_kernel, radius=radius, length=length, kw=kw, n_tiles=n_tiles,
                             has_sink=has_sink, emit_lse=not has_sink)
    return pl.pallas_call(
        kern,
        grid=(b, dil, n_tiles),
        in_specs=in_specs,
        out_specs=out_specs,
        out_shape=out_shape,
        compiler_params=_params(("parallel", "parallel", "arbitrary")),
        name="band_sink" if has_sink else f"band_dil{dil}",
    )(*args)


def _dense_kernel(q_ref, k_ref, v_ref, gain_ref, o_ref):
    q = q_ref[...]
    k = k_ref[...]
    v = v_ref[...]
    tq = q.shape[0]
    hid = _head_id((tq, GROUP_WIDTH))
    qzero = jnp.zeros_like(q)
    acc = jnp.zeros((tq, GROUP_WIDTH), F32)
    for h in range(N_HEADS):
        qh = jnp.where(hid == h, q, qzero)
        s = lax.dot_general(qh, k, (((1,), (1,)), ((), ())), preferred_element_type=F32)
        m = jnp.max(s, axis=-1, keepdims=True)
        p = jnp.exp(s - m)
        den = jnp.sum(p, axis=-1, keepdims=True)
        oh = jnp.dot(p.astype(BF16), v, preferred_element_type=F32) * (1.0 / den)
        acc = jnp.where(hid == h, oh, acc)
    o_ref[...] = _rms(acc, gain_ref[...]).astype(BF16)


def _dense_call(proj3d, gain):
    b, seq, _ = proj3d.shape
    return pl.pallas_call(
        _dense_kernel,
        grid=(b, seq // DENSE_TQ),
        in_specs=[
            pl.BlockSpec((None, DENSE_TQ, GROUP_WIDTH), lambda bi, j: (bi, j, SEC_DQ)),
            pl.BlockSpec((None, seq, GROUP_WIDTH), lambda bi, j: (bi, 0, SEC_DK)),
            pl.BlockSpec((None, seq, GROUP_WIDTH), lambda bi, j: (bi, 0, SEC_DV)),
            pl.BlockSpec((1, GROUP_WIDTH), lambda bi, j: (0, 0)),
        ],
        out_specs=pl.BlockSpec((None, DENSE_TQ, GROUP_WIDTH), lambda bi, j: (bi, j, 0)),
        out_shape=jax.ShapeDtypeStruct((b, seq, GROUP_WIDTH), BF16),
        compiler_params=_params(("parallel", "arbitrary")),
        name="dense",
    )(proj3d, proj3d, proj3d, gain)


def _mix_kernel(x_ref, o1, o2, o3, l1, l2, l3, yb_ref, yc_ref, yd_ref, ga_ref, w_ref, g_ref,
                x1_ref, xn_ref):
    la, lb, lc = l1[...], l2[...], l3[...]
    mx = jnp.maximum(jnp.maximum(la, lb), lc)
    ea, eb, ec = jnp.exp(la - mx), jnp.exp(lb - mx), jnp.exp(lc - mx)
    ya = (ea * o1[...].astype(F32) + eb * o2[...].astype(F32) + ec * o3[...].astype(F32)) / (ea + eb + ec)
    ya = _rms(ya, ga_ref[...]).astype(BF16)
    mixed = jnp.concatenate([ya, yb_ref[...], yc_ref[...], yd_ref[...]], axis=-1)
    x1 = x_ref[...] + jnp.dot(mixed, w_ref[...], preferred_element_type=F32)
    x1_ref[...] = x1
    xn_ref[...] = _rms(x1, g_ref[...]).astype(BF16)


def _mix_call(x2d, oa, la, yb, proj2d, yd, gain_a, w_out, g_ffn):
    n = x2d.shape[0]
    row = lambda i: (i, 0)
    const = lambda i: (0, 0)
    grp = pl.BlockSpec((MIX_TM, GROUP_WIDTH), row)
    return pl.pallas_call(
        _mix_kernel,
        grid=(n // MIX_TM,),
        in_specs=[pl.BlockSpec((MIX_TM, D_MODEL), row), grp, grp, grp, grp, grp, grp, grp,
                  pl.BlockSpec((MIX_TM, GROUP_WIDTH), lambda i: (i, SEC_YC)), grp,
                  pl.BlockSpec((1, GROUP_WIDTH), const),
                  pl.BlockSpec(w_out.shape, const),
                  pl.BlockSpec((1, D_MODEL), const)],
        out_specs=[pl.BlockSpec((MIX_TM, D_MODEL), row), pl.BlockSpec((MIX_TM, D_MODEL), row)],
        out_shape=[jax.ShapeDtypeStruct((n, D_MODEL), F32), jax.ShapeDtypeStruct((n, D_MODEL), BF16)],
        compiler_params=_params(("parallel",)),
        name="mix_out",
    )(x2d, oa[0], oa[1], oa[2], la[0], la[1], la[2], yb, proj2d, yd, gain_a, w_out, g_ffn)


def _ffn_kernel(xn_ref, prev_ref, next_ref, x1_ref, p_ref, wg_ref, wu_ref, cwg_ref, cwu_ref, wd_ref,
                gple_ref, wgate_ref, wproj_ref, o_ref, xcat_ref, acc_ref, *, tiles_per_seq):
    i = pl.program_id(0)
    c = pl.program_id(1)
    tm = xn_ref.shape[0]

    @pl.when(c == 0)
    def _():
        pos = i % tiles_per_seq
        xcat_ref[0:tm, :] = xn_ref[...]
        xcat_ref[tm:tm + HALO, :] = jnp.where(pos == 0, jnp.zeros_like(prev_ref[...]), prev_ref[...])
        xcat_ref[tm + HALO:tm + 2 * HALO, :] = jnp.where(pos == tiles_per_seq - 1,
                                                       jnp.zeros_like(next_ref[...]), next_ref[...])
        acc_ref[...] = jnp.zeros_like(acc_ref)

    xcat = xcat_ref[...]
    row = lax.broadcasted_iota(jnp.int32, (tm, FFN_TF), 0)

    def conv(w_ref, cw_ref):
        h = jnp.dot(xcat, w_ref[...], preferred_element_type=F32)
        hm = h[0:tm]
        before = jnp.where(row == 0, h[tm + HALO - 1:tm + HALO], pltpu.roll(hm, 1, axis=0))
        after = jnp.where(row == tm - 1, h[tm + HALO:tm + HALO + 1], pltpu.roll(hm, tm - 1, axis=0))
        return before * cw_ref[0:1, :] + hm * cw_ref[1:2, :] + after * cw_ref[2:3, :] + cw_ref[3:4, :]

    gt = conv(wg_ref, cwg_ref)
    up = conv(wu_ref, cwu_ref)
    act = (gt / (1.0 + jnp.exp(-gt)) * up).astype(BF16)
    acc_ref[...] += jnp.dot(act, wd_ref[...], preferred_element_type=F32)

    @pl.when(c == pl.num_programs(1) - 1)
    def _():
        x2 = x1_ref[...] + acc_ref[...]
        z = jnp.dot(_rms(x2, gple_ref[...]).astype(BF16), wgate_ref[...], preferred_element_type=F32)
        gate = 1.0 / (1.0 + jnp.exp(-z))
        inj = jnp.dot(p_ref[...].astype(BF16), wproj_ref[...], preferred_element_type=F32)
        o_ref[...] = x2 + inj * gate


def _ffn_call(xn, x1, p2d, w_up, cw, w_down, g_ple, w_gate, w_proj, seq):
    n = xn.shape[0]
    n_chunks = D_FF // FFN_TF
    halo_per_tile = FFN_TM // HALO
    n_halo = n // HALO
    row = lambda i, c: (i, 0)
    const = lambda i, c: (0, 0)
    kern = functools.partial(_ffn_kernel, tiles_per_seq=seq // FFN_TM)
    return pl.pallas_call(
        kern,
        grid=(n // FFN_TM, n_chunks),
        in_specs=[
            pl.BlockSpec((FFN_TM, D_MODEL), row),
            pl.BlockSpec((HALO, D_MODEL), lambda i, c: (jnp.maximum(i * halo_per_tile - 1, 0), 0)),
            pl.BlockSpec((HALO, D_MODEL), lambda i, c: (jnp.minimum((i + 1) * halo_per_tile, n_halo - 1), 0)),
            pl.BlockSpec((FFN_TM, D_MODEL), row),
            pl.BlockSpec((FFN_TM, PLE_DIM), row),
            pl.BlockSpec((D_MODEL, FFN_TF), lambda i, c: (0, c)),
            pl.BlockSpec((D_MODEL, FFN_TF), lambda i, c: (0, c + n_chunks)),
            pl.BlockSpec((4, FFN_TF), lambda i, c: (0, c)),
            pl.BlockSpec((4, FFN_TF), lambda i, c: (0, c + n_chunks)),
            pl.BlockSpec((FFN_TF, D_MODEL), lambda i, c: (c, 0)),
            pl.BlockSpec((1, D_MODEL), const),
            pl.BlockSpec(w_gate.shape, const),
            pl.BlockSpec(w_proj.shape, const),
        ],
        out_specs=pl.BlockSpec((FFN_TM, D_MODEL), row),
        out_shape=jax.ShapeDtypeStruct((n, D_MODEL), F32),
        scratch_shapes=[pltpu.VMEM((FFN_TM + 2 * HALO, D_MODEL), BF16),
                        pltpu.VMEM((FFN_TM, D_MODEL), F32)],
        compiler_params=_params(("parallel", "arbitrary")),
        name="ffn_ple",
    )(xn, xn, xn, x1, p2d, w_up, w_up, cw, cw, w_down, g_ple, w_gate, w_proj)


def _t5_bucket(rel):
    nb = REL_BUCKETS // 2
    ret = jnp.where(rel > 0, nb, 0)
    n = jnp.abs(rel)
    max_exact = nb // 2
    nf = jnp.maximum(n, 1).astype(F32)
    large = max_exact + (jnp.log(nf / max_exact) / math.log(REL_MAX_DIST / max_exact)
                         * (nb - max_exact)).astype(jnp.int32)
    large = jnp.minimum(large, nb - 1)
    return ret + jnp.where(n < max_exact, n, large)


def _band_bias(table, radius, dil, tq, kw):
    cases = []
    for off in (0, radius, kw - tq):
        rel = jnp.arange(kw)[None, :] - jnp.arange(tq)[:, None] - off
        bias = jnp.transpose(table.astype(F32)[_t5_bucket(rel * dil)], (2, 0, 1))
        cases.append(jnp.where((jnp.abs(rel) <= radius)[None], bias, NEG_INF))
    return jnp.stack(cases, axis=0)


def _rope_tables(seq):
    half = HEAD_DIM // 2
    inv = ROPE_THETA ** (-jnp.arange(0, half, 2, dtype=F32) / half)
    t = jnp.arange(seq)
    ang_r = (t // GRID_W).astype(F32)[:, None] * inv[None, :]
    ang_c = (t % GRID_W).astype(F32)[:, None] * inv[None, :]
    cos_h = jnp.concatenate([jnp.cos(ang_r)] * 2 + [jnp.cos(ang_c)] * 2, axis=-1)
    sin_h = jnp.concatenate([-jnp.sin(ang_r), jnp.sin(ang_r), -jnp.sin(ang_c), jnp.sin(ang_c)], axis=-1)
    return jnp.tile(cos_h, (1, N_HEADS)), jnp.tile(sin_h, (1, N_HEADS))


def _w_in_columns():
    a = np.arange
    rep = lambda start: np.concatenate([start + (h // 2) * HEAD_DIM + a(HEAD_DIM) for h in range(N_HEADS)])
    cols = [a(0, 768), a(768, 1024), rep(1024), rep(1152), a(1280, 1792), a(1792, 2048), rep(2048), rep(2176)]
    return np.concatenate(cols)


def _tile_heads(vec, reps):
    return jnp.tile(vec, reps)


def kernel(x, p, rel_bias, ln_mix_g, w_in, qk_gain, sink, c_norm_g, c_norm_b, c_ws, c_bs, out_gain, w_out,
           ln_ffn_g, w_up, conv_w, conv_b, w_down, ln_ple_g, w_ple_gate, w_ple_proj):
    b, s, _ = x.shape
    n = b * s
    depth = w_in.shape[0]
    cos_t, sin_t = _rope_tables(s)
    cols = _w_in_columns()
    bd = (np.arange(GROUP_WIDTH)[:, None] // HEAD_DIM == np.arange(GROUP_WIDTH)[None, :] // HEAD_DIM)
    bd = jnp.asarray(bd, BF16)
    table_a = rel_bias[:, :N_HEADS]
    table_b = rel_bias[:, N_HEADS:]
    bias_a = []
    for window, dil in DILATED_CFGS:
        radius = window // (2 * dil)
        length = s // dil
        tq = min(BAND_TQ, length)
        bias_a.append(_band_bias(table_a, radius, dil, tq, min(tq + 2 * radius, length)))
    tq_b = min(BAND_TQ, s)
    bias_b = _band_bias(table_b, SWA_RADIUS, 1, tq_b, min(tq_b + 2 * SWA_RADIUS, s))

    x2d = x.reshape(n, D_MODEL)
    for i in range(depth):
        w = w_in[i][:, cols].astype(BF16)
        qg = qk_gain[i]
        vec = jnp.stack([
            jnp.tile(qg[0, 0], N_HEADS) * ATTN_SCALE, jnp.tile(qg[0, 1], N_HEADS),
            jnp.tile(qg[1, 0], N_HEADS) * ATTN_SCALE, jnp.tile(qg[1, 1], N_HEADS),
            jnp.tile(qg[2, 0], N_HEADS) * ATTN_SCALE, jnp.tile(qg[2, 1], N_HEADS),
            c_norm_g[i], c_norm_b[i], out_gain[i, 2],
        ] + [jnp.zeros((GROUP_WIDTH,), F32)] * 7, axis=0)
        wcat = jnp.concatenate([c_ws[i, g] for g in range(C_GROUPS)], axis=1).astype(BF16)
        bsx = jnp.repeat(jnp.transpose(c_bs[i]), GROUP_WIDTH // C_GROUPS, axis=1)
        proj2d = _proj_call(x2d, ln_mix_g[i][None], w, vec, cos_t, sin_t, bd, wcat, bsx, s)

        oa, la = [], []
        for (window, dil), bias in zip(DILATED_CFGS, bias_a):
            view = proj2d.reshape(b, s // dil, dil * N_SEC * GROUP_WIDTH)
            o_c, l_c = _band_call(view, SEC_AQ, dil, window // (2 * dil), bias)
            oa.append(o_c.reshape(n, GROUP_WIDTH))
            la.append(l_c.reshape(n, GROUP_WIDTH))
        proj3d = proj2d.reshape(b, s, N_SEC * GROUP_WIDTH)
        yb = _band_call(proj3d, SEC_BQ, 1, SWA_RADIUS, bias_b, sink[i], out_gain[i, 1][None])
        yd = _dense_call(proj3d, out_gain[i, 3][None])

        x1, xn = _mix_call(x2d, oa, la, yb.reshape(n, GROUP_WIDTH), proj2d, yd.reshape(n, GROUP_WIDTH),
                           out_gain[i, 0][None], w_out[i].astype(BF16), ln_ffn_g[i][None])
        cw = jnp.concatenate([conv_w[i], conv_b[i][None]], axis=0)
        x2d = _ffn_call(xn, x1, p[i].reshape(n, PLE_DIM), w_up[i].astype(BF16), cw, w_down[i].astype(BF16),
                        ln_ple_g[i][None], w_ple_gate[i].astype(BF16), w_ple_proj[i].astype(BF16), s)
    return x2d.reshape(b, s, D_MODEL)
```

```python
import functools
import math

import jax
import jax.numpy as jnp
import numpy as np
from jax import lax
from jax.experimental import pallas as pl
from jax.experimental.pallas import tpu as pltpu

D_MODEL = 1024
HEAD_DIM = 64
GROUP_WIDTH = 256
N_HEADS = GROUP_WIDTH // HEAD_DIM
C_GROUPS = 4
C_CHUNK = 128
DILATED_CFGS = ((128, 1), (512, 4), (2048, 16))
SWA_RADIUS = 128
GRID_W = 64
ROPE_THETA = 10000.0
REL_BUCKETS = 32
REL_MAX_DIST = 1024
D_FF = 2816
PLE_DIM = 256
EPS = 1e-6
NEG_INF = -1e30
ATTN_SCALE = HEAD_DIM ** -0.5

BF16 = jnp.bfloat16
F32 = jnp.float32

SEC_AQ, SEC_AK, SEC_AV, SEC_BQ, SEC_BK, SEC_BV, SEC_YC, SEC_DQ, SEC_DK, SEC_DV = range(10)
N_SEC = 10
A_WIDTH = 3 * GROUP_WIDTH

VMEM_LIMIT = 56 * 1024 * 1024

KIN_TM = 512
BAND_TQ = 128
BAND_UNROLL = 4
SWA_UNROLL = 2
DENSE_TQ = 256
MIX_TM = 512
FFN_TM = 512
FFN_TF = 256
LANES = 128
HALO = 16
N_CHUNKS = D_FF // FFN_TF


def _params(sem):
    return pltpu.CompilerParams(dimension_semantics=sem, vmem_limit_bytes=VMEM_LIMIT)


def _resident(shape):
    zeros = (0,) * len(shape)
    return pl.BlockSpec(shape, lambda *_: zeros, pipeline_mode=pl.Buffered(1))


def _rms(t, gain):
    return t * lax.rsqrt(jnp.mean(t * t, axis=-1, keepdims=True) + EPS) * gain


def _head_id(shape):
    return lax.broadcasted_iota(jnp.int32, shape, 1) >> 6


def _proj_kernel(x_ref, g_ref, w_ref, vec_ref, cos_ref, sin_ref, bd_ref, wcat_ref, bsx_ref,
                 o_ref, d4_ref, d16_ref, a_ref):
    x = x_ref[...]
    tm = x.shape[0]
    hn = _rms(x, g_ref[...]).astype(BF16)
    bd = bd_ref[...]

    def proj(sec):
        return jnp.dot(hn, w_ref[:, sec * GROUP_WIDTH:(sec + 1) * GROUP_WIDTH],
                       preferred_element_type=F32)

    def store(sec, val):
        o_ref[:, sec * GROUP_WIDTH:(sec + 1) * GROUP_WIDTH] = val.astype(BF16)

    def store_a(sec, val):
        store(sec, val)
        for half in range(GROUP_WIDTH // LANES):
            slot = sec * (GROUP_WIDTH // LANES) + half
            a_ref[slot] = val[:, half * LANES:(half + 1) * LANES]
            cols = slice(slot * LANES, (slot + 1) * LANES)
            for dil, ref in ((4, d4_ref), (16, d16_ref)):
                for r in range(dil):
                    ref[r, :, cols] = a_ref[slot, pl.ds(r, tm // dil, stride=dil), :].astype(BF16)

    def head_rms(t, row):
        sq = t * t
        hi = sq.astype(BF16)
        lo = (sq - hi.astype(F32)).astype(BF16)
        ss = (jnp.dot(hi, bd, preferred_element_type=F32) + jnp.dot(lo, bd, preferred_element_type=F32))
        return t * lax.rsqrt(ss * (1.0 / HEAD_DIM) + EPS) * vec_ref[row:row + 1, :]

    def rope(t):
        halves = []
        for c0 in range(0, GROUP_WIDTH, 128):
            th = t[:, c0:c0 + 128]
            lane = lax.broadcasted_iota(jnp.int32, th.shape, 1)
            partner = jnp.where((lane & 31) < 16, pltpu.roll(th, 128 - 16, axis=1), pltpu.roll(th, 16, axis=1))
            halves.append(th * cos_ref[:, c0:c0 + 128] + partner * sin_ref[:, c0:c0 + 128])
        return jnp.concatenate(halves, axis=-1)

    store_a(SEC_AQ, head_rms(proj(0), 0))
    store_a(SEC_AK, head_rms(proj(1), 1))
    store_a(SEC_AV, proj(2))
    store(SEC_BQ, head_rms(proj(3), 2))
    store(SEC_BK, head_rms(proj(4), 3))
    store(SEC_BV, proj(5))
    store(SEC_DQ, rope(head_rms(proj(8), 4)))
    store(SEC_DK, rope(head_rms(proj(9), 5)))
    store(SEC_DV, proj(10))

    u = jax.nn.gelu(proj(6))
    v = jax.nn.gelu(proj(7))
    mu = jnp.mean(v, axis=-1, keepdims=True)
    vc = v - mu
    var = jnp.mean(vc * vc, axis=-1, keepdims=True)
    vn = (vc * lax.rsqrt(var + EPS) * vec_ref[6:7, :] + vec_ref[7:8, :]).astype(BF16)
    gid = _head_id((C_CHUNK, GROUP_WIDTH))
    zero = jnp.zeros((C_CHUNK, GROUP_WIDTH), BF16)
    for c in range(tm // C_CHUNK):
        rows = slice(c * C_CHUNK, (c + 1) * C_CHUNK)
        vchunk = vn[rows]
        stacked = jnp.concatenate([jnp.where(gid == g, vchunk, zero) for g in range(C_GROUPS)], axis=0)
        mixed = jnp.dot(wcat_ref[...], stacked, preferred_element_type=F32) + bsx_ref[...]
        yc = _rms(u[rows] * mixed, vec_ref[8:9, :])
        o_ref[rows, SEC_YC * GROUP_WIDTH:(SEC_YC + 1) * GROUP_WIDTH] = yc.astype(BF16)


def _proj_call(x2d, g, w, vec, cos_t, sin_t, bd, wcat, bsx, batch, seq):
    n = x2d.shape[0]
    tps = seq // KIN_TM
    return pl.pallas_call(
        _proj_kernel,
        grid=(n // KIN_TM,),
        in_specs=[
            pl.BlockSpec((KIN_TM, D_MODEL), lambda i: (i, 0)),
            _resident((1, D_MODEL)),
            _resident(w.shape),
            _resident(vec.shape),
            pl.BlockSpec((KIN_TM, GROUP_WIDTH), lambda i: (i % tps, 0)),
            pl.BlockSpec((KIN_TM, GROUP_WIDTH), lambda i: (i % tps, 0)),
            _resident(bd.shape),
            _resident(wcat.shape),
            _resident(bsx.shape),
        ],
        out_specs=[
            pl.BlockSpec((KIN_TM, N_SEC * GROUP_WIDTH), lambda i: (i, 0)),
            pl.BlockSpec((None, 4, KIN_TM // 4, A_WIDTH), lambda i: (i // tps, 0, i % tps, 0)),
            pl.BlockSpec((None, 16, KIN_TM // 16, A_WIDTH), lambda i: (i // tps, 0, i % tps, 0)),
        ],
        out_shape=[
            jax.ShapeDtypeStruct((n, N_SEC * GROUP_WIDTH), BF16),
            jax.ShapeDtypeStruct((batch, 4, seq // 4, A_WIDTH), BF16),
            jax.ShapeDtypeStruct((batch, 16, seq // 16, A_WIDTH), BF16),
        ],
        scratch_shapes=[pltpu.VMEM((A_WIDTH // LANES, KIN_TM, LANES), F32)],
        compiler_params=_params(("parallel",)),
        name="proj",
    )(x2d, g, w, vec, cos_t, sin_t, bd, wcat, bsx)


def _attend(q, kwin, vwin, bias_ref, case, sink_ref=None):
    tq = q.shape[0]
    hid = _head_id((tq, GROUP_WIDTH))
    qzero = jnp.zeros_like(q)
    q4 = jnp.concatenate([jnp.where(hid == h, q, qzero) for h in range(N_HEADS)], axis=0)
    s = lax.dot_general(q4, kwin, (((1,), (1,)), ((), ())), preferred_element_type=F32)
    s = s + bias_ref[case]
    m = jnp.max(s, axis=-1, keepdims=True)
    if sink_ref is not None:
        sink = jnp.concatenate([jnp.full((tq, 1), sink_ref[h], F32) for h in range(N_HEADS)], axis=0)
        m = jnp.maximum(m, sink)
    p = jnp.exp(s - m)
    den = jnp.sum(p, axis=-1, keepdims=True)
    if sink_ref is not None:
        den = den + jnp.exp(sink - m)
    o4 = jnp.dot(p.astype(BF16), vwin, preferred_element_type=F32) * (1.0 / den)
    l4 = jnp.broadcast_to(m + jnp.log(den), o4.shape)
    out = o4[0:tq]
    lse = l4[0:tq]
    for h in range(1, N_HEADS):
        out = jnp.where(hid == h, o4[h * tq:(h + 1) * tq], out)
        lse = jnp.where(hid == h, l4[h * tq:(h + 1) * tq], lse)
    return out, lse


def _band_window(j, tq, radius, length, kw):
    n_tiles = length // tq
    if n_tiles == 1:
        return 0, 0
    ks = pl.multiple_of(jnp.clip(j * tq - radius, 0, length - kw), 64)
    case = jnp.where(j == 0, 0, jnp.where(j == n_tiles - 1, 2, 1))
    return ks, case


def _dilated_kernel(q_ref, k_ref, v_ref, d4_ref, d16_ref, b1_ref, b4_ref, b16_ref, gain_ref, o_ref,
                    acc_ref, lse_ref, *, radii):
    seq = q_ref.shape[0]
    tq = BAND_TQ

    kw1 = min(tq + 2 * radii[0], seq)

    def body1(j, carry):
        q0 = pl.multiple_of(j * tq, tq)
        ks, case = _band_window(j, tq, radii[0], seq, kw1)
        o, l = _attend(q_ref[pl.ds(q0, tq), :], k_ref[pl.ds(ks, kw1), :], v_ref[pl.ds(ks, kw1), :], b1_ref, case)
        for half in range(GROUP_WIDTH // LANES):
            cols = slice(half * LANES, (half + 1) * LANES)
            acc_ref[half, pl.ds(q0, tq), :] = o[:, cols]
            lse_ref[half, pl.ds(q0, tq), :] = l[:, cols]
        return carry

    lax.fori_loop(0, seq // tq, body1, 0, unroll=BAND_UNROLL)

    for dil, ref, bias_ref, radius in ((4, d4_ref, b4_ref, radii[1]), (16, d16_ref, b16_ref, radii[2])):
        length = seq // dil
        tiles = length // tq
        kw = min(tq + 2 * radius, length)

        def body(it, carry, dil=dil, ref=ref, bias_ref=bias_ref, radius=radius, length=length, tiles=tiles, kw=kw):
            r = it // tiles
            j = it % tiles
            q0 = pl.multiple_of(j * tq, tq)
            ks, case = _band_window(j, tq, radius, length, kw)
            o, l = _attend(ref[r, pl.ds(q0, tq), 0:GROUP_WIDTH],
                           ref[r, pl.ds(ks, kw), GROUP_WIDTH:2 * GROUP_WIDTH],
                           ref[r, pl.ds(ks, kw), 2 * GROUP_WIDTH:3 * GROUP_WIDTH], bias_ref, case)
            rows = pl.ds(r + dil * q0, tq, stride=dil)
            for half in range(GROUP_WIDTH // LANES):
                cols = slice(half * LANES, (half + 1) * LANES)
                l_new = l[:, cols]
                l_old = lse_ref[half, rows, :]
                m = jnp.maximum(l_old, l_new)
                wa = jnp.exp(l_old - m)
                wb = jnp.exp(l_new - m)
                tot = wa + wb
                acc_ref[half, rows, :] = (wa * acc_ref[half, rows, :] + wb * o[:, cols]) * (1.0 / tot)
                lse_ref[half, rows, :] = m + jnp.log(tot)
            return carry

        lax.fori_loop(0, dil * tiles, body, 0, unroll=BAND_UNROLL)

    acc = jnp.concatenate([acc_ref[0], acc_ref[1]], axis=-1)
    o_ref[...] = _rms(acc, gain_ref[...]).astype(BF16)


def _dilated_call(proj3d, d4, d16, biases, gain):
    b, seq, _ = proj3d.shape
    radii = tuple(window // (2 * dil) for window, dil in DILATED_CFGS)
    kern = functools.partial(_dilated_kernel, radii=radii)
    return pl.pallas_call(
        kern,
        grid=(b,),
        in_specs=[
            pl.BlockSpec((None, seq, GROUP_WIDTH), lambda bi: (bi, 0, SEC_AQ)),
            pl.BlockSpec((None, seq, GROUP_WIDTH), lambda bi: (bi, 0, SEC_AK)),
            pl.BlockSpec((None, seq, GROUP_WIDTH), lambda bi: (bi, 0, SEC_AV)),
            pl.BlockSpec((None,) + d4.shape[1:], lambda bi: (bi, 0, 0, 0)),
            pl.BlockSpec((None,) + d16.shape[1:], lambda bi: (bi, 0, 0, 0)),
            _resident(biases[0].shape), _resident(biases[1].shape), _resident(biases[2].shape),
            _resident((1, GROUP_WIDTH)),
        ],
        out_specs=pl.BlockSpec((None, seq, GROUP_WIDTH), lambda bi: (bi, 0, 0)),
        out_shape=jax.ShapeDtypeStruct((b, seq, GROUP_WIDTH), BF16),
        scratch_shapes=[pltpu.VMEM((GROUP_WIDTH // LANES, seq, LANES), F32)] * 2,
        compiler_params=_params(("parallel",)),
        name="dilated",
    )(proj3d, proj3d, proj3d, d4, d16, biases[0], biases[1], biases[2], gain)


def _swa_kernel(q_ref, k_ref, v_ref, bias_ref, sink_ref, gain_ref, o_ref):
    seq = q_ref.shape[0]
    tq = BAND_TQ
    kw = min(tq + 2 * SWA_RADIUS, seq)

    def body(j, carry):
        q0 = pl.multiple_of(j * tq, tq)
        ks, case = _band_window(j, tq, SWA_RADIUS, seq, kw)
        o, _ = _attend(q_ref[pl.ds(q0, tq), :], k_ref[pl.ds(ks, kw), :], v_ref[pl.ds(ks, kw), :],
                       bias_ref, case, sink_ref)
        o_ref[pl.ds(q0, tq), :] = _rms(o, gain_ref[...]).astype(BF16)
        return carry

    lax.fori_loop(0, seq // tq, body, 0, unroll=SWA_UNROLL)


def _swa_call(proj3d, bias, sink, gain):
    b, seq, _ = proj3d.shape
    return pl.pallas_call(
        _swa_kernel,
        grid=(b,),
        in_specs=[
            pl.BlockSpec((None, seq, GROUP_WIDTH), lambda bi: (bi, 0, SEC_BQ)),
            pl.BlockSpec((None, seq, GROUP_WIDTH), lambda bi: (bi, 0, SEC_BK)),
            pl.BlockSpec((None, seq, GROUP_WIDTH), lambda bi: (bi, 0, SEC_BV)),
            _resident(bias.shape),
            pl.BlockSpec(memory_space=pltpu.SMEM),
            _resident((1, GROUP_WIDTH)),
        ],
        out_specs=pl.BlockSpec((None, seq, GROUP_WIDTH), lambda bi: (bi, 0, 0)),
        out_shape=jax.ShapeDtypeStruct((b, seq, GROUP_WIDTH), BF16),
        compiler_params=_params(("parallel",)),
        name="swa",
    )(proj3d, proj3d, proj3d, bias, sink, gain)


def _dense_kernel(q_ref, k_ref, v_ref, gain_ref, o_ref):
    q = q_ref[...]
    k = k_ref[...]
    v = v_ref[...]
    tq = q.shape[0]
    hid = _head_id((tq, GROUP_WIDTH))
    qzero = jnp.zeros_like(q)
    acc = jnp.zeros((tq, GROUP_WIDTH), F32)
    for h in range(N_HEADS):
        qh = jnp.where(hid == h, q, qzero)
        s = lax.dot_general(qh, k, (((1,), (1,)), ((), ())), preferred_element_type=F32)
        m = jnp.max(s, axis=-1, keepdims=True)
        p = jnp.exp(s - m)
        den = jnp.sum(p, axis=-1, keepdims=True)
        oh = jnp.dot(p.astype(BF16), v, preferred_element_type=F32) * (1.0 / den)
        acc = jnp.where(hid == h, oh, acc)
    o_ref[...] = _rms(acc, gain_ref[...]).astype(BF16)


def _dense_call(proj3d, gain):
    b, seq, _ = proj3d.shape
    return pl.pallas_call(
        _dense_kernel,
        grid=(b, seq // DENSE_TQ),
        in_specs=[
            pl.BlockSpec((None, DENSE_TQ, GROUP_WIDTH), lambda bi, j: (bi, j, SEC_DQ)),
            pl.BlockSpec((None, seq, GROUP_WIDTH), lambda bi, j: (bi, 0, SEC_DK)),
            pl.BlockSpec((None, seq, GROUP_WIDTH), lambda bi, j: (bi, 0, SEC_DV)),
            _resident((1, GROUP_WIDTH)),
        ],
        out_specs=pl.BlockSpec((None, DENSE_TQ, GROUP_WIDTH), lambda bi, j: (bi, j, 0)),
        out_shape=jax.ShapeDtypeStruct((b, seq, GROUP_WIDTH), BF16),
        compiler_params=_params(("parallel", "arbitrary")),
        name="dense",
    )(proj3d, proj3d, proj3d, gain)


def _mix_kernel(x_ref, ya_ref, yb_ref, yc_ref, yd_ref, w_ref, g_ref, x1_ref, xn_ref):
    mixed = jnp.concatenate([ya_ref[...], yb_ref[...], yc_ref[...], yd_ref[...]], axis=-1)
    x1 = x_ref[...] + jnp.dot(mixed, w_ref[...], preferred_element_type=F32)
    x1_ref[...] = x1
    xn_ref[...] = _rms(x1, g_ref[...]).astype(BF16)


def _mix_call(x2d, ya, yb, proj2d, yd, w_out, g_ffn):
    n = x2d.shape[0]
    row = lambda i: (i, 0)
    grp = pl.BlockSpec((MIX_TM, GROUP_WIDTH), row)
    return pl.pallas_call(
        _mix_kernel,
        grid=(n // MIX_TM,),
        in_specs=[pl.BlockSpec((MIX_TM, D_MODEL), row), grp, grp,
                  pl.BlockSpec((MIX_TM, GROUP_WIDTH), lambda i: (i, SEC_YC)), grp,
                  _resident(w_out.shape), _resident((1, D_MODEL))],
        out_specs=[pl.BlockSpec((MIX_TM, D_MODEL), row), pl.BlockSpec((MIX_TM, D_MODEL), row)],
        out_shape=[jax.ShapeDtypeStruct((n, D_MODEL), F32), jax.ShapeDtypeStruct((n, D_MODEL), BF16)],
        compiler_params=_params(("parallel",)),
        name="mix_out",
    )(x2d, ya, yb, proj2d, yd, w_out, g_ffn)


def _ffn_kernel(xn_ref, prev_ref, next_ref, x1_ref, p_ref, wup_ref, cw_ref, wd_ref, gple_ref, wgate_ref,
                wproj_ref, o_ref, xcat_ref, h0_ref, h1_ref, *, tiles_per_seq):
    pos = pl.program_id(0) % tiles_per_seq
    tm = xn_ref.shape[0]
    rows = tm + HALO

    prev_blk = jnp.where(pos == 0, jnp.zeros_like(prev_ref[...]), prev_ref[...])
    next_blk = jnp.where(pos == tiles_per_seq - 1, jnp.zeros_like(next_ref[...]), next_ref[...])
    halo_row = lax.broadcasted_iota(jnp.int32, (HALO, D_MODEL), 0)
    xcat_ref[0:tm, :] = xn_ref[...]
    xcat_ref[tm:rows, :] = jnp.where(halo_row == 0, next_blk, prev_blk)
    o_ref[...] = x1_ref[...]

    def up(k, h_ref):
        xcat = xcat_ref[...]
        h_ref[0] = jnp.dot(xcat, wup_ref[k], preferred_element_type=F32)
        h_ref[1] = jnp.dot(xcat, wup_ref[k + N_CHUNKS], preferred_element_type=F32)

    def conv(h, cw):
        before = pltpu.roll(h, 1, axis=0)
        after = pltpu.roll(h, rows - 1, axis=0)
        y = before * cw[0:1, :] + h * cw[1:2, :] + after * cw[2:3, :] + cw[3:4, :]
        return y[0:tm]

    def down(k, h_ref):
        gt = conv(h_ref[0], cw_ref[k])
        uu = conv(h_ref[1], cw_ref[k + N_CHUNKS])
        act = (gt / (1.0 + jnp.exp(-gt)) * uu).astype(BF16)
        o_ref[...] += jnp.dot(act, wd_ref[k], preferred_element_type=F32)

    up(0, h0_ref)

    def pair(t, carry):
        k = 2 * t
        up(k + 1, h1_ref)
        down(k, h0_ref)
        up(k + 2, h0_ref)
        down(k + 1, h1_ref)
        return carry

    lax.fori_loop(0, (N_CHUNKS - 1) // 2, pair, 0)
    down(N_CHUNKS - 1, h0_ref)

    x2 = o_ref[...]
    z = jnp.dot(_rms(x2, gple_ref[...]).astype(BF16), wgate_ref[...], preferred_element_type=F32)
    gate = 1.0 / (1.0 + jnp.exp(-z))
    inj = jnp.dot(p_ref[...].astype(BF16), wproj_ref[...], preferred_element_type=F32)
    o_ref[...] = x2 + inj * gate


def _ffn_call(xn, x1, p2d, wup, cw, wd, g_ple, w_gate, w_proj, seq):
    assert N_CHUNKS % 2 == 1
    n = xn.shape[0]
    halo_per_tile = FFN_TM // HALO
    n_halo = n // HALO
    row = lambda i: (i, 0)
    kern = functools.partial(_ffn_kernel, tiles_per_seq=seq // FFN_TM)
    return pl.pallas_call(
        kern,
        grid=(n // FFN_TM,),
        in_specs=[
            pl.BlockSpec((FFN_TM, D_MODEL), row),
            pl.BlockSpec((HALO, D_MODEL), lambda i: (jnp.maximum(i * halo_per_tile - 1, 0), 0)),
            pl.BlockSpec((HALO, D_MODEL), lambda i: (jnp.minimum((i + 1) * halo_per_tile, n_halo - 1), 0)),
            pl.BlockSpec((FFN_TM, D_MODEL), row),
            pl.BlockSpec((FFN_TM, PLE_DIM), row),
            _resident(wup.shape), _resident(cw.shape), _resident(wd.shape),
            _resident((1, D_MODEL)), _resident(w_gate.shape), _resident(w_proj.shape),
        ],
        out_specs=pl.BlockSpec((FFN_TM, D_MODEL), row),
        out_shape=jax.ShapeDtypeStruct((n, D_MODEL), F32),
        scratch_shapes=[pltpu.VMEM((FFN_TM + HALO, D_MODEL), BF16),
                        pltpu.VMEM((2, FFN_TM + HALO, FFN_TF), F32),
                        pltpu.VMEM((2, FFN_TM + HALO, FFN_TF), F32)],
        compiler_params=_params(("parallel",)),
        name="ffn_ple",
    )(xn, xn, xn, x1, p2d, wup, cw, wd, g_ple, w_gate, w_proj)


def _t5_bucket(rel):
    nb = REL_BUCKETS // 2
    ret = jnp.where(rel > 0, nb, 0)
    n = jnp.abs(rel)
    max_exact = nb // 2
    nf = jnp.maximum(n, 1).astype(F32)
    large = max_exact + (jnp.log(nf / max_exact) / math.log(REL_MAX_DIST / max_exact)
                         * (nb - max_exact)).astype(jnp.int32)
    large = jnp.minimum(large, nb - 1)
    return ret + jnp.where(n < max_exact, n, large)


def _band_bias(table, radius, dil, tq, kw):
    cases = []
    for off in (0, radius, kw - tq):
        rel = jnp.arange(kw)[None, :] - jnp.arange(tq)[:, None] - off
        onehot = (_t5_bucket(rel * dil)[..., None] == jnp.arange(REL_BUCKETS)).astype(F32)
        bias = jnp.einsum('qkn,nh->hqk', onehot, table.astype(F32), precision=lax.Precision.HIGHEST)
        bias = jnp.where((jnp.abs(rel) <= radius)[None], bias, NEG_INF)
        cases.append(bias.reshape(N_HEADS * tq, kw))
    return jnp.stack(cases, axis=0)


def _rope_tables(seq):
    half = HEAD_DIM // 2
    inv = ROPE_THETA ** (-jnp.arange(0, half, 2, dtype=F32) / half)
    t = jnp.arange(seq)
    ang_r = (t // GRID_W).astype(F32)[:, None] * inv[None, :]
    ang_c = (t % GRID_W).astype(F32)[:, None] * inv[None, :]
    cos_h = jnp.concatenate([jnp.cos(ang_r)] * 2 + [jnp.cos(ang_c)] * 2, axis=-1)
    sin_h = jnp.concatenate([-jnp.sin(ang_r), jnp.sin(ang_r), -jnp.sin(ang_c), jnp.sin(ang_c)], axis=-1)
    return jnp.tile(cos_h, (1, N_HEADS)), jnp.tile(sin_h, (1, N_HEADS))


def _rep_kv(w):
    return jnp.concatenate([w[:, :HEAD_DIM], w[:, :HEAD_DIM], w[:, HEAD_DIM:], w[:, HEAD_DIM:]], axis=1)


def _w_in_sections(w):
    return jnp.concatenate([w[:, 0:1024], _rep_kv(w[:, 1024:1152]), _rep_kv(w[:, 1152:1280]), w[:, 1280:2048],
                            _rep_kv(w[:, 2048:2176]), _rep_kv(w[:, 2176:2304])], axis=1)


def kernel(x, p, rel_bias, ln_mix_g, w_in, qk_gain, sink, c_norm_g, c_norm_b, c_ws, c_bs, out_gain, w_out,
           ln_ffn_g, w_up, conv_w, conv_b, w_down, ln_ple_g, w_ple_gate, w_ple_proj):
    b, s, _ = x.shape
    n = b * s
    depth = w_in.shape[0]
    cos_t, sin_t = _rope_tables(s)
    bd = (np.arange(GROUP_WIDTH)[:, None] // HEAD_DIM == np.arange(GROUP_WIDTH)[None, :] // HEAD_DIM)
    bd = jnp.asarray(bd, BF16)
    table_a = rel_bias[:, :N_HEADS]
    table_b = rel_bias[:, N_HEADS:]
    bias_a = []
    for window, dil in DILATED_CFGS:
        radius = window // (2 * dil)
        length = s // dil
        tq = min(BAND_TQ, length)
        bias_a.append(_band_bias(table_a, radius, dil, tq, min(tq + 2 * radius, length)))
    tq_b = min(BAND_TQ, s)
    bias_b = _band_bias(table_b, SWA_RADIUS, 1, tq_b, min(tq_b + 2 * SWA_RADIUS, s))

    x2d = x.reshape(n, D_MODEL)
    for i in range(depth):
        w = _w_in_sections(w_in[i]).astype(BF16)
        qg = qk_gain[i]
        vec = jnp.stack([
            jnp.tile(qg[0, 0], N_HEADS) * ATTN_SCALE, jnp.tile(qg[0, 1], N_HEADS),
            jnp.tile(qg[1, 0], N_HEADS) * ATTN_SCALE, jnp.tile(qg[1, 1], N_HEADS),
            jnp.tile(qg[2, 0], N_HEADS) * ATTN_SCALE, jnp.tile(qg[2, 1], N_HEADS),
            c_norm_g[i], c_norm_b[i], out_gain[i, 2],
        ] + [jnp.zeros((GROUP_WIDTH,), F32)] * 7, axis=0)
        wcat = jnp.concatenate([c_ws[i, g] for g in range(C_GROUPS)], axis=1).astype(BF16)
        bsx = jnp.repeat(jnp.transpose(c_bs[i]), GROUP_WIDTH // C_GROUPS, axis=1)
        proj2d, d4, d16 = _proj_call(x2d, ln_mix_g[i][None], w, vec, cos_t, sin_t, bd, wcat, bsx, b, s)

        proj3d = proj2d.reshape(b, s, N_SEC * GROUP_WIDTH)
        ya = _dilated_call(proj3d, d4, d16, bias_a, out_gain[i, 0][None])
        yb = _swa_call(proj3d, bias_b, sink[i], out_gain[i, 1][None])
        yd = _dense_call(proj3d, out_gain[i, 3][None])

        x1, xn = _mix_call(x2d, ya.reshape(n, GROUP_WIDTH), yb.reshape(n, GROUP_WIDTH), proj2d,
                           yd.reshape(n, GROUP_WIDTH), w_out[i].astype(BF16), ln_ffn_g[i][None])
        wup = jnp.transpose(w_up[i].astype(BF16).reshape(D_MODEL, 2 * N_CHUNKS, FFN_TF), (1, 0, 2))
        cw = jnp.concatenate([conv_w[i], conv_b[i][None]], axis=0)
        cw = jnp.transpose(cw.reshape(4, 2 * N_CHUNKS, FFN_TF), (1, 0, 2))
        wd = w_down[i].astype(BF16).reshape(N_CHUNKS, FFN_TF, D_MODEL)
        x2d = _ffn_call(xn, x1, p[i].reshape(n, PLE_DIM), wup, cw, wd, ln_ple_g[i][None],
                        w_ple_gate[i].astype(BF16), w_ple_proj[i].astype(BF16), s)
    return x2d.reshape(b, s, D_MODEL)
```

```python
import functools
import math

import jax
import jax.numpy as jnp
import numpy as np
from jax import lax
from jax.experimental import pallas as pl
from jax.experimental.pallas import tpu as pltpu

D_MODEL = 1024
HEAD_DIM = 64
GROUP_WIDTH = 256
N_HEADS = GROUP_WIDTH // HEAD_DIM
C_GROUPS = 4
C_CHUNK = 128
DILATED_CFGS = ((128, 1), (512, 4), (2048, 16))
SWA_RADIUS = 128
GRID_W = 64
ROPE_THETA = 10000.0
REL_BUCKETS = 32
REL_MAX_DIST = 1024
D_FF = 2816
PLE_DIM = 256
EPS = 1e-6
NEG_INF = -1e30
ATTN_SCALE = HEAD_DIM ** -0.5

BF16 = jnp.bfloat16
F32 = jnp.float32

SEC_AQ, SEC_AK, SEC_AV, SEC_BQ, SEC_BK, SEC_BV, SEC_YC, SEC_DQ, SEC_DK, SEC_DV = range(10)
N_SEC = 10
A_WIDTH = 3 * GROUP_WIDTH

VMEM_LIMIT = 56 * 1024 * 1024

KIN_TM = 512
BAND_TQ = 128
BAND_UNROLL = 4
SWA_UNROLL = 2
DENSE_TQ = 512
MIX_TM = 512
FFN_TM = 512
FFN_TF = 256
LANES = 128
SUB = 8
HALO = 16
N_CHUNKS = D_FF // FFN_TF


def _params(sem):
    return pltpu.CompilerParams(dimension_semantics=sem, vmem_limit_bytes=VMEM_LIMIT)


def _resident(shape):
    zeros = (0,) * len(shape)
    return pl.BlockSpec(shape, lambda *_: zeros, pipeline_mode=pl.Buffered(1))


def _rms(t, gain):
    return t * lax.rsqrt(jnp.mean(t * t, axis=-1, keepdims=True) + EPS) * gain


def _head_id(shape):
    return lax.broadcasted_iota(jnp.int32, shape, 1) >> 6


def _proj_kernel(x_ref, g_ref, w_ref, vec_ref, cos_ref, sin_ref, bd_ref, wcat_ref, bsx_ref,
                 o_ref, d4_ref, d16_ref, a_ref):
    x = x_ref[...]
    tm = x.shape[0]
    hn = _rms(x, g_ref[...]).astype(BF16)
    bd = bd_ref[...]

    def proj(sec):
        return jnp.dot(hn, w_ref[:, sec * GROUP_WIDTH:(sec + 1) * GROUP_WIDTH],
                       preferred_element_type=F32)

    def store(sec, val):
        o_ref[:, sec * GROUP_WIDTH:(sec + 1) * GROUP_WIDTH] = val.astype(BF16)

    def store_a(sec, val):
        store(sec, val)
        for half in range(GROUP_WIDTH // LANES):
            slot = sec * (GROUP_WIDTH // LANES) + half
            a_ref[slot] = val[:, half * LANES:(half + 1) * LANES]
            cols = slice(slot * LANES, (slot + 1) * LANES)
            for dil, ref in ((4, d4_ref), (16, d16_ref)):
                for r in range(dil):
                    ref[r, :, cols] = a_ref[slot, pl.ds(r, tm // dil, stride=dil), :].astype(BF16)

    def head_rms(t, row):
        sq = t * t
        hi = sq.astype(BF16)
        lo = (sq - hi.astype(F32)).astype(BF16)
        ss = (jnp.dot(hi, bd, preferred_element_type=F32) + jnp.dot(lo, bd, preferred_element_type=F32))
        return t * lax.rsqrt(ss * (1.0 / HEAD_DIM) + EPS) * vec_ref[row:row + 1, :]

    def rope(t):
        halves = []
        for c0 in range(0, GROUP_WIDTH, 128):
            th = t[:, c0:c0 + 128]
            lane = lax.broadcasted_iota(jnp.int32, th.shape, 1)
            partner = jnp.where((lane & 31) < 16, pltpu.roll(th, 128 - 16, axis=1), pltpu.roll(th, 16, axis=1))
            halves.append(th * cos_ref[:, c0:c0 + 128] + partner * sin_ref[:, c0:c0 + 128])
        return jnp.concatenate(halves, axis=-1)

    store_a(SEC_AQ, head_rms(proj(0), 0))
    store_a(SEC_AK, head_rms(proj(1), 1))
    store_a(SEC_AV, proj(2))
    store(SEC_BQ, head_rms(proj(3), 2))
    store(SEC_BK, head_rms(proj(4), 3))
    store(SEC_BV, proj(5))
    store(SEC_DQ, rope(head_rms(proj(8), 4)))
    store(SEC_DK, rope(head_rms(proj(9), 5)))
    store(SEC_DV, proj(10))

    u = jax.nn.gelu(proj(6))
    v = jax.nn.gelu(proj(7))
    mu = jnp.mean(v, axis=-1, keepdims=True)
    vc = v - mu
    var = jnp.mean(vc * vc, axis=-1, keepdims=True)
    vn = (vc * lax.rsqrt(var + EPS) * vec_ref[6:7, :] + vec_ref[7:8, :]).astype(BF16)
    gid = _head_id((C_CHUNK, GROUP_WIDTH))
    zero = jnp.zeros((C_CHUNK, GROUP_WIDTH), BF16)
    for c in range(tm // C_CHUNK):
        rows = slice(c * C_CHUNK, (c + 1) * C_CHUNK)
        vchunk = vn[rows]
        stacked = jnp.concatenate([jnp.where(gid == g, vchunk, zero) for g in range(C_GROUPS)], axis=0)
        mixed = jnp.dot(wcat_ref[...], stacked, preferred_element_type=F32) + bsx_ref[...]
        yc = _rms(u[rows] * mixed, vec_ref[8:9, :])
        o_ref[rows, SEC_YC * GROUP_WIDTH:(SEC_YC + 1) * GROUP_WIDTH] = yc.astype(BF16)


def _proj_call(x2d, g, w, vec, cos_t, sin_t, bd, wcat, bsx, batch, seq):
    n = x2d.shape[0]
    tps = seq // KIN_TM
    return pl.pallas_call(
        _proj_kernel,
        grid=(n // KIN_TM,),
        in_specs=[
            pl.BlockSpec((KIN_TM, D_MODEL), lambda i: (i, 0)),
            _resident((1, D_MODEL)),
            _resident(w.shape),
            _resident(vec.shape),
            pl.BlockSpec((KIN_TM, GROUP_WIDTH), lambda i: (i % tps, 0)),
            pl.BlockSpec((KIN_TM, GROUP_WIDTH), lambda i: (i % tps, 0)),
            _resident(bd.shape),
            _resident(wcat.shape),
            _resident(bsx.shape),
        ],
        out_specs=[
            pl.BlockSpec((KIN_TM, N_SEC * GROUP_WIDTH), lambda i: (i, 0)),
            pl.BlockSpec((None, 4, KIN_TM // 4, A_WIDTH), lambda i: (i // tps, 0, i % tps, 0)),
            pl.BlockSpec((None, 16, KIN_TM // 16, A_WIDTH), lambda i: (i // tps, 0, i % tps, 0)),
        ],
        out_shape=[
            jax.ShapeDtypeStruct((n, N_SEC * GROUP_WIDTH), BF16),
            jax.ShapeDtypeStruct((batch, 4, seq // 4, A_WIDTH), BF16),
            jax.ShapeDtypeStruct((batch, 16, seq // 16, A_WIDTH), BF16),
        ],
        scratch_shapes=[pltpu.VMEM((A_WIDTH // LANES, KIN_TM, LANES), F32)],
        compiler_params=_params(("parallel",)),
        name="proj",
    )(x2d, g, w, vec, cos_t, sin_t, bd, wcat, bsx)


def _attend(q, kwin, vwin, bias_ref, case, sink_ref=None):
    tq = q.shape[0]
    hid = _head_id((tq, GROUP_WIDTH))
    qzero = jnp.zeros_like(q)
    q4 = jnp.concatenate([jnp.where(hid == h, q, qzero) for h in range(N_HEADS)], axis=0)
    s = lax.dot_general(q4, kwin, (((1,), (1,)), ((), ())), preferred_element_type=F32)
    s = s + bias_ref[case]
    if sink_ref is None:
        m = jnp.max(s, axis=-1, keepdims=True)
        p = jnp.exp(s - m)
        den = jnp.sum(p, axis=-1, keepdims=True)
    else:
        sink = sink_ref[...]
        blocks = [s[:, c:c + LANES] for c in range(0, s.shape[1], LANES)]
        m = jnp.max(functools.reduce(jnp.maximum, blocks + [sink]), axis=-1, keepdims=True)
        p = jnp.exp(s - m)
        pblocks = [p[:, c:c + LANES] for c in range(0, s.shape[1], LANES)]
        den = jnp.sum(functools.reduce(jnp.add, pblocks + [jnp.exp(sink - m)]), axis=-1, keepdims=True)
    o4 = jnp.dot(p.astype(BF16), vwin, preferred_element_type=F32) * (1.0 / den)
    l4 = jnp.broadcast_to(m + jnp.log(den), o4.shape)
    out = o4[0:tq]
    lse = l4[0:tq]
    for h in range(1, N_HEADS):
        out = jnp.where(hid == h, o4[h * tq:(h + 1) * tq], out)
        lse = jnp.where(hid == h, l4[h * tq:(h + 1) * tq], lse)
    return out, lse


def _band_window(j, tq, radius, length, kw):
    n_tiles = length // tq
    if n_tiles == 1:
        return 0, 0
    ks = pl.multiple_of(jnp.clip(j * tq - radius, 0, length - kw), 64)
    case = jnp.where(j == 0, 0, jnp.where(j == n_tiles - 1, 2, 1))
    return ks, case


def _dilated_kernel(q_ref, k_ref, v_ref, d4_ref, d16_ref, b1_ref, b4_ref, b16_ref, gain_ref, o_ref,
                    acc_ref, lse_ref, *, radii):
    seq = q_ref.shape[0]
    tq = BAND_TQ

    kw1 = min(tq + 2 * radii[0], seq)

    def body1(j, carry):
        q0 = pl.multiple_of(j * tq, tq)
        ks, case = _band_window(j, tq, radii[0], seq, kw1)
        o, l = _attend(q_ref[pl.ds(q0, tq), :], k_ref[pl.ds(ks, kw1), :], v_ref[pl.ds(ks, kw1), :], b1_ref, case)
        for half in range(GROUP_WIDTH // LANES):
            cols = slice(half * LANES, (half + 1) * LANES)
            acc_ref[half, pl.ds(q0, tq), :] = o[:, cols]
            lse_ref[half, pl.ds(q0, tq), :] = l[:, cols]
        return carry

    lax.fori_loop(0, seq // tq, body1, 0, unroll=BAND_UNROLL)

    for dil, ref, bias_ref, radius in ((4, d4_ref, b4_ref, radii[1]), (16, d16_ref, b16_ref, radii[2])):
        length = seq // dil
        tiles = length // tq
        kw = min(tq + 2 * radius, length)

        def body(it, carry, dil=dil, ref=ref, bias_ref=bias_ref, radius=radius, length=length, tiles=tiles, kw=kw):
            r = it // tiles
            j = it % tiles
            q0 = pl.multiple_of(j * tq, tq)
            ks, case = _band_window(j, tq, radius, length, kw)
            o, l = _attend(ref[r, pl.ds(q0, tq), 0:GROUP_WIDTH],
                           ref[r, pl.ds(ks, kw), GROUP_WIDTH:2 * GROUP_WIDTH],
                           ref[r, pl.ds(ks, kw), 2 * GROUP_WIDTH:3 * GROUP_WIDTH], bias_ref, case)
            rows = pl.ds(r + dil * q0, tq, stride=dil)
            for half in range(GROUP_WIDTH // LANES):
                cols = slice(half * LANES, (half + 1) * LANES)
                l_new = l[:, cols]
                l_old = lse_ref[half, rows, :]
                m = jnp.maximum(l_old, l_new)
                wa = jnp.exp(l_old - m)
                wb = jnp.exp(l_new - m)
                tot = wa + wb
                acc_ref[half, rows, :] = (wa * acc_ref[half, rows, :] + wb * o[:, cols]) * (1.0 / tot)
                lse_ref[half, rows, :] = m + jnp.log(tot)
            return carry

        lax.fori_loop(0, dil * tiles, body, 0, unroll=BAND_UNROLL)

    acc = jnp.concatenate([acc_ref[0], acc_ref[1]], axis=-1)
    o_ref[...] = _rms(acc, gain_ref[...]).astype(BF16)


def _dilated_call(proj3d, d4, d16, biases, gain):
    b, seq, _ = proj3d.shape
    radii = tuple(window // (2 * dil) for window, dil in DILATED_CFGS)
    kern = functools.partial(_dilated_kernel, radii=radii)
    return pl.pallas_call(
        kern,
        grid=(b,),
        in_specs=[
            pl.BlockSpec((None, seq, GROUP_WIDTH), lambda bi: (bi, 0, SEC_AQ)),
            pl.BlockSpec((None, seq, GROUP_WIDTH), lambda bi: (bi, 0, SEC_AK)),
            pl.BlockSpec((None, seq, GROUP_WIDTH), lambda bi: (bi, 0, SEC_AV)),
            pl.BlockSpec((None,) + d4.shape[1:], lambda bi: (bi, 0, 0, 0)),
            pl.BlockSpec((None,) + d16.shape[1:], lambda bi: (bi, 0, 0, 0)),
            _resident(biases[0].shape), _resident(biases[1].shape), _resident(biases[2].shape),
            _resident((1, GROUP_WIDTH)),
        ],
        out_specs=pl.BlockSpec((None, seq, GROUP_WIDTH), lambda bi: (bi, 0, 0)),
        out_shape=jax.ShapeDtypeStruct((b, seq, GROUP_WIDTH), BF16),
        scratch_shapes=[pltpu.VMEM((GROUP_WIDTH // LANES, seq, LANES), F32)] * 2,
        compiler_params=_params(("parallel",)),
        name="dilated",
    )(proj3d, proj3d, proj3d, d4, d16, biases[0], biases[1], biases[2], gain)


def _swa_kernel(q_ref, k_ref, v_ref, bias_ref, sink_ref, gain_ref, o_ref):
    seq = q_ref.shape[0]
    tq = BAND_TQ
    kw = min(tq + 2 * SWA_RADIUS, seq)

    def body(j, carry):
        q0 = pl.multiple_of(j * tq, tq)
        ks, case = _band_window(j, tq, SWA_RADIUS, seq, kw)
        o, _ = _attend(q_ref[pl.ds(q0, tq), :], k_ref[pl.ds(ks, kw), :], v_ref[pl.ds(ks, kw), :],
                       bias_ref, case, sink_ref)
        o_ref[pl.ds(q0, tq), :] = _rms(o, gain_ref[...]).astype(BF16)
        return carry

    lax.fori_loop(0, seq // tq, body, 0, unroll=SWA_UNROLL)


def _swa_call(proj3d, bias, sink, gain):
    b, seq, _ = proj3d.shape
    return pl.pallas_call(
        _swa_kernel,
        grid=(b,),
        in_specs=[
            pl.BlockSpec((None, seq, GROUP_WIDTH), lambda bi: (bi, 0, SEC_BQ)),
            pl.BlockSpec((None, seq, GROUP_WIDTH), lambda bi: (bi, 0, SEC_BK)),
            pl.BlockSpec((None, seq, GROUP_WIDTH), lambda bi: (bi, 0, SEC_BV)),
            _resident(bias.shape),
            _resident(sink.shape),
            _resident((1, GROUP_WIDTH)),
        ],
        out_specs=pl.BlockSpec((None, seq, GROUP_WIDTH), lambda bi: (bi, 0, 0)),
        out_shape=jax.ShapeDtypeStruct((b, seq, GROUP_WIDTH), BF16),
        compiler_params=_params(("parallel",)),
        name="swa",
    )(proj3d, proj3d, proj3d, bias, sink, gain)


def _dense_kernel(q_ref, k_ref, v_ref, gain_ref, o_ref):
    q = q_ref[...]
    k = k_ref[...]
    v = v_ref[...]
    tq = q.shape[0]
    hid = _head_id((tq, GROUP_WIDTH))
    qzero = jnp.zeros_like(q)
    acc = jnp.zeros((tq, GROUP_WIDTH), F32)
    for h in range(N_HEADS):
        qh = jnp.where(hid == h, q, qzero)
        s = lax.dot_general(qh, k, (((1,), (1,)), ((), ())), preferred_element_type=F32)
        m = jnp.max(s, axis=-1, keepdims=True)
        p = jnp.exp(s - m)
        den = jnp.sum(p, axis=-1, keepdims=True)
        oh = jnp.dot(p.astype(BF16), v, preferred_element_type=F32) * (1.0 / den)
        acc = jnp.where(hid == h, oh, acc)
    o_ref[...] = _rms(acc, gain_ref[...]).astype(BF16)


def _dense_call(proj3d, gain):
    b, seq, _ = proj3d.shape
    return pl.pallas_call(
        _dense_kernel,
        grid=(b, seq // DENSE_TQ),
        in_specs=[
            pl.BlockSpec((None, DENSE_TQ, GROUP_WIDTH), lambda bi, j: (bi, j, SEC_DQ)),
            pl.BlockSpec((None, seq, GROUP_WIDTH), lambda bi, j: (bi, 0, SEC_DK)),
            pl.BlockSpec((None, seq, GROUP_WIDTH), lambda bi, j: (bi, 0, SEC_DV)),
            _resident((1, GROUP_WIDTH)),
        ],
        out_specs=pl.BlockSpec((None, DENSE_TQ, GROUP_WIDTH), lambda bi, j: (bi, j, 0)),
        out_shape=jax.ShapeDtypeStruct((b, seq, GROUP_WIDTH), BF16),
        compiler_params=_params(("parallel", "arbitrary")),
        name="dense",
    )(proj3d, proj3d, proj3d, gain)


def _mix_kernel(x_ref, ya_ref, yb_ref, yc_ref, yd_ref, w_ref, g_ref, x1_ref, xn_ref):
    mixed = jnp.concatenate([ya_ref[...], yb_ref[...], yc_ref[...], yd_ref[...]], axis=-1)
    x1 = x_ref[...] + jnp.dot(mixed, w_ref[...], preferred_element_type=F32)
    x1_ref[...] = x1
    xn_ref[...] = _rms(x1, g_ref[...]).astype(BF16)


def _mix_call(x2d, ya, yb, proj2d, yd, w_out, g_ffn):
    n = x2d.shape[0]
    row = lambda i: (i, 0)
    grp = pl.BlockSpec((MIX_TM, GROUP_WIDTH), row)
    return pl.pallas_call(
        _mix_kernel,
        grid=(n // MIX_TM,),
        in_specs=[pl.BlockSpec((MIX_TM, D_MODEL), row), grp, grp,
                  pl.BlockSpec((MIX_TM, GROUP_WIDTH), lambda i: (i, SEC_YC)), grp,
                  _resident(w_out.shape), _resident((1, D_MODEL))],
        out_specs=[pl.BlockSpec((MIX_TM, D_MODEL), row), pl.BlockSpec((MIX_TM, D_MODEL), row)],
        out_shape=[jax.ShapeDtypeStruct((n, D_MODEL), F32), jax.ShapeDtypeStruct((n, D_MODEL), BF16)],
        compiler_params=_params(("parallel",)),
        name="mix_out",
    )(x2d, ya, yb, proj2d, yd, w_out, g_ffn)


def _ffn_kernel(xn_ref, prev_ref, next_ref, x1_ref, p_ref, wup_ref, cw_ref, wd_ref, gple_ref, wgate_ref,
                wproj_ref, o_ref, xcat_ref, h0_ref, h1_ref, *, tiles_per_seq):
    pos = pl.program_id(0) % tiles_per_seq
    tm = xn_ref.shape[0]
    rows = tm + HALO

    prev_blk = jnp.where(pos == 0, jnp.zeros_like(prev_ref[...]), prev_ref[...])
    next_blk = jnp.where(pos == tiles_per_seq - 1, jnp.zeros_like(next_ref[...]), next_ref[...])
    halo_row = lax.broadcasted_iota(jnp.int32, (HALO, D_MODEL), 0)
    xcat_ref[0:tm, :] = xn_ref[...]
    xcat_ref[tm:rows, :] = jnp.where(halo_row == 0, next_blk, prev_blk)
    o_ref[...] = x1_ref[...]

    def up(k, h_ref):
        xcat = xcat_ref[...]
        for part in range(2):
            h = jnp.dot(xcat, wup_ref[k + part * N_CHUNKS], preferred_element_type=F32)
            h_ref[part, SUB:SUB + tm, :] = h[0:tm]
            h_ref[part, SUB + tm:2 * SUB + tm, :] = h[tm:tm + SUB]
            h_ref[part, 0:SUB, :] = h[rows - SUB:rows]

    def conv(h_ref, part, cw):
        before = h_ref[part, SUB - 1:SUB - 1 + tm, :]
        centre = h_ref[part, SUB:SUB + tm, :]
        after = h_ref[part, SUB + 1:SUB + 1 + tm, :]
        return before * cw[0:1, :] + centre * cw[1:2, :] + after * cw[2:3, :] + cw[3:4, :]

    def down(k, h_ref):
        gt = conv(h_ref, 0, cw_ref[k])
        uu = conv(h_ref, 1, cw_ref[k + N_CHUNKS])
        act = (gt / (1.0 + jnp.exp(-gt)) * uu).astype(BF16)
        o_ref[...] += jnp.dot(act, wd_ref[k], preferred_element_type=F32)

    up(0, h0_ref)

    def pair(t, carry):
        k = 2 * t
        up(k + 1, h1_ref)
        down(k, h0_ref)
        up(k + 2, h0_ref)
        down(k + 1, h1_ref)
        return carry

    lax.fori_loop(0, (N_CHUNKS - 1) // 2, pair, 0)
    down(N_CHUNKS - 1, h0_ref)

    x2 = o_ref[...]
    z = jnp.dot(_rms(x2, gple_ref[...]).astype(BF16), wgate_ref[...], preferred_element_type=F32)
    gate = 1.0 / (1.0 + jnp.exp(-z))
    inj = jnp.dot(p_ref[...].astype(BF16), wproj_ref[...], preferred_element_type=F32)
    o_ref[...] = x2 + inj * gate


def _ffn_call(xn, x1, p2d, wup, cw, wd, g_ple, w_gate, w_proj, seq):
    assert N_CHUNKS % 2 == 1
    n = xn.shape[0]
    halo_per_tile = FFN_TM // HALO
    n_halo = n // HALO
    row = lambda i: (i, 0)
    kern = functools.partial(_ffn_kernel, tiles_per_seq=seq // FFN_TM)
    return pl.pallas_call(
        kern,
        grid=(n // FFN_TM,),
        in_specs=[
            pl.BlockSpec((FFN_TM, D_MODEL), row),
            pl.BlockSpec((HALO, D_MODEL), lambda i: (jnp.maximum(i * halo_per_tile - 1, 0), 0)),
            pl.BlockSpec((HALO, D_MODEL), lambda i: (jnp.minimum((i + 1) * halo_per_tile, n_halo - 1), 0)),
            pl.BlockSpec((FFN_TM, D_MODEL), row),
            pl.BlockSpec((FFN_TM, PLE_DIM), row),
            _resident(wup.shape), _resident(cw.shape), _resident(wd.shape),
            _resident((1, D_MODEL)), _resident(w_gate.shape), _resident(w_proj.shape),
        ],
        out_specs=pl.BlockSpec((FFN_TM, D_MODEL), row),
        out_shape=jax.ShapeDtypeStruct((n, D_MODEL), F32),
        scratch_shapes=[pltpu.VMEM((FFN_TM + HALO, D_MODEL), BF16),
                        pltpu.VMEM((2, FFN_TM + HALO, FFN_TF), F32),
                        pltpu.VMEM((2, FFN_TM + HALO, FFN_TF), F32)],
        compiler_params=_params(("parallel",)),
        name="ffn_ple",
    )(xn, xn, xn, x1, p2d, wup, cw, wd, g_ple, w_gate, w_proj)


def _t5_bucket(rel):
    nb = REL_BUCKETS // 2
    ret = jnp.where(rel > 0, nb, 0)
    n = jnp.abs(rel)
    max_exact = nb // 2
    nf = jnp.maximum(n, 1).astype(F32)
    large = max_exact + (jnp.log(nf / max_exact) / math.log(REL_MAX_DIST / max_exact)
                         * (nb - max_exact)).astype(jnp.int32)
    large = jnp.minimum(large, nb - 1)
    return ret + jnp.where(n < max_exact, n, large)


def _band_bias(table, radius, dil, tq, kw):
    cases = []
    for off in (0, radius, kw - tq):
        rel = jnp.arange(kw)[None, :] - jnp.arange(tq)[:, None] - off
        onehot = (_t5_bucket(rel * dil)[..., None] == jnp.arange(REL_BUCKETS)).astype(F32)
        bias = jnp.einsum('qkn,nh->hqk', onehot, table.astype(F32), precision=lax.Precision.HIGHEST)
        bias = jnp.where((jnp.abs(rel) <= radius)[None], bias, NEG_INF)
        cases.append(bias.reshape(N_HEADS * tq, kw))
    return jnp.stack(cases, axis=0)


def _rope_tables(seq):
    half = HEAD_DIM // 2
    inv = ROPE_THETA ** (-jnp.arange(0, half, 2, dtype=F32) / half)
    t = jnp.arange(seq)
    ang_r = (t // GRID_W).astype(F32)[:, None] * inv[None, :]
    ang_c = (t % GRID_W).astype(F32)[:, None] * inv[None, :]
    cos_h = jnp.concatenate([jnp.cos(ang_r)] * 2 + [jnp.cos(ang_c)] * 2, axis=-1)
    sin_h = jnp.concatenate([-jnp.sin(ang_r), jnp.sin(ang_r), -jnp.sin(ang_c), jnp.sin(ang_c)], axis=-1)
    return jnp.tile(cos_h, (1, N_HEADS)), jnp.tile(sin_h, (1, N_HEADS))


def _rep_kv(w):
    return jnp.concatenate([w[:, :HEAD_DIM], w[:, :HEAD_DIM], w[:, HEAD_DIM:], w[:, HEAD_DIM:]], axis=1)


def _w_in_sections(w):
    return jnp.concatenate([w[:, 0:1024], _rep_kv(w[:, 1024:1152]), _rep_kv(w[:, 1152:1280]), w[:, 1280:2048],
                            _rep_kv(w[:, 2048:2176]), _rep_kv(w[:, 2176:2304])], axis=1)


def kernel(x, p, rel_bias, ln_mix_g, w_in, qk_gain, sink, c_norm_g, c_norm_b, c_ws, c_bs, out_gain, w_out,
           ln_ffn_g, w_up, conv_w, conv_b, w_down, ln_ple_g, w_ple_gate, w_ple_proj):
    b, s, _ = x.shape
    n = b * s
    depth = w_in.shape[0]
    cos_t, sin_t = _rope_tables(s)
    bd = (np.arange(GROUP_WIDTH)[:, None] // HEAD_DIM == np.arange(GROUP_WIDTH)[None, :] // HEAD_DIM)
    bd = jnp.asarray(bd, BF16)
    table_a = rel_bias[:, :N_HEADS]
    table_b = rel_bias[:, N_HEADS:]
    bias_a = []
    for window, dil in DILATED_CFGS:
        radius = window // (2 * dil)
        length = s // dil
        tq = min(BAND_TQ, length)
        bias_a.append(_band_bias(table_a, radius, dil, tq, min(tq + 2 * radius, length)))
    tq_b = min(BAND_TQ, s)
    bias_b = _band_bias(table_b, SWA_RADIUS, 1, tq_b, min(tq_b + 2 * SWA_RADIUS, s))

    x2d = x.reshape(n, D_MODEL)
    for i in range(depth):
        w = _w_in_sections(w_in[i]).astype(BF16)
        qg = qk_gain[i]
        vec = jnp.stack([
            jnp.tile(qg[0, 0], N_HEADS) * ATTN_SCALE, jnp.tile(qg[0, 1], N_HEADS),
            jnp.tile(qg[1, 0], N_HEADS) * ATTN_SCALE, jnp.tile(qg[1, 1], N_HEADS),
            jnp.tile(qg[2, 0], N_HEADS) * ATTN_SCALE, jnp.tile(qg[2, 1], N_HEADS),
            c_norm_g[i], c_norm_b[i], out_gain[i, 2],
        ] + [jnp.zeros((GROUP_WIDTH,), F32)] * 7, axis=0)
        wcat = jnp.concatenate([c_ws[i, g] for g in range(C_GROUPS)], axis=1).astype(BF16)
        bsx = jnp.repeat(jnp.transpose(c_bs[i]), GROUP_WIDTH // C_GROUPS, axis=1)
        proj2d, d4, d16 = _proj_call(x2d, ln_mix_g[i][None], w, vec, cos_t, sin_t, bd, wcat, bsx, b, s)

        proj3d = proj2d.reshape(b, s, N_SEC * GROUP_WIDTH)
        ya = _dilated_call(proj3d, d4, d16, bias_a, out_gain[i, 0][None])
        sink_blk = jnp.full((N_HEADS * tq_b, LANES), NEG_INF, F32).at[:, 0].set(jnp.repeat(sink[i], tq_b))
        yb = _swa_call(proj3d, bias_b, sink_blk, out_gain[i, 1][None])
        yd = _dense_call(proj3d, out_gain[i, 3][None])

        x1, xn = _mix_call(x2d, ya.reshape(n, GROUP_WIDTH), yb.reshape(n, GROUP_WIDTH), proj2d,
                           yd.reshape(n, GROUP_WIDTH), w_out[i].astype(BF16), ln_ffn_g[i][None])
        wup = jnp.transpose(w_up[i].astype(BF16).reshape(D_MODEL, 2 * N_CHUNKS, FFN_TF), (1, 0, 2))
        cw = jnp.concatenate([conv_w[i], conv_b[i][None]], axis=0)
        cw = jnp.transpose(cw.reshape(4, 2 * N_CHUNKS, FFN_TF), (1, 0, 2))
        wd = w_down[i].astype(BF16).reshape(N_CHUNKS, FFN_TF, D_MODEL)
        x2d = _ffn_call(xn, x1, p[i].reshape(n, PLE_DIM), wup, cw, wd, ln_ple_g[i][None],
                        w_ple_gate[i].astype(BF16), w_ple_proj[i].astype(BF16), s)
    return x2d.reshape(b, s, D_MODEL)
```

```python
import functools
import math

import jax
import jax.numpy as jnp
import numpy as np
from jax import lax
from jax.experimental import pallas as pl
from jax.experimental.pallas import tpu as pltpu

D_MODEL = 1024
HEAD_DIM = 64
GROUP_WIDTH = 256
N_HEADS = GROUP_WIDTH // HEAD_DIM
C_GROUPS = 4
C_CHUNK = 128
DILATED_CFGS = ((128, 1), (512, 4), (2048, 16))
SWA_RADIUS = 128
GRID_W = 64
ROPE_THETA = 10000.0
REL_BUCKETS = 32
REL_MAX_DIST = 1024
D_FF = 2816
PLE_DIM = 256
EPS = 1e-6
NEG_INF = -1e30
ATTN_SCALE = HEAD_DIM ** -0.5
LOG2E = math.log2(math.e)

BF16 = jnp.bfloat16
F32 = jnp.float32

SEC_AQ, SEC_AK, SEC_AV, SEC_BQ, SEC_BK, SEC_BV, SEC_YC, SEC_DQ, SEC_DK, SEC_DV = range(10)
N_SEC = 10
A_WIDTH = 3 * GROUP_WIDTH

VMEM_LIMIT = 56 * 1024 * 1024

KIN_TM = 512
BAND_TQ = 128
BAND_UNROLL = 8
SWA_UNROLL = 2
DENSE_TQ = 512
DENSE_ROWS = 1024
MIX_TM = 512
FFN_TM = 512
FFN_TF = 256
LANES = 128
SUB = 8
HALO = 16
N_CHUNKS = D_FF // FFN_TF


def _params(sem):
    return pltpu.CompilerParams(dimension_semantics=sem, vmem_limit_bytes=VMEM_LIMIT)


def _resident(shape):
    zeros = (0,) * len(shape)
    return pl.BlockSpec(shape, lambda *_: zeros, pipeline_mode=pl.Buffered(1))


def _rms(t, gain):
    return t * lax.rsqrt(jnp.mean(t * t, axis=-1, keepdims=True) + EPS) * gain


def _head_id(shape):
    return lax.broadcasted_iota(jnp.int32, shape, 1) >> 6


def _proj_kernel(x_ref, g_ref, w_ref, vec_ref, cos_ref, sin_ref, bd_ref, wcat_ref, bsx_ref,
                 o_ref, d4_ref, d16_ref, a_ref):
    x = x_ref[...]
    tm = x.shape[0]
    hn = _rms(x, g_ref[...]).astype(BF16)
    bd = bd_ref[...]

    def proj(sec):
        return jnp.dot(hn, w_ref[:, sec * GROUP_WIDTH:(sec + 1) * GROUP_WIDTH],
                       preferred_element_type=F32)

    def store(sec, val):
        o_ref[:, sec * GROUP_WIDTH:(sec + 1) * GROUP_WIDTH] = val.astype(BF16)

    def store_a(sec, val):
        store(sec, val)
        for half in range(GROUP_WIDTH // LANES):
            slot = sec * (GROUP_WIDTH // LANES) + half
            a_ref[slot] = val[:, half * LANES:(half + 1) * LANES]
            cols = slice(slot * LANES, (slot + 1) * LANES)
            for dil, ref in ((4, d4_ref), (16, d16_ref)):
                for r in range(dil):
                    ref[r, :, cols] = a_ref[slot, pl.ds(r, tm // dil, stride=dil), :].astype(BF16)

    def head_rms(t, row):
        width = t.shape[1]
        sq = t * t
        hi = sq.astype(BF16)
        lo = (sq - hi.astype(F32)).astype(BF16)
        blk = bd[0:width, 0:width]
        ms = (jnp.dot(hi, blk, preferred_element_type=F32) + jnp.dot(lo, blk, preferred_element_type=F32))
        return t * lax.rsqrt(ms + EPS) * vec_ref[row:row + 1, 0:width]

    def rope(t):
        blocks = []
        for c0 in range(0, t.shape[1], LANES):
            th = t[:, c0:c0 + LANES]
            lane = lax.broadcasted_iota(jnp.int32, th.shape, 1)
            partner = jnp.where((lane & 31) < 16, pltpu.roll(th, LANES - 16, axis=1), pltpu.roll(th, 16, axis=1))
            blocks.append(th * cos_ref[:, c0:c0 + LANES] + partner * sin_ref[:, c0:c0 + LANES])
        return blocks[0] if len(blocks) == 1 else jnp.concatenate(blocks, axis=-1)

    def rep_kv(t):
        swapped = pltpu.roll(t, HEAD_DIM, axis=1)
        low = lax.broadcasted_iota(jnp.int32, t.shape, 1) < HEAD_DIM
        return jnp.concatenate([jnp.where(low, t, swapped), jnp.where(low, swapped, t)], axis=-1)

    store_a(SEC_AQ, head_rms(proj(0), 0))
    store_a(SEC_AK, head_rms(proj(1), 1))
    store_a(SEC_AV, proj(2))
    store(SEC_BQ, head_rms(proj(3), 2))
    b_kv = proj(4)
    store(SEC_BK, rep_kv(head_rms(b_kv[:, 0:LANES], 3)))
    store(SEC_BV, rep_kv(b_kv[:, LANES:]))
    store(SEC_DQ, rope(head_rms(proj(7), 4)))
    d_kv = proj(8)
    store(SEC_DK, rep_kv(rope(head_rms(d_kv[:, 0:LANES], 5))))
    store(SEC_DV, rep_kv(d_kv[:, LANES:]))

    u = jax.nn.gelu(proj(5))
    v = jax.nn.gelu(proj(6))
    mu = jnp.mean(v, axis=-1, keepdims=True)
    vc = v - mu
    var = jnp.mean(vc * vc, axis=-1, keepdims=True)
    vn = (vc * lax.rsqrt(var + EPS) * vec_ref[6:7, :] + vec_ref[7:8, :]).astype(BF16)
    gid = _head_id((C_CHUNK, GROUP_WIDTH))
    zero = jnp.zeros((C_CHUNK, GROUP_WIDTH), BF16)
    for c in range(tm // C_CHUNK):
        rows = slice(c * C_CHUNK, (c + 1) * C_CHUNK)
        vchunk = vn[rows]
        stacked = jnp.concatenate([jnp.where(gid == g, vchunk, zero) for g in range(C_GROUPS)], axis=0)
        mixed = jnp.dot(wcat_ref[...], stacked, preferred_element_type=F32) + bsx_ref[...]
        yc = _rms(u[rows] * mixed, vec_ref[8:9, :])
        o_ref[rows, SEC_YC * GROUP_WIDTH:(SEC_YC + 1) * GROUP_WIDTH] = yc.astype(BF16)


def _proj_call(x2d, g, w, vec, cos_t, sin_t, bd, wcat, bsx, batch, seq):
    n = x2d.shape[0]
    tps = seq // KIN_TM
    return pl.pallas_call(
        _proj_kernel,
        grid=(n // KIN_TM,),
        in_specs=[
            pl.BlockSpec((KIN_TM, D_MODEL), lambda i: (i, 0)),
            _resident((1, D_MODEL)),
            _resident(w.shape),
            _resident(vec.shape),
            pl.BlockSpec((KIN_TM, GROUP_WIDTH), lambda i: (i % tps, 0)),
            pl.BlockSpec((KIN_TM, GROUP_WIDTH), lambda i: (i % tps, 0)),
            _resident(bd.shape),
            _resident(wcat.shape),
            _resident(bsx.shape),
        ],
        out_specs=[
            pl.BlockSpec((KIN_TM, N_SEC * GROUP_WIDTH), lambda i: (i, 0)),
            pl.BlockSpec((None, 4, KIN_TM // 4, A_WIDTH), lambda i: (i // tps, 0, i % tps, 0)),
            pl.BlockSpec((None, 16, KIN_TM // 16, A_WIDTH), lambda i: (i // tps, 0, i % tps, 0)),
        ],
        out_shape=[
            jax.ShapeDtypeStruct((n, N_SEC * GROUP_WIDTH), BF16),
            jax.ShapeDtypeStruct((batch, 4, seq // 4, A_WIDTH), BF16),
            jax.ShapeDtypeStruct((batch, 16, seq // 16, A_WIDTH), BF16),
        ],
        scratch_shapes=[pltpu.VMEM((A_WIDTH // LANES, KIN_TM, LANES), F32)],
        compiler_params=_params(("parallel",)),
        name="proj",
    )(x2d, g, w, vec, cos_t, sin_t, bd, wcat, bsx)


def _attend(q, kwin, vwin, bias_ref, case, sink_ref=None):
    tq = q.shape[0]
    hid = _head_id((tq, GROUP_WIDTH))
    qzero = jnp.zeros_like(q)
    q4 = jnp.concatenate([jnp.where(hid == h, q, qzero) for h in range(N_HEADS)], axis=0)
    s = lax.dot_general(q4, kwin, (((1,), (1,)), ((), ())), preferred_element_type=F32)
    s = s + bias_ref[case]
    if sink_ref is None:
        m = jnp.max(s, axis=-1, keepdims=True)
        p = jnp.exp2(s - m)
        den = jnp.sum(p, axis=-1, keepdims=True)
    else:
        sink = sink_ref[...]
        blocks = [s[:, c:c + LANES] for c in range(0, s.shape[1], LANES)]
        m = jnp.max(functools.reduce(jnp.maximum, blocks + [sink]), axis=-1, keepdims=True)
        p = jnp.exp2(s - m)
        pblocks = [p[:, c:c + LANES] for c in range(0, s.shape[1], LANES)]
        den = jnp.sum(functools.reduce(jnp.add, pblocks + [jnp.exp2(sink - m)]), axis=-1, keepdims=True)
    o4 = jnp.dot(p.astype(BF16), vwin, preferred_element_type=F32) * (1.0 / den)
    l4 = jnp.broadcast_to(m + jnp.log2(den), o4.shape)
    out = o4[0:tq]
    lse = l4[0:tq]
    for h in range(1, N_HEADS):
        out = jnp.where(hid == h, o4[h * tq:(h + 1) * tq], out)
        lse = jnp.where(hid == h, l4[h * tq:(h + 1) * tq], lse)
    return out, lse


def _band_window(j, tq, radius, length, kw):
    n_tiles = length // tq
    if n_tiles == 1:
        return 0, 0
    ks = pl.multiple_of(jnp.clip(j * tq - radius, 0, length - kw), 64)
    case = jnp.where(j == 0, 0, jnp.where(j == n_tiles - 1, 2, 1))
    return ks, case


def _dilated_kernel(q_ref, k_ref, v_ref, d4_ref, d16_ref, b1_ref, b4_ref, b16_ref, gain_ref, o_ref,
                    acc_ref, lse_ref, *, radii):
    seq = q_ref.shape[0]
    tq = BAND_TQ

    kw1 = min(tq + 2 * radii[0], seq)

    def body1(j, carry):
        q0 = pl.multiple_of(j * tq, tq)
        ks, case = _band_window(j, tq, radii[0], seq, kw1)
        o, l = _attend(q_ref[pl.ds(q0, tq), :], k_ref[pl.ds(ks, kw1), :], v_ref[pl.ds(ks, kw1), :], b1_ref, case)
        for half in range(GROUP_WIDTH // LANES):
            cols = slice(half * LANES, (half + 1) * LANES)
            acc_ref[half, pl.ds(q0, tq), :] = o[:, cols]
            lse_ref[half, pl.ds(q0, tq), :] = l[:, cols]
        return carry

    lax.fori_loop(0, seq // tq, body1, 0, unroll=BAND_UNROLL)

    for dil, ref, bias_ref, radius in ((4, d4_ref, b4_ref, radii[1]), (16, d16_ref, b16_ref, radii[2])):
        length = seq // dil
        tiles = length // tq
        kw = min(tq + 2 * radius, length)

        def body(it, carry, dil=dil, ref=ref, bias_ref=bias_ref, radius=radius, length=length, tiles=tiles, kw=kw):
            r = it // tiles
            j = it % tiles
            q0 = pl.multiple_of(j * tq, tq)
            ks, case = _band_window(j, tq, radius, length, kw)
            o, l = _attend(ref[r, pl.ds(q0, tq), 0:GROUP_WIDTH],
                           ref[r, pl.ds(ks, kw), GROUP_WIDTH:2 * GROUP_WIDTH],
                           ref[r, pl.ds(ks, kw), 2 * GROUP_WIDTH:3 * GROUP_WIDTH], bias_ref, case)
            rows = pl.ds(r + dil * q0, tq, stride=dil)
            for half in range(GROUP_WIDTH // LANES):
                cols = slice(half * LANES, (half + 1) * LANES)
                l_new = l[:, cols]
                l_old = lse_ref[half, rows, :]
                m = jnp.maximum(l_old, l_new)
                wa = jnp.exp2(l_old - m)
                wb = jnp.exp2(l_new - m)
                tot = wa + wb
                acc_ref[half, rows, :] = (wa * acc_ref[half, rows, :] + wb * o[:, cols]) * (1.0 / tot)
                lse_ref[half, rows, :] = m + jnp.log2(tot)
            return carry

        lax.fori_loop(0, dil * tiles, body, 0, unroll=BAND_UNROLL)

    acc = jnp.concatenate([acc_ref[0], acc_ref[1]], axis=-1)
    o_ref[...] = _rms(acc, gain_ref[...]).astype(BF16)


def _dilated_call(proj3d, d4, d16, biases, gain):
    b, seq, _ = proj3d.shape
    radii = tuple(window // (2 * dil) for window, dil in DILATED_CFGS)
    kern = functools.partial(_dilated_kernel, radii=radii)
    return pl.pallas_call(
        kern,
        grid=(b,),
        in_specs=[
            pl.BlockSpec((None, seq, GROUP_WIDTH), lambda bi: (bi, 0, SEC_AQ)),
            pl.BlockSpec((None, seq, GROUP_WIDTH), lambda bi: (bi, 0, SEC_AK)),
            pl.BlockSpec((None, seq, GROUP_WIDTH), lambda bi: (bi, 0, SEC_AV)),
            pl.BlockSpec((None,) + d4.shape[1:], lambda bi: (bi, 0, 0, 0)),
            pl.BlockSpec((None,) + d16.shape[1:], lambda bi: (bi, 0, 0, 0)),
            _resident(biases[0].shape), _resident(biases[1].shape), _resident(biases[2].shape),
            _resident((1, GROUP_WIDTH)),
        ],
        out_specs=pl.BlockSpec((None, seq, GROUP_WIDTH), lambda bi: (bi, 0, 0)),
        out_shape=jax.ShapeDtypeStruct((b, seq, GROUP_WIDTH), BF16),
        scratch_shapes=[pltpu.VMEM((GROUP_WIDTH // LANES, seq, LANES), F32)] * 2,
        compiler_params=_params(("parallel",)),
        name="dilated",
    )(proj3d, proj3d, proj3d, d4, d16, biases[0], biases[1], biases[2], gain)


def _swa_kernel(q_ref, k_ref, v_ref, bias_ref, sink_ref, gain_ref, o_ref):
    seq = q_ref.shape[0]
    tq = BAND_TQ
    kw = min(tq + 2 * SWA_RADIUS, seq)

    def body(j, carry):
        q0 = pl.multiple_of(j * tq, tq)
        ks, case = _band_window(j, tq, SWA_RADIUS, seq, kw)
        o, _ = _attend(q_ref[pl.ds(q0, tq), :], k_ref[pl.ds(ks, kw), :], v_ref[pl.ds(ks, kw), :],
                       bias_ref, case, sink_ref)
        o_ref[pl.ds(q0, tq), :] = _rms(o, gain_ref[...]).astype(BF16)
        return carry

    lax.fori_loop(0, seq // tq, body, 0, unroll=SWA_UNROLL)


def _swa_call(proj3d, bias, sink, gain):
    b, seq, _ = proj3d.shape
    return pl.pallas_call(
        _swa_kernel,
        grid=(b,),
        in_specs=[
            pl.BlockSpec((None, seq, GROUP_WIDTH), lambda bi: (bi, 0, SEC_BQ)),
            pl.BlockSpec((None, seq, GROUP_WIDTH), lambda bi: (bi, 0, SEC_BK)),
            pl.BlockSpec((None, seq, GROUP_WIDTH), lambda bi: (bi, 0, SEC_BV)),
            _resident(bias.shape),
            _resident(sink.shape),
            _resident((1, GROUP_WIDTH)),
        ],
        out_specs=pl.BlockSpec((None, seq, GROUP_WIDTH), lambda bi: (bi, 0, 0)),
        out_shape=jax.ShapeDtypeStruct((b, seq, GROUP_WIDTH), BF16),
        compiler_params=_params(("parallel",)),
        name="swa",
    )(proj3d, proj3d, proj3d, bias, sink, gain)


def _dense_kernel(q_ref, k_ref, v_ref, gain_ref, o_ref):
    k = k_ref[...]
    v = v_ref[...]
    tq = DENSE_TQ
    hid = _head_id((tq, GROUP_WIDTH))
    for t in range(q_ref.shape[0] // tq):
        q = q_ref[t * tq:(t + 1) * tq, :]
        qzero = jnp.zeros_like(q)
        acc = jnp.zeros((tq, GROUP_WIDTH), F32)
        for h in range(N_HEADS):
            qh = jnp.where(hid == h, q, qzero)
            s = lax.dot_general(qh, k, (((1,), (1,)), ((), ())), preferred_element_type=F32)
            m = jnp.max(s, axis=-1, keepdims=True)
            p = jnp.exp2(s - m)
            den = jnp.sum(p, axis=-1, keepdims=True)
            oh = jnp.dot(p.astype(BF16), v, preferred_element_type=F32) * (1.0 / den)
            acc = jnp.where(hid == h, oh, acc)
        o_ref[t * tq:(t + 1) * tq, :] = _rms(acc, gain_ref[...]).astype(BF16)


def _dense_call(proj3d, gain):
    b, seq, _ = proj3d.shape
    return pl.pallas_call(
        _dense_kernel,
        grid=(b, seq // DENSE_ROWS),
        in_specs=[
            pl.BlockSpec((None, DENSE_ROWS, GROUP_WIDTH), lambda bi, j: (bi, j, SEC_DQ)),
            pl.BlockSpec((None, seq, GROUP_WIDTH), lambda bi, j: (bi, 0, SEC_DK)),
            pl.BlockSpec((None, seq, GROUP_WIDTH), lambda bi, j: (bi, 0, SEC_DV)),
            _resident((1, GROUP_WIDTH)),
        ],
        out_specs=pl.BlockSpec((None, DENSE_ROWS, GROUP_WIDTH), lambda bi, j: (bi, j, 0)),
        out_shape=jax.ShapeDtypeStruct((b, seq, GROUP_WIDTH), BF16),
        compiler_params=_params(("parallel", "arbitrary")),
        name="dense",
    )(proj3d, proj3d, proj3d, gain)


def _mix_kernel(x_ref, ya_ref, yb_ref, yc_ref, yd_ref, w_ref, g_ref, x1_ref, xn_ref):
    mixed = jnp.concatenate([ya_ref[...], yb_ref[...], yc_ref[...], yd_ref[...]], axis=-1)
    x1 = x_ref[...] + jnp.dot(mixed, w_ref[...], preferred_element_type=F32)
    x1_ref[...] = x1
    xn_ref[...] = _rms(x1, g_ref[...]).astype(BF16)


def _mix_call(x2d, ya, yb, proj2d, yd, w_out, g_ffn):
    n = x2d.shape[0]
    row = lambda i: (i, 0)
    grp = pl.BlockSpec((MIX_TM, GROUP_WIDTH), row)
    return pl.pallas_call(
        _mix_kernel,
        grid=(n // MIX_TM,),
        in_specs=[pl.BlockSpec((MIX_TM, D_MODEL), row), grp, grp,
                  pl.BlockSpec((MIX_TM, GROUP_WIDTH), lambda i: (i, SEC_YC)), grp,
                  _resident(w_out.shape), _resident((1, D_MODEL))],
        out_specs=[pl.BlockSpec((MIX_TM, D_MODEL), row), pl.BlockSpec((MIX_TM, D_MODEL), row)],
        out_shape=[jax.ShapeDtypeStruct((n, D_MODEL), F32), jax.ShapeDtypeStruct((n, D_MODEL), BF16)],
        compiler_params=_params(("parallel",)),
        name="mix_out",
    )(x2d, ya, yb, proj2d, yd, w_out, g_ffn)


def _ffn_kernel(xn_ref, prev_ref, next_ref, x1_ref, p_ref, wup_ref, cw_ref, wd_ref, gple_ref, wgate_ref,
                wproj_ref, o_ref, xcat_ref, h0_ref, h1_ref, *, tiles_per_seq):
    pos = pl.program_id(0) % tiles_per_seq
    tm = xn_ref.shape[0]
    rows = tm + HALO

    prev_blk = jnp.where(pos == 0, jnp.zeros_like(prev_ref[...]), prev_ref[...])
    next_blk = jnp.where(pos == tiles_per_seq - 1, jnp.zeros_like(next_ref[...]), next_ref[...])
    halo_row = lax.broadcasted_iota(jnp.int32, (HALO, D_MODEL), 0)
    xcat_ref[0:tm, :] = xn_ref[...]
    xcat_ref[tm:rows, :] = jnp.where(halo_row == 0, next_blk, prev_blk)
    o_ref[...] = x1_ref[...]

    def up(k, h_ref):
        xcat = xcat_ref[...]
        for part in range(2):
            h = jnp.dot(xcat, wup_ref[k + part * N_CHUNKS], preferred_element_type=F32)
            h_ref[part, SUB:SUB + tm, :] = h[0:tm]
            h_ref[part, SUB + tm:2 * SUB + tm, :] = h[tm:tm + SUB]
            h_ref[part, 0:SUB, :] = h[rows - SUB:rows]

    def conv(h_ref, part, cw):
        before = h_ref[part, SUB - 1:SUB - 1 + tm, :]
        centre = h_ref[part, SUB:SUB + tm, :]
        after = h_ref[part, SUB + 1:SUB + 1 + tm, :]
        return before * cw[0:1, :] + centre * cw[1:2, :] + after * cw[2:3, :] + cw[3:4, :]

    def down(k, h_ref):
        gt = conv(h_ref, 0, cw_ref[k])
        uu = conv(h_ref, 1, cw_ref[k + N_CHUNKS])
        act = (gt / (1.0 + jnp.exp(-gt)) * uu).astype(BF16)
        o_ref[...] += jnp.dot(act, wd_ref[k], preferred_element_type=F32)

    up(0, h0_ref)

    def pair(t, carry):
        k = 2 * t
        up(k + 1, h1_ref)
        down(k, h0_ref)
        up(k + 2, h0_ref)
        down(k + 1, h1_ref)
        return carry

    lax.fori_loop(0, (N_CHUNKS - 1) // 2, pair, 0)
    down(N_CHUNKS - 1, h0_ref)

    x2 = o_ref[...]
    z = jnp.dot(_rms(x2, gple_ref[...]).astype(BF16), wgate_ref[...], preferred_element_type=F32)
    gate = 1.0 / (1.0 + jnp.exp(-z))
    inj = jnp.dot(p_ref[...].astype(BF16), wproj_ref[...], preferred_element_type=F32)
    o_ref[...] = x2 + inj * gate


def _ffn_call(xn, x1, p2d, wup, cw, wd, g_ple, w_gate, w_proj, seq):
    assert N_CHUNKS % 2 == 1
    n = xn.shape[0]
    halo_per_tile = FFN_TM // HALO
    n_halo = n // HALO
    row = lambda i: (i, 0)
    kern = functools.partial(_ffn_kernel, tiles_per_seq=seq // FFN_TM)
    return pl.pallas_call(
        kern,
        grid=(n // FFN_TM,),
        in_specs=[
            pl.BlockSpec((FFN_TM, D_MODEL), row),
            pl.BlockSpec((HALO, D_MODEL), lambda i: (jnp.maximum(i * halo_per_tile - 1, 0), 0)),
            pl.BlockSpec((HALO, D_MODEL), lambda i: (jnp.minimum((i + 1) * halo_per_tile, n_halo - 1), 0)),
            pl.BlockSpec((FFN_TM, D_MODEL), row),
            pl.BlockSpec((FFN_TM, PLE_DIM), row),
            _resident(wup.shape), _resident(cw.shape), _resident(wd.shape),
            _resident((1, D_MODEL)), _resident(w_gate.shape), _resident(w_proj.shape),
        ],
        out_specs=pl.BlockSpec((FFN_TM, D_MODEL), row),
        out_shape=jax.ShapeDtypeStruct((n, D_MODEL), F32),
        scratch_shapes=[pltpu.VMEM((FFN_TM + HALO, D_MODEL), BF16),
                        pltpu.VMEM((2, FFN_TM + HALO, FFN_TF), F32),
                        pltpu.VMEM((2, FFN_TM + HALO, FFN_TF), F32)],
        compiler_params=_params(("parallel",)),
        name="ffn_ple",
    )(xn, xn, xn, x1, p2d, wup, cw, wd, g_ple, w_gate, w_proj)


def _t5_bucket(rel):
    nb = REL_BUCKETS // 2
    ret = jnp.where(rel > 0, nb, 0)
    n = jnp.abs(rel)
    max_exact = nb // 2
    nf = jnp.maximum(n, 1).astype(F32)
    large = max_exact + (jnp.log(nf / max_exact) / math.log(REL_MAX_DIST / max_exact)
                         * (nb - max_exact)).astype(jnp.int32)
    large = jnp.minimum(large, nb - 1)
    return ret + jnp.where(n < max_exact, n, large)


def _band_bias(table, radius, dil, tq, kw):
    cases = []
    for off in (0, radius, kw - tq):
        rel = jnp.arange(kw)[None, :] - jnp.arange(tq)[:, None] - off
        onehot = (_t5_bucket(rel * dil)[..., None] == jnp.arange(REL_BUCKETS)).astype(F32)
        bias = jnp.einsum('qkn,nh->hqk', onehot, table.astype(F32), precision=lax.Precision.HIGHEST)
        bias = jnp.where((jnp.abs(rel) <= radius)[None], bias, NEG_INF)
        cases.append(bias.reshape(N_HEADS * tq, kw))
    return jnp.stack(cases, axis=0)


def _rope_tables(seq):
    half = HEAD_DIM // 2
    inv = ROPE_THETA ** (-jnp.arange(0, half, 2, dtype=F32) / half)
    t = jnp.arange(seq)
    ang_r = (t // GRID_W).astype(F32)[:, None] * inv[None, :]
    ang_c = (t % GRID_W).astype(F32)[:, None] * inv[None, :]
    cos_h = jnp.concatenate([jnp.cos(ang_r)] * 2 + [jnp.cos(ang_c)] * 2, axis=-1)
    sin_h = jnp.concatenate([-jnp.sin(ang_r), jnp.sin(ang_r), -jnp.sin(ang_c), jnp.sin(ang_c)], axis=-1)
    return jnp.tile(cos_h, (1, N_HEADS)), jnp.tile(sin_h, (1, N_HEADS))


def kernel(x, p, rel_bias, ln_mix_g, w_in, qk_gain, sink, c_norm_g, c_norm_b, c_ws, c_bs, out_gain, w_out,
           ln_ffn_g, w_up, conv_w, conv_b, w_down, ln_ple_g, w_ple_gate, w_ple_proj):
    b, s, _ = x.shape
    n = b * s
    depth = w_in.shape[0]
    cos_t, sin_t = _rope_tables(s)
    bd = (np.arange(GROUP_WIDTH)[:, None] // HEAD_DIM == np.arange(GROUP_WIDTH)[None, :] // HEAD_DIM)
    bd = jnp.asarray(bd / HEAD_DIM, BF16)
    table_a = rel_bias[:, :N_HEADS] * LOG2E
    table_b = rel_bias[:, N_HEADS:] * LOG2E
    q_scale = ATTN_SCALE * LOG2E
    bias_a = []
    for window, dil in DILATED_CFGS:
        radius = window // (2 * dil)
        length = s // dil
        tq = min(BAND_TQ, length)
        bias_a.append(_band_bias(table_a, radius, dil, tq, min(tq + 2 * radius, length)))
    tq_b = min(BAND_TQ, s)
    bias_b = _band_bias(table_b, SWA_RADIUS, 1, tq_b, min(tq_b + 2 * SWA_RADIUS, s))

    x2d = x.reshape(n, D_MODEL)
    for i in range(depth):
        w = w_in[i].astype(BF16)
        qg = qk_gain[i]
        vec = jnp.stack([
            jnp.tile(qg[0, 0], N_HEADS) * q_scale, jnp.tile(qg[0, 1], N_HEADS),
            jnp.tile(qg[1, 0], N_HEADS) * q_scale, jnp.tile(qg[1, 1], N_HEADS),
            jnp.tile(qg[2, 0], N_HEADS) * q_scale, jnp.tile(qg[2, 1], N_HEADS),
            c_norm_g[i], c_norm_b[i], out_gain[i, 2],
        ] + [jnp.zeros((GROUP_WIDTH,), F32)] * 7, axis=0)
        wcat = jnp.concatenate([c_ws[i, g] for g in range(C_GROUPS)], axis=1).astype(BF16)
        bsx = jnp.repeat(jnp.transpose(c_bs[i]), GROUP_WIDTH // C_GROUPS, axis=1)
        proj2d, d4, d16 = _proj_call(x2d, ln_mix_g[i][None], w, vec, cos_t, sin_t, bd, wcat, bsx, b, s)

        proj3d = proj2d.reshape(b, s, N_SEC * GROUP_WIDTH)
        ya = _dilated_call(proj3d, d4, d16, bias_a, out_gain[i, 0][None])
        sink_blk = jnp.full((N_HEADS * tq_b, LANES), NEG_INF, F32).at[:, 0].set(jnp.repeat(sink[i] * LOG2E, tq_b))
        yb = _swa_call(proj3d, bias_b, sink_blk, out_gain[i, 1][None])
        yd = _dense_call(proj3d, out_gain[i, 3][None])

        x1, xn = _mix_call(x2d, ya.reshape(n, GROUP_WIDTH), yb.reshape(n, GROUP_WIDTH), proj2d,
                           yd.reshape(n, GROUP_WIDTH), w_out[i].astype(BF16), ln_ffn_g[i][None])
        wup = jnp.transpose(w_up[i].astype(BF16).reshape(D_MODEL, 2 * N_CHUNKS, FFN_TF), (1, 0, 2))
        cw = jnp.concatenate([conv_w[i], conv_b[i][None]], axis=0)
        cw = jnp.transpose(cw.reshape(4, 2 * N_CHUNKS, FFN_TF), (1, 0, 2))
        wd = w_down[i].astype(BF16).reshape(N_CHUNKS, FFN_TF, D_MODEL)
        x2d = _ffn_call(xn, x1, p[i].reshape(n, PLE_DIM), wup, cw, wd, ln_ple_g[i][None],
                        w_ple_gate[i].astype(BF16), w_ple_proj[i].astype(BF16), s)
    return x2d.reshape(b, s, D_MODEL)
```

```python
import functools
import math

import jax
import jax.numpy as jnp
import numpy as np
from jax import lax
from jax.experimental import pallas as pl
from jax.experimental.pallas import tpu as pltpu

D_MODEL = 1024
HEAD_DIM = 64
GROUP_WIDTH = 256
N_HEADS = GROUP_WIDTH // HEAD_DIM
C_GROUPS = 4
C_CHUNK = 128
DILATED_CFGS = ((128, 1), (512, 4), (2048, 16))
SWA_RADIUS = 128
GRID_W = 64
ROPE_THETA = 10000.0
REL_BUCKETS = 32
REL_MAX_DIST = 1024
D_FF = 2816
PLE_DIM = 256
EPS = 1e-6
NEG_INF = -1e30
ATTN_SCALE = HEAD_DIM ** -0.5
LOG2E = math.log2(math.e)

BF16 = jnp.bfloat16
F32 = jnp.float32

SEC_AQ, SEC_AK, SEC_AV, SEC_BQ, SEC_BK, SEC_BV, SEC_YC, SEC_DQ, SEC_DK, SEC_DV = range(10)
N_SEC = 10
A_WIDTH = 3 * GROUP_WIDTH

VMEM_LIMIT = 56 * 1024 * 1024

KIN_TM = 512
BAND_TQ = 128
BAND_UNROLL = 8
SWA_UNROLL = 2
DENSE_TQ = 512
DENSE_ROWS = 1024
MIX_TM = 512
FFN_TM = 512
FFN_TF = 256
LANES = 128
SUB = 8
STRANDS = 8
HALO = 16
N_CHUNKS = D_FF // FFN_TF


def _params(sem):
    return pltpu.CompilerParams(dimension_semantics=sem, vmem_limit_bytes=VMEM_LIMIT)


def _resident(shape):
    zeros = (0,) * len(shape)
    return pl.BlockSpec(shape, lambda *_: zeros, pipeline_mode=pl.Buffered(1))


def _rms(t, gain):
    return t * lax.rsqrt(jnp.mean(t * t, axis=-1, keepdims=True) + EPS) * gain


def _head_id(shape):
    return lax.broadcasted_iota(jnp.int32, shape, 1) >> 6


def _proj_kernel(x_ref, g_ref, w_ref, vec_ref, cos_ref, sin_ref, bd_ref, wcat_ref, bsx_ref,
                 o_ref, d4_ref, d16_ref, a_ref):
    x = x_ref[...]
    tm = x.shape[0]
    hn = _rms(x, g_ref[...]).astype(BF16)
    bd = bd_ref[...]

    def proj(sec):
        return jnp.dot(hn, w_ref[:, sec * GROUP_WIDTH:(sec + 1) * GROUP_WIDTH],
                       preferred_element_type=F32)

    def store(sec, val):
        o_ref[:, sec * GROUP_WIDTH:(sec + 1) * GROUP_WIDTH] = val.astype(BF16)

    def store_a(sec, val):
        store(sec, val)
        for half in range(GROUP_WIDTH // LANES):
            slot = sec * (GROUP_WIDTH // LANES) + half
            a_ref[slot] = val[:, half * LANES:(half + 1) * LANES]
            cols = slice(slot * LANES, (slot + 1) * LANES)
            for dil, ref in ((4, d4_ref), (16, d16_ref)):
                for r in range(dil):
                    ref[r, :, cols] = a_ref[slot, pl.ds(r, tm // dil, stride=dil), :].astype(BF16)

    def head_rms(t, row):
        width = t.shape[1]
        sq = t * t
        hi = sq.astype(BF16)
        lo = (sq - hi.astype(F32)).astype(BF16)
        blk = bd[0:width, 0:width]
        ms = (jnp.dot(hi, blk, preferred_element_type=F32) + jnp.dot(lo, blk, preferred_element_type=F32))
        return t * lax.rsqrt(ms + EPS) * vec_ref[row:row + 1, 0:width]

    def rope(t):
        blocks = []
        for c0 in range(0, t.shape[1], LANES):
            th = t[:, c0:c0 + LANES]
            lane = lax.broadcasted_iota(jnp.int32, th.shape, 1)
            partner = jnp.where((lane & 31) < 16, pltpu.roll(th, LANES - 16, axis=1), pltpu.roll(th, 16, axis=1))
            blocks.append(th * cos_ref[:, c0:c0 + LANES] + partner * sin_ref[:, c0:c0 + LANES])
        return blocks[0] if len(blocks) == 1 else jnp.concatenate(blocks, axis=-1)

    def rep_kv(t):
        swapped = pltpu.roll(t, HEAD_DIM, axis=1)
        low = lax.broadcasted_iota(jnp.int32, t.shape, 1) < HEAD_DIM
        return jnp.concatenate([jnp.where(low, t, swapped), jnp.where(low, swapped, t)], axis=-1)

    store_a(SEC_AQ, head_rms(proj(0), 0))
    store_a(SEC_AK, head_rms(proj(1), 1))
    store_a(SEC_AV, proj(2))
    store(SEC_BQ, head_rms(proj(3), 2))
    b_kv = proj(4)
    store(SEC_BK, rep_kv(head_rms(b_kv[:, 0:LANES], 3)))
    store(SEC_BV, rep_kv(b_kv[:, LANES:]))
    store(SEC_DQ, rope(head_rms(proj(7), 4)))
    d_kv = proj(8)
    store(SEC_DK, rep_kv(rope(head_rms(d_kv[:, 0:LANES], 5))))
    store(SEC_DV, rep_kv(d_kv[:, LANES:]))

    u = jax.nn.gelu(proj(5))
    v = jax.nn.gelu(proj(6))
    mu = jnp.mean(v, axis=-1, keepdims=True)
    vc = v - mu
    var = jnp.mean(vc * vc, axis=-1, keepdims=True)
    vn = (vc * lax.rsqrt(var + EPS) * vec_ref[6:7, :] + vec_ref[7:8, :]).astype(BF16)
    gid = _head_id((C_CHUNK, GROUP_WIDTH))
    zero = jnp.zeros((C_CHUNK, GROUP_WIDTH), BF16)
    for c in range(tm // C_CHUNK):
        rows = slice(c * C_CHUNK, (c + 1) * C_CHUNK)
        vchunk = vn[rows]
        stacked = jnp.concatenate([jnp.where(gid == g, vchunk, zero) for g in range(C_GROUPS)], axis=0)
        mixed = jnp.dot(wcat_ref[...], stacked, preferred_element_type=F32) + bsx_ref[...]
        yc = _rms(u[rows] * mixed, vec_ref[8:9, :])
        o_ref[rows, SEC_YC * GROUP_WIDTH:(SEC_YC + 1) * GROUP_WIDTH] = yc.astype(BF16)


def _proj_call(x2d, g, w, vec, cos_t, sin_t, bd, wcat, bsx, batch, seq):
    n = x2d.shape[0]
    tps = seq // KIN_TM
    return pl.pallas_call(
        _proj_kernel,
        grid=(n // KIN_TM,),
        in_specs=[
            pl.BlockSpec((KIN_TM, D_MODEL), lambda i: (i, 0)),
            _resident((1, D_MODEL)),
            _resident(w.shape),
            _resident(vec.shape),
            pl.BlockSpec((KIN_TM, GROUP_WIDTH), lambda i: (i % tps, 0)),
            pl.BlockSpec((KIN_TM, GROUP_WIDTH), lambda i: (i % tps, 0)),
            _resident(bd.shape),
            _resident(wcat.shape),
            _resident(bsx.shape),
        ],
        out_specs=[
            pl.BlockSpec((KIN_TM, N_SEC * GROUP_WIDTH), lambda i: (i, 0)),
            pl.BlockSpec((None, 4, KIN_TM // 4, A_WIDTH), lambda i: (i // tps, 0, i % tps, 0)),
            pl.BlockSpec((None, 16, KIN_TM // 16, A_WIDTH), lambda i: (i // tps, 0, i % tps, 0)),
        ],
        out_shape=[
            jax.ShapeDtypeStruct((n, N_SEC * GROUP_WIDTH), BF16),
            jax.ShapeDtypeStruct((batch, 4, seq // 4, A_WIDTH), BF16),
            jax.ShapeDtypeStruct((batch, 16, seq // 16, A_WIDTH), BF16),
        ],
        scratch_shapes=[pltpu.VMEM((A_WIDTH // LANES, KIN_TM, LANES), F32)],
        compiler_params=_params(("parallel",)),
        name="proj",
    )(x2d, g, w, vec, cos_t, sin_t, bd, wcat, bsx)


def _attend(q, kwin, vwin, bias_ref, case, sink_ref=None):
    tq = q.shape[0]
    hid = _head_id((tq, GROUP_WIDTH))
    qzero = jnp.zeros_like(q)
    q4 = jnp.concatenate([jnp.where(hid == h, q, qzero) for h in range(N_HEADS)], axis=0)
    s = lax.dot_general(q4, kwin, (((1,), (1,)), ((), ())), preferred_element_type=F32)
    s = s + bias_ref[case]
    if sink_ref is None:
        m = jnp.max(s, axis=-1, keepdims=True)
        p = jnp.exp2(s - m)
        den = jnp.sum(p, axis=-1, keepdims=True)
    else:
        sink = sink_ref[...]
        blocks = [s[:, c:c + LANES] for c in range(0, s.shape[1], LANES)]
        m = jnp.max(functools.reduce(jnp.maximum, blocks + [sink]), axis=-1, keepdims=True)
        p = jnp.exp2(s - m)
        pblocks = [p[:, c:c + LANES] for c in range(0, s.shape[1], LANES)]
        den = jnp.sum(functools.reduce(jnp.add, pblocks + [jnp.exp2(sink - m)]), axis=-1, keepdims=True)
    o4 = jnp.dot(p.astype(BF16), vwin, preferred_element_type=F32) * (1.0 / den)
    l4 = jnp.broadcast_to(m + jnp.log2(den), o4.shape)
    out = o4[0:tq]
    lse = l4[0:tq]
    for h in range(1, N_HEADS):
        out = jnp.where(hid == h, o4[h * tq:(h + 1) * tq], out)
        lse = jnp.where(hid == h, l4[h * tq:(h + 1) * tq], lse)
    return out, lse


def _band_window(j, tq, radius, length, kw):
    n_tiles = length // tq
    if n_tiles == 1:
        return 0, 0
    ks = pl.multiple_of(jnp.clip(j * tq - radius, 0, length - kw), 64)
    case = jnp.where(j == 0, 0, jnp.where(j == n_tiles - 1, 2, 1))
    return ks, case


def _dilated_kernel(q_ref, k_ref, v_ref, d4_ref, d16_ref, b1_ref, b4_ref, b16_ref, gain_ref, o_ref,
                    acc_ref, lse_ref, *, radii):
    seq = q_ref.shape[0]
    tq = BAND_TQ

    kw1 = min(tq + 2 * radii[0], seq)

    def body1(j, carry):
        q0 = pl.multiple_of(j * tq, tq)
        ks, case = _band_window(j, tq, radii[0], seq, kw1)
        o, l = _attend(q_ref[pl.ds(q0, tq), :], k_ref[pl.ds(ks, kw1), :], v_ref[pl.ds(ks, kw1), :], b1_ref, case)
        for half in range(GROUP_WIDTH // LANES):
            cols = slice(half * LANES, (half + 1) * LANES)
            acc_ref[half, pl.ds(q0, tq), :] = o[:, cols]
            lse_ref[half, pl.ds(q0, tq), :] = l[:, cols]
        return carry

    lax.fori_loop(0, seq // tq, body1, 0, unroll=BAND_UNROLL)

    for dil, ref, bias_ref, radius in ((4, d4_ref, b4_ref, radii[1]), (16, d16_ref, b16_ref, radii[2])):
        length = seq // dil
        tiles = length // tq
        kw = min(tq + 2 * radius, length)

        def body(it, carry, dil=dil, ref=ref, bias_ref=bias_ref, radius=radius, length=length, tiles=tiles, kw=kw):
            r = it // tiles
            j = it % tiles
            q0 = pl.multiple_of(j * tq, tq)
            ks, case = _band_window(j, tq, radius, length, kw)
            o, l = _attend(ref[r, pl.ds(q0, tq), 0:GROUP_WIDTH],
                           ref[r, pl.ds(ks, kw), GROUP_WIDTH:2 * GROUP_WIDTH],
                           ref[r, pl.ds(ks, kw), 2 * GROUP_WIDTH:3 * GROUP_WIDTH], bias_ref, case)
            rows = pl.ds(r + dil * q0, tq, stride=dil)
            for half in range(GROUP_WIDTH // LANES):
                cols = slice(half * LANES, (half + 1) * LANES)
                l_new = l[:, cols]
                l_old = lse_ref[half, rows, :]
                m = jnp.maximum(l_old, l_new)
                wa = jnp.exp2(l_old - m)
                wb = jnp.exp2(l_new - m)
                tot = wa + wb
                acc_ref[half, rows, :] = (wa * acc_ref[half, rows, :] + wb * o[:, cols]) * (1.0 / tot)
                lse_ref[half, rows, :] = m + jnp.log2(tot)
            return carry

        lax.fori_loop(0, dil * tiles, body, 0, unroll=BAND_UNROLL)

    acc = jnp.concatenate([acc_ref[0], acc_ref[1]], axis=-1)
    o_ref[...] = _rms(acc, gain_ref[...]).astype(BF16)


def _dilated_call(proj3d, d4, d16, biases, gain):
    b, seq, _ = proj3d.shape
    radii = tuple(window // (2 * dil) for window, dil in DILATED_CFGS)
    kern = functools.partial(_dilated_kernel, radii=radii)
    return pl.pallas_call(
        kern,
        grid=(b,),
        in_specs=[
            pl.BlockSpec((None, seq, GROUP_WIDTH), lambda bi: (bi, 0, SEC_AQ)),
            pl.BlockSpec((None, seq, GROUP_WIDTH), lambda bi: (bi, 0, SEC_AK)),
            pl.BlockSpec((None, seq, GROUP_WIDTH), lambda bi: (bi, 0, SEC_AV)),
            pl.BlockSpec((None,) + d4.shape[1:], lambda bi: (bi, 0, 0, 0)),
            pl.BlockSpec((None,) + d16.shape[1:], lambda bi: (bi, 0, 0, 0)),
            _resident(biases[0].shape), _resident(biases[1].shape), _resident(biases[2].shape),
            _resident((1, GROUP_WIDTH)),
        ],
        out_specs=pl.BlockSpec((None, seq, GROUP_WIDTH), lambda bi: (bi, 0, 0)),
        out_shape=jax.ShapeDtypeStruct((b, seq, GROUP_WIDTH), BF16),
        scratch_shapes=[pltpu.VMEM((GROUP_WIDTH // LANES, seq, LANES), F32)] * 2,
        compiler_params=_params(("parallel",)),
        name="dilated",
    )(proj3d, proj3d, proj3d, d4, d16, biases[0], biases[1], biases[2], gain)


def _swa_kernel(q_ref, k_ref, v_ref, bias_ref, sink_ref, gain_ref, o_ref):
    seq = q_ref.shape[0]
    tq = BAND_TQ
    kw = min(tq + 2 * SWA_RADIUS, seq)

    def body(j, carry):
        q0 = pl.multiple_of(j * tq, tq)
        ks, case = _band_window(j, tq, SWA_RADIUS, seq, kw)
        o, _ = _attend(q_ref[pl.ds(q0, tq), :], k_ref[pl.ds(ks, kw), :], v_ref[pl.ds(ks, kw), :],
                       bias_ref, case, sink_ref)
        o_ref[pl.ds(q0, tq), :] = _rms(o, gain_ref[...]).astype(BF16)
        return carry

    lax.fori_loop(0, seq // tq, body, 0, unroll=SWA_UNROLL)


def _swa_call(proj3d, bias, sink, gain):
    b, seq, _ = proj3d.shape
    return pl.pallas_call(
        _swa_kernel,
        grid=(b,),
        in_specs=[
            pl.BlockSpec((None, seq, GROUP_WIDTH), lambda bi: (bi, 0, SEC_BQ)),
            pl.BlockSpec((None, seq, GROUP_WIDTH), lambda bi: (bi, 0, SEC_BK)),
            pl.BlockSpec((None, seq, GROUP_WIDTH), lambda bi: (bi, 0, SEC_BV)),
            _resident(bias.shape),
            _resident(sink.shape),
            _resident((1, GROUP_WIDTH)),
        ],
        out_specs=pl.BlockSpec((None, seq, GROUP_WIDTH), lambda bi: (bi, 0, 0)),
        out_shape=jax.ShapeDtypeStruct((b, seq, GROUP_WIDTH), BF16),
        compiler_params=_params(("parallel",)),
        name="swa",
    )(proj3d, proj3d, proj3d, bias, sink, gain)


def _dense_kernel(q_ref, k_ref, v_ref, gain_ref, o_ref):
    k = k_ref[...]
    v = v_ref[...]
    tq = DENSE_TQ
    hid = _head_id((tq, GROUP_WIDTH))
    for t in range(q_ref.shape[0] // tq):
        q = q_ref[t * tq:(t + 1) * tq, :]
        qzero = jnp.zeros_like(q)
        acc = jnp.zeros((tq, GROUP_WIDTH), F32)
        for h in range(N_HEADS):
            qh = jnp.where(hid == h, q, qzero)
            s = lax.dot_general(qh, k, (((1,), (1,)), ((), ())), preferred_element_type=F32)
            m = jnp.max(s, axis=-1, keepdims=True)
            p = jnp.exp2(s - m)
            den = jnp.sum(p, axis=-1, keepdims=True)
            oh = jnp.dot(p.astype(BF16), v, preferred_element_type=F32) * (1.0 / den)
            acc = jnp.where(hid == h, oh, acc)
        o_ref[t * tq:(t + 1) * tq, :] = _rms(acc, gain_ref[...]).astype(BF16)


def _dense_call(proj3d, gain):
    b, seq, _ = proj3d.shape
    return pl.pallas_call(
        _dense_kernel,
        grid=(b, seq // DENSE_ROWS),
        in_specs=[
            pl.BlockSpec((None, DENSE_ROWS, GROUP_WIDTH), lambda bi, j: (bi, j, SEC_DQ)),
            pl.BlockSpec((None, seq, GROUP_WIDTH), lambda bi, j: (bi, 0, SEC_DK)),
            pl.BlockSpec((None, seq, GROUP_WIDTH), lambda bi, j: (bi, 0, SEC_DV)),
            _resident((1, GROUP_WIDTH)),
        ],
        out_specs=pl.BlockSpec((None, DENSE_ROWS, GROUP_WIDTH), lambda bi, j: (bi, j, 0)),
        out_shape=jax.ShapeDtypeStruct((b, seq, GROUP_WIDTH), BF16),
        compiler_params=_params(("parallel", "arbitrary")),
        name="dense",
    )(proj3d, proj3d, proj3d, gain)


def _mix_kernel(x_ref, ya_ref, yb_ref, yc_ref, yd_ref, w_ref, g_ref, x1_ref, xn_ref, xs_ref):
    mixed = jnp.concatenate([ya_ref[...], yb_ref[...], yc_ref[...], yd_ref[...]], axis=-1)
    x1 = x_ref[...] + jnp.dot(mixed, w_ref[...], preferred_element_type=F32)
    x1_ref[...] = x1
    xn = _rms(x1, g_ref[...])
    tm = xn.shape[0]
    for c in range(D_MODEL // LANES):
        xs_ref[c] = xn[:, c * LANES:(c + 1) * LANES]
        col = jnp.concatenate([xs_ref[c, pl.ds(b, tm // STRANDS, stride=STRANDS), :] for b in range(STRANDS)],
                              axis=0)
        xn_ref[:, c * LANES:(c + 1) * LANES] = col.astype(BF16)


def _mix_call(x2d, ya, yb, proj2d, yd, w_out, g_ffn):
    n = x2d.shape[0]
    row = lambda i: (i, 0)
    grp = pl.BlockSpec((MIX_TM, GROUP_WIDTH), row)
    return pl.pallas_call(
        _mix_kernel,
        grid=(n // MIX_TM,),
        in_specs=[pl.BlockSpec((MIX_TM, D_MODEL), row), grp, grp,
                  pl.BlockSpec((MIX_TM, GROUP_WIDTH), lambda i: (i, SEC_YC)), grp,
                  _resident(w_out.shape), _resident((1, D_MODEL))],
        out_specs=[pl.BlockSpec((MIX_TM, D_MODEL), row), pl.BlockSpec((MIX_TM, D_MODEL), row)],
        out_shape=[jax.ShapeDtypeStruct((n, D_MODEL), F32), jax.ShapeDtypeStruct((n, D_MODEL), BF16)],
        scratch_shapes=[pltpu.VMEM((D_MODEL // LANES, MIX_TM, LANES), F32)],
        compiler_params=_params(("parallel",)),
        name="mix_out",
    )(x2d, ya, yb, proj2d, yd, w_out, g_ffn)


def _ffn_kernel(xn_ref, prev_ref, next_ref, x1_ref, p_ref, wup_ref, cw_ref, wd_ref, gple_ref, wgate_ref,
                wproj_ref, o_ref, xcat_ref, h0_ref, h1_ref, acc_ref, tok_ref, *, tiles_per_seq):
    pos = pl.program_id(0) % tiles_per_seq
    tm = xn_ref.shape[0]
    rows = tm + HALO
    sl = tm // STRANDS
    pitch = sl + SUB

    prev_blk = jnp.where(pos == 0, jnp.zeros_like(prev_ref[...]), prev_ref[...])
    next_blk = jnp.where(pos == tiles_per_seq - 1, jnp.zeros_like(next_ref[...]), next_ref[...])
    halo_row = lax.broadcasted_iota(jnp.int32, (HALO, D_MODEL), 0)
    xcat_ref[0:tm, :] = xn_ref[...]
    xcat_ref[tm:rows, :] = jnp.where(halo_row == 0, next_blk, prev_blk)
    acc_ref[...] = jnp.zeros_like(acc_ref)

    def up(k, h_ref):
        xcat = xcat_ref[...]
        for part in range(2):
            h = jnp.dot(xcat, wup_ref[k + part * N_CHUNKS], preferred_element_type=F32)
            for b in range(STRANDS):
                h_ref[part, SUB + b * pitch:SUB + b * pitch + sl, :] = h[b * sl:(b + 1) * sl]
            h_ref[part, SUB + sl:SUB + pitch, :] = h[tm:tm + SUB]
            h_ref[part, (STRANDS - 1) * pitch:SUB + (STRANDS - 1) * pitch, :] = h[rows - SUB:rows]

    def strand(h_ref, part, b, shift=0):
        start = SUB + b * pitch + shift
        return h_ref[part, start:start + sl, :]

    def conv(h_ref, part, cw):
        out = []
        for b in range(STRANDS):
            before = strand(h_ref, part, b - 1) if b > 0 else strand(h_ref, part, STRANDS - 1, -1)
            after = strand(h_ref, part, b + 1) if b < STRANDS - 1 else strand(h_ref, part, 0, 1)
            out.append(before * cw[0:1, :] + strand(h_ref, part, b) * cw[1:2, :] + after * cw[2:3, :] + cw[3:4, :])
        return jnp.concatenate(out, axis=0)

    def down(k, h_ref):
        gt = conv(h_ref, 0, cw_ref[k])
        uu = conv(h_ref, 1, cw_ref[k + N_CHUNKS])
        act = (gt / (1.0 + jnp.exp(-gt)) * uu).astype(BF16)
        acc_ref[...] += jnp.dot(act, wd_ref[k], preferred_element_type=F32)

    up(0, h0_ref)

    def pair(t, carry):
        k = 2 * t
        up(k + 1, h1_ref)
        down(k, h0_ref)
        up(k + 2, h0_ref)
        down(k + 1, h1_ref)
        return carry

    lax.fori_loop(0, (N_CHUNKS - 1) // 2, pair, 0)
    down(N_CHUNKS - 1, h0_ref)

    for c in range(D_MODEL // LANES):
        for b in range(STRANDS):
            tok_ref[c, pl.ds(b, sl, stride=STRANDS), :] = acc_ref[b * sl:(b + 1) * sl, c * LANES:(c + 1) * LANES]
    ffn = jnp.concatenate([tok_ref[c] for c in range(D_MODEL // LANES)], axis=-1)
    x2 = x1_ref[...] + ffn
    z = jnp.dot(_rms(x2, gple_ref[...]).astype(BF16), wgate_ref[...], preferred_element_type=F32)
    gate = 1.0 / (1.0 + jnp.exp(-z))
    inj = jnp.dot(p_ref[...].astype(BF16), wproj_ref[...], preferred_element_type=F32)
    o_ref[...] = x2 + inj * gate


def _ffn_call(xn, x1, p2d, wup, cw, wd, g_ple, w_gate, w_proj, seq):
    assert N_CHUNKS % 2 == 1
    assert MIX_TM == FFN_TM
    n = xn.shape[0]
    halo_per_tile = FFN_TM // HALO
    n_halo = n // HALO
    row = lambda i: (i, 0)
    kern = functools.partial(_ffn_kernel, tiles_per_seq=seq // FFN_TM)
    return pl.pallas_call(
        kern,
        grid=(n // FFN_TM,),
        in_specs=[
            pl.BlockSpec((FFN_TM, D_MODEL), row),
            pl.BlockSpec((HALO, D_MODEL), lambda i: (jnp.maximum(i * halo_per_tile - 1, 0), 0)),
            pl.BlockSpec((HALO, D_MODEL), lambda i: (jnp.minimum((i + 1) * halo_per_tile, n_halo - 1), 0)),
            pl.BlockSpec((FFN_TM, D_MODEL), row),
            pl.BlockSpec((FFN_TM, PLE_DIM), row),
            _resident(wup.shape), _resident(cw.shape), _resident(wd.shape),
            _resident((1, D_MODEL)), _resident(w_gate.shape), _resident(w_proj.shape),
        ],
        out_specs=pl.BlockSpec((FFN_TM, D_MODEL), row),
        out_shape=jax.ShapeDtypeStruct((n, D_MODEL), F32),
        scratch_shapes=[pltpu.VMEM((FFN_TM + HALO, D_MODEL), BF16),
                        pltpu.VMEM((2, SUB + STRANDS * (FFN_TM // STRANDS + SUB), FFN_TF), F32),
                        pltpu.VMEM((2, SUB + STRANDS * (FFN_TM // STRANDS + SUB), FFN_TF), F32),
                        pltpu.VMEM((FFN_TM, D_MODEL), F32),
                        pltpu.VMEM((D_MODEL // LANES, FFN_TM, LANES), F32)],
        compiler_params=_params(("parallel",)),
        name="ffn_ple",
    )(xn, xn, xn, x1, p2d, wup, cw, wd, g_ple, w_gate, w_proj)


def _t5_bucket(rel):
    nb = REL_BUCKETS // 2
    ret = jnp.where(rel > 0, nb, 0)
    n = jnp.abs(rel)
    max_exact = nb // 2
    nf = jnp.maximum(n, 1).astype(F32)
    large = max_exact + (jnp.log(nf / max_exact) / math.log(REL_MAX_DIST / max_exact)
                         * (nb - max_exact)).astype(jnp.int32)
    large = jnp.minimum(large, nb - 1)
    return ret + jnp.where(n < max_exact, n, large)


def _band_bias(table, radius, dil, tq, kw):
    cases = []
    for off in (0, radius, kw - tq):
        rel = jnp.arange(kw)[None, :] - jnp.arange(tq)[:, None] - off
        onehot = (_t5_bucket(rel * dil)[..., None] == jnp.arange(REL_BUCKETS)).astype(F32)
        bias = jnp.einsum('qkn,nh->hqk', onehot, table.astype(F32), precision=lax.Precision.HIGHEST)
        bias = jnp.where((jnp.abs(rel) <= radius)[None], bias, NEG_INF)
        cases.append(bias.reshape(N_HEADS * tq, kw))
    return jnp.stack(cases, axis=0)


def _rope_tables(seq):
    half = HEAD_DIM // 2
    inv = ROPE_THETA ** (-jnp.arange(0, half, 2, dtype=F32) / half)
    t = jnp.arange(seq)
    ang_r = (t // GRID_W).astype(F32)[:, None] * inv[None, :]
    ang_c = (t % GRID_W).astype(F32)[:, None] * inv[None, :]
    cos_h = jnp.concatenate([jnp.cos(ang_r)] * 2 + [jnp.cos(ang_c)] * 2, axis=-1)
    sin_h = jnp.concatenate([-jnp.sin(ang_r), jnp.sin(ang_r), -jnp.sin(ang_c), jnp.sin(ang_c)], axis=-1)
    return jnp.tile(cos_h, (1, N_HEADS)), jnp.tile(sin_h, (1, N_HEADS))


def kernel(x, p, rel_bias, ln_mix_g, w_in, qk_gain, sink, c_norm_g, c_norm_b, c_ws, c_bs, out_gain, w_out,
           ln_ffn_g, w_up, conv_w, conv_b, w_down, ln_ple_g, w_ple_gate, w_ple_proj):
    b, s, _ = x.shape
    n = b * s
    depth = w_in.shape[0]
    cos_t, sin_t = _rope_tables(s)
    bd = (np.arange(GROUP_WIDTH)[:, None] // HEAD_DIM == np.arange(GROUP_WIDTH)[None, :] // HEAD_DIM)
    bd = jnp.asarray(bd / HEAD_DIM, BF16)
    table_a = rel_bias[:, :N_HEADS] * LOG2E
    table_b = rel_bias[:, N_HEADS:] * LOG2E
    q_scale = ATTN_SCALE * LOG2E
    bias_a = []
    for window, dil in DILATED_CFGS:
        radius = window // (2 * dil)
        length = s // dil
        tq = min(BAND_TQ, length)
        bias_a.append(_band_bias(table_a, radius, dil, tq, min(tq + 2 * radius, length)))
    tq_b = min(BAND_TQ, s)
    bias_b = _band_bias(table_b, SWA_RADIUS, 1, tq_b, min(tq_b + 2 * SWA_RADIUS, s))

    x2d = x.reshape(n, D_MODEL)
    for i in range(depth):
        w = w_in[i].astype(BF16)
        qg = qk_gain[i]
        vec = jnp.stack([
            jnp.tile(qg[0, 0], N_HEADS) * q_scale, jnp.tile(qg[0, 1], N_HEADS),
            jnp.tile(qg[1, 0], N_HEADS) * q_scale, jnp.tile(qg[1, 1], N_HEADS),
            jnp.tile(qg[2, 0], N_HEADS) * q_scale, jnp.tile(qg[2, 1], N_HEADS),
            c_norm_g[i], c_norm_b[i], out_gain[i, 2],
        ] + [jnp.zeros((GROUP_WIDTH,), F32)] * 7, axis=0)
        wcat = jnp.concatenate([c_ws[i, g] for g in range(C_GROUPS)], axis=1).astype(BF16)
        bsx = jnp.repeat(jnp.transpose(c_bs[i]), GROUP_WIDTH // C_GROUPS, axis=1)
        proj2d, d4, d16 = _proj_call(x2d, ln_mix_g[i][None], w, vec, cos_t, sin_t, bd, wcat, bsx, b, s)

        proj3d = proj2d.reshape(b, s, N_SEC * GROUP_WIDTH)
        ya = _dilated_call(proj3d, d4, d16, bias_a, out_gain[i, 0][None])
        sink_blk = jnp.full((N_HEADS * tq_b, LANES), NEG_INF, F32).at[:, 0].set(jnp.repeat(sink[i] * LOG2E, tq_b))
        yb = _swa_call(proj3d, bias_b, sink_blk, out_gain[i, 1][None])
        yd = _dense_call(proj3d, out_gain[i, 3][None])

        x1, xn = _mix_call(x2d, ya.reshape(n, GROUP_WIDTH), yb.reshape(n, GROUP_WIDTH), proj2d,
                           yd.reshape(n, GROUP_WIDTH), w_out[i].astype(BF16), ln_ffn_g[i][None])
        wup = jnp.transpose(w_up[i].astype(BF16).reshape(D_MODEL, 2 * N_CHUNKS, FFN_TF), (1, 0, 2))
        cw = jnp.concatenate([conv_w[i], conv_b[i][None]], axis=0)
        cw = jnp.transpose(cw.reshape(4, 2 * N_CHUNKS, FFN_TF), (1, 0, 2))
        wd = w_down[i].astype(BF16).reshape(N_CHUNKS, FFN_TF, D_MODEL)
        x2d = _ffn_call(xn, x1, p[i].reshape(n, PLE_DIM), wup, cw, wd, ln_ple_g[i][None],
                        w_ple_gate[i].astype(BF16), w_ple_proj[i].astype(BF16), s)
    return x2d.reshape(b, s, D_MODEL)
```

```python
import functools
import math

import jax
import jax.numpy as jnp
import numpy as np
from jax import lax
from jax.experimental import pallas as pl
from jax.experimental.pallas import tpu as pltpu

D_MODEL = 1024
HEAD_DIM = 64
GROUP_WIDTH = 256
N_HEADS = GROUP_WIDTH // HEAD_DIM
C_GROUPS = 4
C_CHUNK = 128
DILATED_CFGS = ((128, 1), (512, 4), (2048, 16))
SWA_RADIUS = 128
GRID_W = 64
ROPE_THETA = 10000.0
REL_BUCKETS = 32
REL_MAX_DIST = 1024
D_FF = 2816
PLE_DIM = 256
EPS = 1e-6
NEG_INF = -1e30
ATTN_SCALE = HEAD_DIM ** -0.5
LOG2E = math.log2(math.e)

BF16 = jnp.bfloat16
F32 = jnp.float32

SEC_AQ, SEC_AK, SEC_AV, SEC_BQ, SEC_BK, SEC_BV, SEC_YC, SEC_DQ, SEC_DK, SEC_DV = range(10)
N_SEC = 10
A_WIDTH = 3 * GROUP_WIDTH

VMEM_LIMIT = 56 * 1024 * 1024

KIN_TM = 512
BAND_TQ = 128
BAND_UNROLL = 8
SWA_UNROLL = 2
DENSE_TQ = 512
DENSE_ROWS = 1024
MIX_TM = 512
FFN_TM = 512
FFN_TF = 256
LANES = 128
SUB = 8
STRANDS = 8
HALO = 16
N_CHUNKS = D_FF // FFN_TF


def _params(sem):
    return pltpu.CompilerParams(dimension_semantics=sem, vmem_limit_bytes=VMEM_LIMIT)


def _resident(shape):
    zeros = (0,) * len(shape)
    return pl.BlockSpec(shape, lambda *_: zeros, pipeline_mode=pl.Buffered(1))


def _rms(t, gain):
    return t * lax.rsqrt(jnp.mean(t * t, axis=-1, keepdims=True) + EPS) * gain


def _head_id(shape):
    return lax.broadcasted_iota(jnp.int32, shape, 1) >> 6


def _proj_kernel(x_ref, g_ref, w_ref, vec_ref, cos_ref, sin_ref, bd_ref, wcat_ref, bsx_ref,
                 o_ref, d4_ref, d16_ref, a_ref):
    x = x_ref[...]
    tm = x.shape[0]
    hn = _rms(x, g_ref[...]).astype(BF16)
    bd = bd_ref[...]

    def proj(sec):
        return jnp.dot(hn, w_ref[:, sec * GROUP_WIDTH:(sec + 1) * GROUP_WIDTH],
                       preferred_element_type=F32)

    def store(sec, val):
        o_ref[:, sec * GROUP_WIDTH:(sec + 1) * GROUP_WIDTH] = val.astype(BF16)

    def store_a(sec, val):
        store(sec, val)
        for half in range(GROUP_WIDTH // LANES):
            slot = sec * (GROUP_WIDTH // LANES) + half
            a_ref[slot] = val[:, half * LANES:(half + 1) * LANES]
            cols = slice(slot * LANES, (slot + 1) * LANES)
            for dil, ref in ((4, d4_ref), (16, d16_ref)):
                for r in range(dil):
                    ref[r, :, cols] = a_ref[slot, pl.ds(r, tm // dil, stride=dil), :].astype(BF16)

    def head_rms(t, row):
        width = t.shape[1]
        sq = t * t
        hi = sq.astype(BF16)
        lo = (sq - hi.astype(F32)).astype(BF16)
        blk = bd[0:width, 0:width]
        ms = (jnp.dot(hi, blk, preferred_element_type=F32) + jnp.dot(lo, blk, preferred_element_type=F32))
        return t * lax.rsqrt(ms + EPS) * vec_ref[row:row + 1, 0:width]

    def rope(t):
        blocks = []
        for c0 in range(0, t.shape[1], LANES):
            th = t[:, c0:c0 + LANES]
            lane = lax.broadcasted_iota(jnp.int32, th.shape, 1)
            partner = jnp.where((lane & 31) < 16, pltpu.roll(th, LANES - 16, axis=1), pltpu.roll(th, 16, axis=1))
            blocks.append(th * cos_ref[:, c0:c0 + LANES] + partner * sin_ref[:, c0:c0 + LANES])
        return blocks[0] if len(blocks) == 1 else jnp.concatenate(blocks, axis=-1)

    def rep_kv(t):
        swapped = pltpu.roll(t, HEAD_DIM, axis=1)
        low = lax.broadcasted_iota(jnp.int32, t.shape, 1) < HEAD_DIM
        return jnp.concatenate([jnp.where(low, t, swapped), jnp.where(low, swapped, t)], axis=-1)

    store_a(SEC_AQ, head_rms(proj(0), 0))
    store_a(SEC_AK, head_rms(proj(1), 1))
    store_a(SEC_AV, proj(2))
    store(SEC_BQ, head_rms(proj(3), 2))
    b_kv = proj(4)
    store(SEC_BK, rep_kv(head_rms(b_kv[:, 0:LANES], 3)))
    store(SEC_BV, rep_kv(b_kv[:, LANES:]))
    store(SEC_DQ, rope(head_rms(proj(7), 4)))
    d_kv = proj(8)
    store(SEC_DK, rep_kv(rope(head_rms(d_kv[:, 0:LANES], 5))))
    store(SEC_DV, jnp.concatenate([d_kv[:, LANES:]] * 2, axis=-1))

    u = jax.nn.gelu(proj(5))
    v = jax.nn.gelu(proj(6))
    mu = jnp.mean(v, axis=-1, keepdims=True)
    vc = v - mu
    var = jnp.mean(vc * vc, axis=-1, keepdims=True)
    vn = (vc * lax.rsqrt(var + EPS) * vec_ref[6:7, :] + vec_ref[7:8, :]).astype(BF16)
    gid = _head_id((C_CHUNK, GROUP_WIDTH))
    zero = jnp.zeros((C_CHUNK, GROUP_WIDTH), BF16)
    for c in range(tm // C_CHUNK):
        rows = slice(c * C_CHUNK, (c + 1) * C_CHUNK)
        vchunk = vn[rows]
        stacked = jnp.concatenate([jnp.where(gid == g, vchunk, zero) for g in range(C_GROUPS)], axis=0)
        mixed = jnp.dot(wcat_ref[...], stacked, preferred_element_type=F32) + bsx_ref[...]
        yc = _rms(u[rows] * mixed, vec_ref[8:9, :])
        o_ref[rows, SEC_YC * GROUP_WIDTH:(SEC_YC + 1) * GROUP_WIDTH] = yc.astype(BF16)


def _proj_call(x2d, g, w, vec, cos_t, sin_t, bd, wcat, bsx, batch, seq):
    n = x2d.shape[0]
    tps = seq // KIN_TM
    return pl.pallas_call(
        _proj_kernel,
        grid=(n // KIN_TM,),
        in_specs=[
            pl.BlockSpec((KIN_TM, D_MODEL), lambda i: (i, 0)),
            _resident((1, D_MODEL)),
            _resident(w.shape),
            _resident(vec.shape),
            pl.BlockSpec((KIN_TM, GROUP_WIDTH), lambda i: (i % tps, 0)),
            pl.BlockSpec((KIN_TM, GROUP_WIDTH), lambda i: (i % tps, 0)),
            _resident(bd.shape),
            _resident(wcat.shape),
            _resident(bsx.shape),
        ],
        out_specs=[
            pl.BlockSpec((KIN_TM, N_SEC * GROUP_WIDTH), lambda i: (i, 0)),
            pl.BlockSpec((None, 4, KIN_TM // 4, A_WIDTH), lambda i: (i // tps, 0, i % tps, 0)),
            pl.BlockSpec((None, 16, KIN_TM // 16, A_WIDTH), lambda i: (i // tps, 0, i % tps, 0)),
        ],
        out_shape=[
            jax.ShapeDtypeStruct((n, N_SEC * GROUP_WIDTH), BF16),
            jax.ShapeDtypeStruct((batch, 4, seq // 4, A_WIDTH), BF16),
            jax.ShapeDtypeStruct((batch, 16, seq // 16, A_WIDTH), BF16),
        ],
        scratch_shapes=[pltpu.VMEM((A_WIDTH // LANES, KIN_TM, LANES), F32)],
        compiler_params=_params(("parallel",)),
        name="proj",
    )(x2d, g, w, vec, cos_t, sin_t, bd, wcat, bsx)


def _attend(q, kwin, vwin, bias_ref, case, sink_ref=None):
    tq = q.shape[0]
    hid = _head_id((tq, GROUP_WIDTH))
    qzero = jnp.zeros_like(q)
    q4 = jnp.concatenate([jnp.where(hid == h, q, qzero) for h in range(N_HEADS)], axis=0)
    s = lax.dot_general(q4, kwin, (((1,), (1,)), ((), ())), preferred_element_type=F32)
    s = s + bias_ref[case]
    if sink_ref is None:
        m = jnp.max(s, axis=-1, keepdims=True)
        p = jnp.exp2(s - m)
        den = jnp.sum(p, axis=-1, keepdims=True)
    else:
        sink = sink_ref[...]
        blocks = [s[:, c:c + LANES] for c in range(0, s.shape[1], LANES)]
        m = jnp.max(functools.reduce(jnp.maximum, blocks + [sink]), axis=-1, keepdims=True)
        p = jnp.exp2(s - m)
        pblocks = [p[:, c:c + LANES] for c in range(0, s.shape[1], LANES)]
        den = jnp.sum(functools.reduce(jnp.add, pblocks + [jnp.exp2(sink - m)]), axis=-1, keepdims=True)
    o4 = jnp.dot(p.astype(BF16), vwin, preferred_element_type=F32) * (1.0 / den)
    l4 = jnp.broadcast_to(m + jnp.log2(den), o4.shape)
    out = o4[0:tq]
    lse = l4[0:tq]
    for h in range(1, N_HEADS):
        out = jnp.where(hid == h, o4[h * tq:(h + 1) * tq], out)
        lse = jnp.where(hid == h, l4[h * tq:(h + 1) * tq], lse)
    return out, lse


def _band_window(j, tq, radius, length, kw):
    n_tiles = length // tq
    if n_tiles == 1:
        return 0, 0
    ks = pl.multiple_of(jnp.clip(j * tq - radius, 0, length - kw), 64)
    case = jnp.where(j == 0, 0, jnp.where(j == n_tiles - 1, 2, 1))
    return ks, case


def _dilated_kernel(q_ref, k_ref, v_ref, d4_ref, d16_ref, b1_ref, b4_ref, b16_ref, gain_ref, o_ref,
                    acc_ref, lse_ref, *, radii):
    seq = q_ref.shape[0]
    tq = BAND_TQ

    kw1 = min(tq + 2 * radii[0], seq)

    def body1(j, carry):
        q0 = pl.multiple_of(j * tq, tq)
        ks, case = _band_window(j, tq, radii[0], seq, kw1)
        o, l = _attend(q_ref[pl.ds(q0, tq), :], k_ref[pl.ds(ks, kw1), :], v_ref[pl.ds(ks, kw1), :], b1_ref, case)
        for half in range(GROUP_WIDTH // LANES):
            cols = slice(half * LANES, (half + 1) * LANES)
            acc_ref[half, pl.ds(q0, tq), :] = o[:, cols]
            lse_ref[half, pl.ds(q0, tq), :] = l[:, cols]
        return carry

    lax.fori_loop(0, seq // tq, body1, 0, unroll=BAND_UNROLL)

    for dil, ref, bias_ref, radius in ((4, d4_ref, b4_ref, radii[1]), (16, d16_ref, b16_ref, radii[2])):
        length = seq // dil
        tiles = length // tq
        kw = min(tq + 2 * radius, length)

        def body(it, carry, dil=dil, ref=ref, bias_ref=bias_ref, radius=radius, length=length, tiles=tiles, kw=kw):
            r = it // tiles
            j = it % tiles
            q0 = pl.multiple_of(j * tq, tq)
            ks, case = _band_window(j, tq, radius, length, kw)
            o, l = _attend(ref[r, pl.ds(q0, tq), 0:GROUP_WIDTH],
                           ref[r, pl.ds(ks, kw), GROUP_WIDTH:2 * GROUP_WIDTH],
                           ref[r, pl.ds(ks, kw), 2 * GROUP_WIDTH:3 * GROUP_WIDTH], bias_ref, case)
            rows = pl.ds(r + dil * q0, tq, stride=dil)
            for half in range(GROUP_WIDTH // LANES):
                cols = slice(half * LANES, (half + 1) * LANES)
                l_new = l[:, cols]
                l_old = lse_ref[half, rows, :]
                m = jnp.maximum(l_old, l_new)
                wa = jnp.exp2(l_old - m)
                wb = jnp.exp2(l_new - m)
                tot = wa + wb
                acc_ref[half, rows, :] = (wa * acc_ref[half, rows, :] + wb * o[:, cols]) * (1.0 / tot)
                lse_ref[half, rows, :] = m + jnp.log2(tot)
            return carry

        lax.fori_loop(0, dil * tiles, body, 0, unroll=BAND_UNROLL)

    acc = jnp.concatenate([acc_ref[0], acc_ref[1]], axis=-1)
    o_ref[...] = _rms(acc, gain_ref[...]).astype(BF16)


def _dilated_call(proj3d, d4, d16, biases, gain):
    b, seq, _ = proj3d.shape
    radii = tuple(window // (2 * dil) for window, dil in DILATED_CFGS)
    kern = functools.partial(_dilated_kernel, radii=radii)
    return pl.pallas_call(
        kern,
        grid=(b,),
        in_specs=[
            pl.BlockSpec((None, seq, GROUP_WIDTH), lambda bi: (bi, 0, SEC_AQ)),
            pl.BlockSpec((None, seq, GROUP_WIDTH), lambda bi: (bi, 0, SEC_AK)),
            pl.BlockSpec((None, seq, GROUP_WIDTH), lambda bi: (bi, 0, SEC_AV)),
            pl.BlockSpec((None,) + d4.shape[1:], lambda bi: (bi, 0, 0, 0)),
            pl.BlockSpec((None,) + d16.shape[1:], lambda bi: (bi, 0, 0, 0)),
            _resident(biases[0].shape), _resident(biases[1].shape), _resident(biases[2].shape),
            _resident((1, GROUP_WIDTH)),
        ],
        out_specs=pl.BlockSpec((None, seq, GROUP_WIDTH), lambda bi: (bi, 0, 0)),
        out_shape=jax.ShapeDtypeStruct((b, seq, GROUP_WIDTH), BF16),
        scratch_shapes=[pltpu.VMEM((GROUP_WIDTH // LANES, seq, LANES), F32)] * 2,
        compiler_params=_params(("parallel",)),
        name="dilated",
    )(proj3d, proj3d, proj3d, d4, d16, biases[0], biases[1], biases[2], gain)


def _swa_kernel(q_ref, k_ref, v_ref, bias_ref, sink_ref, gain_ref, o_ref):
    seq = q_ref.shape[0]
    tq = BAND_TQ
    kw = min(tq + 2 * SWA_RADIUS, seq)

    def body(j, carry):
        q0 = pl.multiple_of(j * tq, tq)
        ks, case = _band_window(j, tq, SWA_RADIUS, seq, kw)
        o, _ = _attend(q_ref[pl.ds(q0, tq), :], k_ref[pl.ds(ks, kw), :], v_ref[pl.ds(ks, kw), :],
                       bias_ref, case, sink_ref)
        o_ref[pl.ds(q0, tq), :] = _rms(o, gain_ref[...]).astype(BF16)
        return carry

    lax.fori_loop(0, seq // tq, body, 0, unroll=SWA_UNROLL)


def _swa_call(proj3d, bias, sink, gain):
    b, seq, _ = proj3d.shape
    return pl.pallas_call(
        _swa_kernel,
        grid=(b,),
        in_specs=[
            pl.BlockSpec((None, seq, GROUP_WIDTH), lambda bi: (bi, 0, SEC_BQ)),
            pl.BlockSpec((None, seq, GROUP_WIDTH), lambda bi: (bi, 0, SEC_BK)),
            pl.BlockSpec((None, seq, GROUP_WIDTH), lambda bi: (bi, 0, SEC_BV)),
            _resident(bias.shape),
            _resident(sink.shape),
            _resident((1, GROUP_WIDTH)),
        ],
        out_specs=pl.BlockSpec((None, seq, GROUP_WIDTH), lambda bi: (bi, 0, 0)),
        out_shape=jax.ShapeDtypeStruct((b, seq, GROUP_WIDTH), BF16),
        compiler_params=_params(("parallel",)),
        name="swa",
    )(proj3d, proj3d, proj3d, bias, sink, gain)


def _dense_kernel(q_ref, k_ref, v_ref, gain_ref, o_ref):
    k = k_ref[...]
    vext = jnp.concatenate([v_ref[...], jnp.ones((k.shape[0], LANES), BF16)], axis=-1)
    tq = DENSE_TQ
    hid = _head_id((tq, GROUP_WIDTH))
    low = lax.broadcasted_iota(jnp.int32, (tq, LANES), 1) < HEAD_DIM
    for t in range(q_ref.shape[0] // tq):
        q = q_ref[t * tq:(t + 1) * tq, :]
        qzero = jnp.zeros_like(q)
        heads = []
        for h in range(N_HEADS):
            qh = jnp.where(hid == h, q, qzero)
            s = lax.dot_general(qh, k, (((1,), (1,)), ((), ())), preferred_element_type=F32)
            m = jnp.max(s, axis=-1, keepdims=True)
            p = jnp.exp2((s - m).astype(BF16))
            o = jnp.dot(p, vext, preferred_element_type=F32)
            heads.append(o[:, 0:LANES] * (1.0 / o[:, LANES:]))
        blk0 = jnp.where(low, heads[0], pltpu.roll(heads[1], HEAD_DIM, axis=1))
        blk1 = jnp.where(low, pltpu.roll(heads[2], HEAD_DIM, axis=1), heads[3])
        acc = jnp.concatenate([blk0, blk1], axis=-1)
        o_ref[t * tq:(t + 1) * tq, :] = _rms(acc, gain_ref[...]).astype(BF16)


def _dense_call(proj3d, gain):
    b, seq, _ = proj3d.shape
    return pl.pallas_call(
        _dense_kernel,
        grid=(b, seq // DENSE_ROWS),
        in_specs=[
            pl.BlockSpec((None, DENSE_ROWS, GROUP_WIDTH), lambda bi, j: (bi, j, SEC_DQ)),
            pl.BlockSpec((None, seq, GROUP_WIDTH), lambda bi, j: (bi, 0, SEC_DK)),
            pl.BlockSpec((None, seq, LANES), lambda bi, j: (bi, 0, SEC_DV * (GROUP_WIDTH // LANES))),
            _resident((1, GROUP_WIDTH)),
        ],
        out_specs=pl.BlockSpec((None, DENSE_ROWS, GROUP_WIDTH), lambda bi, j: (bi, j, 0)),
        out_shape=jax.ShapeDtypeStruct((b, seq, GROUP_WIDTH), BF16),
        compiler_params=_params(("parallel", "arbitrary")),
        name="dense",
    )(proj3d, proj3d, proj3d, gain)


def _mix_kernel(x_ref, ya_ref, yb_ref, yc_ref, yd_ref, w_ref, g_ref, x1_ref, xn_ref, xs_ref):
    mixed = jnp.concatenate([ya_ref[...], yb_ref[...], yc_ref[...], yd_ref[...]], axis=-1)
    x1 = x_ref[...] + jnp.dot(mixed, w_ref[...], preferred_element_type=F32)
    x1_ref[...] = x1
    xn = _rms(x1, g_ref[...])
    tm = xn.shape[0]
    for c in range(D_MODEL // LANES):
        xs_ref[c] = xn[:, c * LANES:(c + 1) * LANES]
        col = jnp.concatenate([xs_ref[c, pl.ds(b, tm // STRANDS, stride=STRANDS), :] for b in range(STRANDS)],
                              axis=0)
        xn_ref[:, c * LANES:(c + 1) * LANES] = col.astype(BF16)


def _mix_call(x2d, ya, yb, proj2d, yd, w_out, g_ffn):
    n = x2d.shape[0]
    row = lambda i: (i, 0)
    grp = pl.BlockSpec((MIX_TM, GROUP_WIDTH), row)
    return pl.pallas_call(
        _mix_kernel,
        grid=(n // MIX_TM,),
        in_specs=[pl.BlockSpec((MIX_TM, D_MODEL), row), grp, grp,
                  pl.BlockSpec((MIX_TM, GROUP_WIDTH), lambda i: (i, SEC_YC)), grp,
                  _resident(w_out.shape), _resident((1, D_MODEL))],
        out_specs=[pl.BlockSpec((MIX_TM, D_MODEL), row), pl.BlockSpec((MIX_TM, D_MODEL), row)],
        out_shape=[jax.ShapeDtypeStruct((n, D_MODEL), F32), jax.ShapeDtypeStruct((n, D_MODEL), BF16)],
        scratch_shapes=[pltpu.VMEM((D_MODEL // LANES, MIX_TM, LANES), F32)],
        compiler_params=_params(("parallel",)),
        name="mix_out",
    )(x2d, ya, yb, proj2d, yd, w_out, g_ffn)


def _ffn_kernel(xn_ref, prev_ref, next_ref, x1_ref, p_ref, wup_ref, cw_ref, wd_ref, gple_ref, wgate_ref,
                wproj_ref, o_ref, xcat_ref, h0_ref, h1_ref, acc_ref, tok_ref, *, tiles_per_seq, n_tiles):
    step = pl.program_id(0)
    tm = xn_ref.shape[0]
    rows = tm + HALO
    sl = tm // STRANDS
    pitch = sl + SUB

    def up(k, h_ref):
        xcat = xcat_ref[...]
        for part in range(2):
            col = (k + part * N_CHUNKS) * FFN_TF
            if not isinstance(col, int):
                col = pl.multiple_of(col, FFN_TF)
            h = jnp.dot(xcat, wup_ref[:, pl.ds(col, FFN_TF)], preferred_element_type=F32)
            for b in range(STRANDS):
                h_ref[part, SUB + b * pitch:SUB + b * pitch + sl, :] = h[b * sl:(b + 1) * sl]
            h_ref[part, SUB + sl:SUB + pitch, :] = h[tm:tm + SUB]
            h_ref[part, (STRANDS - 1) * pitch:SUB + (STRANDS - 1) * pitch, :] = h[rows - SUB:rows]

    def strand(h_ref, part, b, shift=0):
        start = SUB + b * pitch + shift
        return h_ref[part, start:start + sl, :]

    def conv(h_ref, part, cw):
        out = []
        for b in range(STRANDS):
            before = strand(h_ref, part, b - 1) if b > 0 else strand(h_ref, part, STRANDS - 1, -1)
            after = strand(h_ref, part, b + 1) if b < STRANDS - 1 else strand(h_ref, part, 0, 1)
            out.append(before * cw[0:1, :] + strand(h_ref, part, b) * cw[1:2, :] + after * cw[2:3, :] + cw[3:4, :])
        return jnp.concatenate(out, axis=0)

    def down(k, h_ref):
        gt = conv(h_ref, 0, cw_ref[k])
        uu = conv(h_ref, 1, cw_ref[k + N_CHUNKS])
        act = (gt / (1.0 + jnp.exp(-gt)) * uu).astype(BF16)
        acc_ref[...] += jnp.dot(act, wd_ref[k], preferred_element_type=F32)

    def stage():
        pos = jnp.minimum(step, n_tiles - 1) % tiles_per_seq
        prev_blk = jnp.where(pos == 0, jnp.zeros_like(prev_ref[...]), prev_ref[...])
        next_blk = jnp.where(pos == tiles_per_seq - 1, jnp.zeros_like(next_ref[...]), next_ref[...])
        halo_row = lax.broadcasted_iota(jnp.int32, (HALO, D_MODEL), 0)
        xcat_ref[0:tm, :] = xn_ref[...]
        xcat_ref[tm:rows, :] = jnp.where(halo_row == 0, next_blk, prev_blk)
        up(0, h0_ref)

    @pl.when(step == 0)
    def _():
        stage()

    @pl.when(step > 0)
    def _():
        acc_ref[...] = jnp.zeros_like(acc_ref)

        def pair(t, carry):
            k = 2 * t
            up(k + 1, h1_ref)
            down(k, h0_ref)
            up(k + 2, h0_ref)
            down(k + 1, h1_ref)
            return carry

        lax.fori_loop(0, (N_CHUNKS - 1) // 2, pair, 0)
        down(N_CHUNKS - 1, h0_ref)

        stage()

        for c in range(D_MODEL // LANES):
            for b in range(STRANDS):
                tok_ref[c, pl.ds(b, sl, stride=STRANDS), :] = acc_ref[b * sl:(b + 1) * sl, c * LANES:(c + 1) * LANES]
        ffn = jnp.concatenate([tok_ref[c] for c in range(D_MODEL // LANES)], axis=-1)
        x2 = x1_ref[...] + ffn
        z = jnp.dot(_rms(x2, gple_ref[...]).astype(BF16), wgate_ref[...], preferred_element_type=F32)
        gate = 1.0 / (1.0 + jnp.exp(-z))
        inj = jnp.dot(p_ref[...].astype(BF16), wproj_ref[...], preferred_element_type=F32)
        o_ref[...] = x2 + inj * gate


def _ffn_call(xn, x1, p2d, wup, cw, wd, g_ple, w_gate, w_proj, seq):
    assert N_CHUNKS % 2 == 1
    assert MIX_TM == FFN_TM
    n = xn.shape[0]
    n_tiles = n // FFN_TM
    halo_per_tile = FFN_TM // HALO
    n_halo = n // HALO
    staged = lambda s: jnp.minimum(s, n_tiles - 1)
    done = lambda s: (jnp.maximum(s - 1, 0), 0)
    kern = functools.partial(_ffn_kernel, tiles_per_seq=seq // FFN_TM, n_tiles=n_tiles)
    return pl.pallas_call(
        kern,
        grid=(n_tiles + 1,),
        in_specs=[
            pl.BlockSpec((FFN_TM, D_MODEL), lambda s: (staged(s), 0)),
            pl.BlockSpec((HALO, D_MODEL), lambda s: (jnp.maximum(staged(s) * halo_per_tile - 1, 0), 0)),
            pl.BlockSpec((HALO, D_MODEL),
                         lambda s: (jnp.minimum((staged(s) + 1) * halo_per_tile, n_halo - 1), 0)),
            pl.BlockSpec((FFN_TM, D_MODEL), done),
            pl.BlockSpec((FFN_TM, PLE_DIM), done),
            _resident(wup.shape), _resident(cw.shape), _resident(wd.shape),
            _resident((1, D_MODEL)), _resident(w_gate.shape), _resident(w_proj.shape),
        ],
        out_specs=pl.BlockSpec((FFN_TM, D_MODEL), done),
        out_shape=jax.ShapeDtypeStruct((n, D_MODEL), F32),
        scratch_shapes=[pltpu.VMEM((FFN_TM + HALO, D_MODEL), BF16),
                        pltpu.VMEM((2, SUB + STRANDS * (FFN_TM // STRANDS + SUB), FFN_TF), F32),
                        pltpu.VMEM((2, SUB + STRANDS * (FFN_TM // STRANDS + SUB), FFN_TF), F32),
                        pltpu.VMEM((FFN_TM, D_MODEL), F32),
                        pltpu.VMEM((D_MODEL // LANES, FFN_TM, LANES), F32)],
        compiler_params=_params(("arbitrary",)),
        name="ffn_ple",
    )(xn, xn, xn, x1, p2d, wup, cw, wd, g_ple, w_gate, w_proj)


def _t5_bucket(rel):
    nb = REL_BUCKETS // 2
    ret = jnp.where(rel > 0, nb, 0)
    n = jnp.abs(rel)
    max_exact = nb // 2
    nf = jnp.maximum(n, 1).astype(F32)
    large = max_exact + (jnp.log(nf / max_exact) / math.log(REL_MAX_DIST / max_exact)
                         * (nb - max_exact)).astype(jnp.int32)
    large = jnp.minimum(large, nb - 1)
    return ret + jnp.where(n < max_exact, n, large)


def _band_bias(table, radius, dil, tq, kw):
    cases = []
    for off in (0, radius, kw - tq):
        rel = jnp.arange(kw)[None, :] - jnp.arange(tq)[:, None] - off
        onehot = (_t5_bucket(rel * dil)[..., None] == jnp.arange(REL_BUCKETS)).astype(F32)
        bias = jnp.einsum('qkn,nh->hqk', onehot, table.astype(F32), precision=lax.Precision.HIGHEST)
        bias = jnp.where((jnp.abs(rel) <= radius)[None], bias, NEG_INF)
        cases.append(bias.reshape(N_HEADS * tq, kw))
    return jnp.stack(cases, axis=0)


def _rope_tables(seq):
    half = HEAD_DIM // 2
    inv = ROPE_THETA ** (-jnp.arange(0, half, 2, dtype=F32) / half)
    t = jnp.arange(seq)
    ang_r = (t // GRID_W).astype(F32)[:, None] * inv[None, :]
    ang_c = (t % GRID_W).astype(F32)[:, None] * inv[None, :]
    cos_h = jnp.concatenate([jnp.cos(ang_r)] * 2 + [jnp.cos(ang_c)] * 2, axis=-1)
    sin_h = jnp.concatenate([-jnp.sin(ang_r), jnp.sin(ang_r), -jnp.sin(ang_c), jnp.sin(ang_c)], axis=-1)
    return jnp.tile(cos_h, (1, N_HEADS)), jnp.tile(sin_h, (1, N_HEADS))


def kernel(x, p, rel_bias, ln_mix_g, w_in, qk_gain, sink, c_norm_g, c_norm_b, c_ws, c_bs, out_gain, w_out,
           ln_ffn_g, w_up, conv_w, conv_b, w_down, ln_ple_g, w_ple_gate, w_ple_proj):
    b, s, _ = x.shape
    n = b * s
    depth = w_in.shape[0]
    cos_t, sin_t = _rope_tables(s)
    bd = (np.arange(GROUP_WIDTH)[:, None] // HEAD_DIM == np.arange(GROUP_WIDTH)[None, :] // HEAD_DIM)
    bd = jnp.asarray(bd / HEAD_DIM, BF16)
    table_a = rel_bias[:, :N_HEADS] * LOG2E
    table_b = rel_bias[:, N_HEADS:] * LOG2E
    q_scale = ATTN_SCALE * LOG2E
    bias_a = []
    for window, dil in DILATED_CFGS:
        radius = window // (2 * dil)
        length = s // dil
        tq = min(BAND_TQ, length)
        bias_a.append(_band_bias(table_a, radius, dil, tq, min(tq + 2 * radius, length)))
    tq_b = min(BAND_TQ, s)
    bias_b = _band_bias(table_b, SWA_RADIUS, 1, tq_b, min(tq_b + 2 * SWA_RADIUS, s))

    x2d = x.reshape(n, D_MODEL)
    for i in range(depth):
        w = w_in[i].astype(BF16)
        qg = qk_gain[i]
        vec = jnp.stack([
            jnp.tile(qg[0, 0], N_HEADS) * q_scale, jnp.tile(qg[0, 1], N_HEADS),
            jnp.tile(qg[1, 0], N_HEADS) * q_scale, jnp.tile(qg[1, 1], N_HEADS),
            jnp.tile(qg[2, 0], N_HEADS) * q_scale, jnp.tile(qg[2, 1], N_HEADS),
            c_norm_g[i], c_norm_b[i], out_gain[i, 2],
        ] + [jnp.zeros((GROUP_WIDTH,), F32)] * 7, axis=0)
        wcat = jnp.concatenate([c_ws[i, g] for g in range(C_GROUPS)], axis=1).astype(BF16)
        bsx = jnp.repeat(jnp.transpose(c_bs[i]), GROUP_WIDTH // C_GROUPS, axis=1)
        proj2d, d4, d16 = _proj_call(x2d, ln_mix_g[i][None], w, vec, cos_t, sin_t, bd, wcat, bsx, b, s)

        proj3d = proj2d.reshape(b, s, N_SEC * GROUP_WIDTH)
        ya = _dilated_call(proj3d, d4, d16, bias_a, out_gain[i, 0][None])
        sink_blk = jnp.full((N_HEADS * tq_b, LANES), NEG_INF, F32).at[:, 0].set(jnp.repeat(sink[i] * LOG2E, tq_b))
        yb = _swa_call(proj3d, bias_b, sink_blk, out_gain[i, 1][None])
        yd = _dense_call(proj3d, out_gain[i, 3][None])

        x1, xn = _mix_call(x2d, ya.reshape(n, GROUP_WIDTH), yb.reshape(n, GROUP_WIDTH), proj2d,
                           yd.reshape(n, GROUP_WIDTH), w_out[i].astype(BF16), ln_ffn_g[i][None])
        wup = w_up[i].astype(BF16)
        cw = jnp.concatenate([conv_w[i], conv_b[i][None]], axis=0)
        cw = jnp.transpose(cw.reshape(4, 2 * N_CHUNKS, FFN_TF), (1, 0, 2))
        wd = w_down[i].astype(BF16).reshape(N_CHUNKS, FFN_TF, D_MODEL)
        x2d = _ffn_call(xn, x1, p[i].reshape(n, PLE_DIM), wup, cw, wd, ln_ple_g[i][None],
                        w_ple_gate[i].astype(BF16), w_ple_proj[i].astype(BF16), s)
    return x2d.reshape(b, s, D_MODEL)
```

```python
import functools
import math

import jax
import jax.numpy as jnp
import numpy as np
from jax import lax
from jax.experimental import pallas as pl
from jax.experimental.pallas import tpu as pltpu

D_MODEL = 1024
HEAD_DIM = 64
GROUP_WIDTH = 256
N_HEADS = GROUP_WIDTH // HEAD_DIM
C_GROUPS = 4
C_CHUNK = 128
DILATED_CFGS = ((128, 1), (512, 4), (2048, 16))
SWA_RADIUS = 128
GRID_W = 64
ROPE_THETA = 10000.0
REL_BUCKETS = 32
REL_MAX_DIST = 1024
D_FF = 2816
PLE_DIM = 256
EPS = 1e-6
NEG_INF = -1e30
ATTN_SCALE = HEAD_DIM ** -0.5
LOG2E = math.log2(math.e)

BF16 = jnp.bfloat16
F32 = jnp.float32

SEC_AQ, SEC_AK, SEC_AV, SEC_BQ, SEC_BK, SEC_BV, SEC_YC, SEC_DQ, SEC_DK, SEC_DV = range(10)
N_SEC = 10
A_WIDTH = 3 * GROUP_WIDTH

VMEM_LIMIT = 56 * 1024 * 1024

KIN_TM = 512
KIN_ROWS = 512
BAND_TQ = 128
BAND_UNROLL = 16
SWA_UNROLL = 4
DENSE_TQ = 512
DENSE_ROWS = 1024
MIX_TM = 512
FFN_TM = 512
FFN_TF = 256
LANES = 128
SUB = 8
STRANDS = 8
HALO = 16
N_CHUNKS = D_FF // FFN_TF


def _params(sem):
    return pltpu.CompilerParams(dimension_semantics=sem, vmem_limit_bytes=VMEM_LIMIT)


def _resident(shape):
    zeros = (0,) * len(shape)
    return pl.BlockSpec(shape, lambda *_: zeros, pipeline_mode=pl.Buffered(1))


def _rms(t, gain):
    return t * lax.rsqrt(jnp.mean(t * t, axis=-1, keepdims=True) + EPS) * gain


def _head_id(shape):
    return lax.broadcasted_iota(jnp.int32, shape, 1) >> 6


def _proj_kernel(x_ref, g_ref, w_ref, vec_ref, cos_ref, sin_ref, bd_ref, wcat_ref, bsx_ref,
                 o_ref, d4_ref, d16_ref, a_ref):
    for t in range(x_ref.shape[0] // KIN_TM):
        _proj_tile(t, x_ref, g_ref, w_ref, vec_ref, cos_ref, sin_ref, bd_ref, wcat_ref, bsx_ref,
                   o_ref, d4_ref, d16_ref, a_ref)


def _proj_tile(t, x_ref, g_ref, w_ref, vec_ref, cos_ref, sin_ref, bd_ref, wcat_ref, bsx_ref,
               o_ref, d4_ref, d16_ref, a_ref):
    tm = KIN_TM
    r0 = t * tm
    x = x_ref[r0:r0 + tm, :]
    hn = _rms(x, g_ref[...]).astype(BF16)
    bd = bd_ref[...]
    slots = A_WIDTH // LANES

    def proj(sec):
        return jnp.dot(hn, w_ref[:, sec * GROUP_WIDTH:(sec + 1) * GROUP_WIDTH],
                       preferred_element_type=F32)

    def store(sec, val):
        o_ref[r0:r0 + tm, sec * GROUP_WIDTH:(sec + 1) * GROUP_WIDTH] = val.astype(BF16)

    def store_a(sec, val):
        store(sec, val)
        for half in range(GROUP_WIDTH // LANES):
            slot = sec * (GROUP_WIDTH // LANES) + half
            a_ref[t * slots + slot] = val[:, half * LANES:(half + 1) * LANES]
            cols = slice(slot * LANES, (slot + 1) * LANES)
            for dil, ref in ((4, d4_ref), (16, d16_ref)):
                for r in range(dil):
                    piece = a_ref[t * slots + slot, pl.ds(r, tm // dil, stride=dil), :]
                    ref[r, t * (tm // dil):(t + 1) * (tm // dil), cols] = piece.astype(BF16)

    def head_rms(t, row):
        width = t.shape[1]
        sq = t * t
        hi = sq.astype(BF16)
        lo = (sq - hi.astype(F32)).astype(BF16)
        blk = bd[0:width, 0:width]
        ms = (jnp.dot(hi, blk, preferred_element_type=F32) + jnp.dot(lo, blk, preferred_element_type=F32))
        return t * lax.rsqrt(ms + EPS) * vec_ref[row:row + 1, 0:width]

    def rope(t):
        blocks = []
        for c0 in range(0, t.shape[1], LANES):
            th = t[:, c0:c0 + LANES]
            lane = lax.broadcasted_iota(jnp.int32, th.shape, 1)
            partner = jnp.where((lane & 31) < 16, pltpu.roll(th, LANES - 16, axis=1), pltpu.roll(th, 16, axis=1))
            blocks.append(th * cos_ref[r0:r0 + tm, c0:c0 + LANES] + partner * sin_ref[r0:r0 + tm, c0:c0 + LANES])
        return blocks[0] if len(blocks) == 1 else jnp.concatenate(blocks, axis=-1)

    def rep_kv(t):
        swapped = pltpu.roll(t, HEAD_DIM, axis=1)
        low = lax.broadcasted_iota(jnp.int32, t.shape, 1) < HEAD_DIM
        return jnp.concatenate([jnp.where(low, t, swapped), jnp.where(low, swapped, t)], axis=-1)

    store_a(SEC_AQ, head_rms(proj(0), 0))
    store_a(SEC_AK, head_rms(proj(1), 1))
    store_a(SEC_AV, proj(2))
    store(SEC_BQ, head_rms(proj(3), 2))
    b_kv = proj(4)
    store(SEC_BK, rep_kv(head_rms(b_kv[:, 0:LANES], 3)))
    store(SEC_BV, rep_kv(b_kv[:, LANES:]))
    store(SEC_DQ, rope(head_rms(proj(7), 4)))
    d_kv = proj(8)
    store(SEC_DK, rep_kv(rope(head_rms(d_kv[:, 0:LANES], 5))))
    store(SEC_DV, jnp.concatenate([d_kv[:, LANES:]] * 2, axis=-1))

    u = jax.nn.gelu(proj(5))
    v = jax.nn.gelu(proj(6))
    mu = jnp.mean(v, axis=-1, keepdims=True)
    vc = v - mu
    var = jnp.mean(vc * vc, axis=-1, keepdims=True)
    vn = (vc * lax.rsqrt(var + EPS) * vec_ref[6:7, :] + vec_ref[7:8, :]).astype(BF16)
    gid = _head_id((C_CHUNK, GROUP_WIDTH))
    zero = jnp.zeros((C_CHUNK, GROUP_WIDTH), BF16)
    for c in range(tm // C_CHUNK):
        rows = slice(c * C_CHUNK, (c + 1) * C_CHUNK)
        vchunk = vn[rows]
        stacked = jnp.concatenate([jnp.where(gid == g, vchunk, zero) for g in range(C_GROUPS)], axis=0)
        mixed = jnp.dot(wcat_ref[...], stacked, preferred_element_type=F32) + bsx_ref[...]
        yc = _rms(u[rows] * mixed, vec_ref[8:9, :])
        o_ref[r0 + c * C_CHUNK:r0 + (c + 1) * C_CHUNK, SEC_YC * GROUP_WIDTH:(SEC_YC + 1) * GROUP_WIDTH] = yc.astype(BF16)


def _proj_call(x2d, g, w, vec, cos_t, sin_t, bd, wcat, bsx, batch, seq):
    n = x2d.shape[0]
    tps = seq // KIN_ROWS
    return pl.pallas_call(
        _proj_kernel,
        grid=(n // KIN_ROWS,),
        in_specs=[
            pl.BlockSpec((KIN_ROWS, D_MODEL), lambda i: (i, 0)),
            _resident((1, D_MODEL)),
            _resident(w.shape),
            _resident(vec.shape),
            pl.BlockSpec((KIN_ROWS, GROUP_WIDTH), lambda i: (i % tps, 0)),
            pl.BlockSpec((KIN_ROWS, GROUP_WIDTH), lambda i: (i % tps, 0)),
            _resident(bd.shape),
            _resident(wcat.shape),
            _resident(bsx.shape),
        ],
        out_specs=[
            pl.BlockSpec((KIN_ROWS, N_SEC * GROUP_WIDTH), lambda i: (i, 0)),
            pl.BlockSpec((None, 4, KIN_ROWS // 4, A_WIDTH), lambda i: (i // tps, 0, i % tps, 0)),
            pl.BlockSpec((None, 16, KIN_ROWS // 16, A_WIDTH), lambda i: (i // tps, 0, i % tps, 0)),
        ],
        out_shape=[
            jax.ShapeDtypeStruct((n, N_SEC * GROUP_WIDTH), BF16),
            jax.ShapeDtypeStruct((batch, 4, seq // 4, A_WIDTH), BF16),
            jax.ShapeDtypeStruct((batch, 16, seq // 16, A_WIDTH), BF16),
        ],
        scratch_shapes=[pltpu.VMEM((KIN_ROWS // KIN_TM * (A_WIDTH // LANES), KIN_TM, LANES), F32)],
        compiler_params=_params(("parallel",)),
        name="proj",
    )(x2d, g, w, vec, cos_t, sin_t, bd, wcat, bsx)


def _attend(q, kwin, vwin, bias_ref, case, sink_ref=None):
    tq = q.shape[0]
    hid = _head_id((tq, GROUP_WIDTH))
    qzero = jnp.zeros_like(q)
    q4 = jnp.concatenate([jnp.where(hid == h, q, qzero) for h in range(N_HEADS)], axis=0)
    s = lax.dot_general(q4, kwin, (((1,), (1,)), ((), ())), preferred_element_type=F32)
    s = s + bias_ref[case]
    if sink_ref is None:
        m = jnp.max(s, axis=-1, keepdims=True)
        p = jnp.exp2(s - m)
        den = jnp.sum(p, axis=-1, keepdims=True)
    else:
        sink = sink_ref[...]
        blocks = [s[:, c:c + LANES] for c in range(0, s.shape[1], LANES)]
        m = jnp.max(functools.reduce(jnp.maximum, blocks + [sink]), axis=-1, keepdims=True)
        p = jnp.exp2(s - m)
        pblocks = [p[:, c:c + LANES] for c in range(0, s.shape[1], LANES)]
        den = jnp.sum(functools.reduce(jnp.add, pblocks + [jnp.exp2(sink - m)]), axis=-1, keepdims=True)
    o4 = jnp.dot(p.astype(BF16), vwin, preferred_element_type=F32) * (1.0 / den)
    l4 = jnp.broadcast_to(m + jnp.log2(den), o4.shape)
    out = o4[0:tq]
    lse = l4[0:tq]
    for h in range(1, N_HEADS):
        out = jnp.where(hid == h, o4[h * tq:(h + 1) * tq], out)
        lse = jnp.where(hid == h, l4[h * tq:(h + 1) * tq], lse)
    return out, lse


def _band_window(j, tq, radius, length, kw):
    n_tiles = length // tq
    if n_tiles == 1:
        return 0, 0
    ks = pl.multiple_of(jnp.clip(j * tq - radius, 0, length - kw), 64)
    case = jnp.where(j == 0, 0, jnp.where(j == n_tiles - 1, 2, 1))
    return ks, case


def _dilated_kernel(q_ref, k_ref, v_ref, d4_ref, d16_ref, b1_ref, b4_ref, b16_ref, gain_ref, o_ref,
                    acc_ref, lse_ref, *, radii):
    seq = q_ref.shape[0]
    tq = BAND_TQ

    kw1 = min(tq + 2 * radii[0], seq)

    def body1(j, carry):
        q0 = pl.multiple_of(j * tq, tq)
        ks, case = _band_window(j, tq, radii[0], seq, kw1)
        o, l = _attend(q_ref[pl.ds(q0, tq), :], k_ref[pl.ds(ks, kw1), :], v_ref[pl.ds(ks, kw1), :], b1_ref, case)
        for half in range(GROUP_WIDTH // LANES):
            cols = slice(half * LANES, (half + 1) * LANES)
            acc_ref[half, pl.ds(q0, tq), :] = o[:, cols]
            lse_ref[half, pl.ds(q0, tq), :] = l[:, cols]
        return carry

    lax.fori_loop(0, seq // tq, body1, 0, unroll=BAND_UNROLL)

    for dil, ref, bias_ref, radius in ((4, d4_ref, b4_ref, radii[1]), (16, d16_ref, b16_ref, radii[2])):
        length = seq // dil
        tiles = length // tq
        kw = min(tq + 2 * radius, length)

        def body(it, carry, dil=dil, ref=ref, bias_ref=bias_ref, radius=radius, length=length, tiles=tiles, kw=kw):
            r = it // tiles
            j = it % tiles
            q0 = pl.multiple_of(j * tq, tq)
            ks, case = _band_window(j, tq, radius, length, kw)
            o, l = _attend(ref[r, pl.ds(q0, tq), 0:GROUP_WIDTH],
                           ref[r, pl.ds(ks, kw), GROUP_WIDTH:2 * GROUP_WIDTH],
                           ref[r, pl.ds(ks, kw), 2 * GROUP_WIDTH:3 * GROUP_WIDTH], bias_ref, case)
            rows = pl.ds(r + dil * q0, tq, stride=dil)
            for half in range(GROUP_WIDTH // LANES):
                cols = slice(half * LANES, (half + 1) * LANES)
                l_new = l[:, cols]
                l_old = lse_ref[half, rows, :]
                m = jnp.maximum(l_old, l_new)
                wa = jnp.exp2(l_old - m)
                wb = jnp.exp2(l_new - m)
                tot = wa + wb
                acc_ref[half, rows, :] = (wa * acc_ref[half, rows, :] + wb * o[:, cols]) * (1.0 / tot)
                lse_ref[half, rows, :] = m + jnp.log2(tot)
            return carry

        lax.fori_loop(0, dil * tiles, body, 0, unroll=BAND_UNROLL)

    acc = jnp.concatenate([acc_ref[0], acc_ref[1]], axis=-1)
    o_ref[...] = _rms(acc, gain_ref[...]).astype(BF16)


def _dilated_call(proj3d, d4, d16, biases, gain):
    b, seq, _ = proj3d.shape
    radii = tuple(window // (2 * dil) for window, dil in DILATED_CFGS)
    kern = functools.partial(_dilated_kernel, radii=radii)
    return pl.pallas_call(
        kern,
        grid=(b,),
        in_specs=[
            pl.BlockSpec((None, seq, GROUP_WIDTH), lambda bi: (bi, 0, SEC_AQ)),
            pl.BlockSpec((None, seq, GROUP_WIDTH), lambda bi: (bi, 0, SEC_AK)),
            pl.BlockSpec((None, seq, GROUP_WIDTH), lambda bi: (bi, 0, SEC_AV)),
            pl.BlockSpec((None,) + d4.shape[1:], lambda bi: (bi, 0, 0, 0)),
            pl.BlockSpec((None,) + d16.shape[1:], lambda bi: (bi, 0, 0, 0)),
            _resident(biases[0].shape), _resident(biases[1].shape), _resident(biases[2].shape),
            _resident((1, GROUP_WIDTH)),
        ],
        out_specs=pl.BlockSpec((None, seq, GROUP_WIDTH), lambda bi: (bi, 0, 0)),
        out_shape=jax.ShapeDtypeStruct((b, seq, GROUP_WIDTH), BF16),
        scratch_shapes=[pltpu.VMEM((GROUP_WIDTH // LANES, seq, LANES), F32)] * 2,
        compiler_params=_params(("parallel",)),
        name="dilated",
    )(proj3d, proj3d, proj3d, d4, d16, biases[0], biases[1], biases[2], gain)


def _swa_kernel(q_ref, k_ref, v_ref, bias_ref, sink_ref, gain_ref, o_ref):
    seq = q_ref.shape[0]
    tq = BAND_TQ
    kw = min(tq + 2 * SWA_RADIUS, seq)

    def body(j, carry):
        q0 = pl.multiple_of(j * tq, tq)
        ks, case = _band_window(j, tq, SWA_RADIUS, seq, kw)
        o, _ = _attend(q_ref[pl.ds(q0, tq), :], k_ref[pl.ds(ks, kw), :], v_ref[pl.ds(ks, kw), :],
                       bias_ref, case, sink_ref)
        o_ref[pl.ds(q0, tq), :] = _rms(o, gain_ref[...]).astype(BF16)
        return carry

    lax.fori_loop(0, seq // tq, body, 0, unroll=SWA_UNROLL)


def _swa_call(proj3d, bias, sink, gain):
    b, seq, _ = proj3d.shape
    return pl.pallas_call(
        _swa_kernel,
        grid=(b,),
        in_specs=[
            pl.BlockSpec((None, seq, GROUP_WIDTH), lambda bi: (bi, 0, SEC_BQ)),
            pl.BlockSpec((None, seq, GROUP_WIDTH), lambda bi: (bi, 0, SEC_BK)),
            pl.BlockSpec((None, seq, GROUP_WIDTH), lambda bi: (bi, 0, SEC_BV)),
            _resident(bias.shape),
            _resident(sink.shape),
            _resident((1, GROUP_WIDTH)),
        ],
        out_specs=pl.BlockSpec((None, seq, GROUP_WIDTH), lambda bi: (bi, 0, 0)),
        out_shape=jax.ShapeDtypeStruct((b, seq, GROUP_WIDTH), BF16),
        compiler_params=_params(("parallel",)),
        name="swa",
    )(proj3d, proj3d, proj3d, bias, sink, gain)


def _dense_kernel(q_ref, k_ref, v_ref, gain_ref, o_ref):
    k = k_ref[...]
    vext = jnp.concatenate([v_ref[...], jnp.ones((k.shape[0], LANES), BF16)], axis=-1)
    tq = DENSE_TQ
    hid = _head_id((tq, GROUP_WIDTH))
    low = lax.broadcasted_iota(jnp.int32, (tq, LANES), 1) < HEAD_DIM
    for t in range(q_ref.shape[0] // tq):
        q = q_ref[t * tq:(t + 1) * tq, :]
        qzero = jnp.zeros_like(q)
        heads = []
        for h in range(N_HEADS):
            qh = jnp.where(hid == h, q, qzero)
            s = lax.dot_general(qh, k, (((1,), (1,)), ((), ())), preferred_element_type=F32)
            m = jnp.max(s, axis=-1, keepdims=True)
            p = jnp.exp2((s - m).astype(BF16))
            o = jnp.dot(p, vext, preferred_element_type=F32)
            heads.append(o[:, 0:LANES] * (1.0 / o[:, LANES:]))
        blk0 = jnp.where(low, heads[0], pltpu.roll(heads[1], HEAD_DIM, axis=1))
        blk1 = jnp.where(low, pltpu.roll(heads[2], HEAD_DIM, axis=1), heads[3])
        acc = jnp.concatenate([blk0, blk1], axis=-1)
        o_ref[t * tq:(t + 1) * tq, :] = _rms(acc, gain_ref[...]).astype(BF16)


def _dense_call(proj3d, gain):
    b, seq, _ = proj3d.shape
    return pl.pallas_call(
        _dense_kernel,
        grid=(b, seq // DENSE_ROWS),
        in_specs=[
            pl.BlockSpec((None, DENSE_ROWS, GROUP_WIDTH), lambda bi, j: (bi, j, SEC_DQ)),
            pl.BlockSpec((None, seq, GROUP_WIDTH), lambda bi, j: (bi, 0, SEC_DK)),
            pl.BlockSpec((None, seq, LANES), lambda bi, j: (bi, 0, SEC_DV * (GROUP_WIDTH // LANES))),
            _resident((1, GROUP_WIDTH)),
        ],
        out_specs=pl.BlockSpec((None, DENSE_ROWS, GROUP_WIDTH), lambda bi, j: (bi, j, 0)),
        out_shape=jax.ShapeDtypeStruct((b, seq, GROUP_WIDTH), BF16),
        compiler_params=_params(("parallel", "arbitrary")),
        name="dense",
    )(proj3d, proj3d, proj3d, gain)


def _mix_kernel(x_ref, ya_ref, yb_ref, yc_ref, yd_ref, w_ref, g_ref, x1_ref, xn_ref, xs_ref):
    mixed = jnp.concatenate([ya_ref[...], yb_ref[...], yc_ref[...], yd_ref[...]], axis=-1)
    x1 = x_ref[...] + jnp.dot(mixed, w_ref[...], preferred_element_type=F32)
    x1_ref[...] = x1
    xn = _rms(x1, g_ref[...])
    tm = xn.shape[0]
    for c in range(D_MODEL // LANES):
        xs_ref[c] = xn[:, c * LANES:(c + 1) * LANES]
        col = jnp.concatenate([xs_ref[c, pl.ds(b, tm // STRANDS, stride=STRANDS), :] for b in range(STRANDS)],
                              axis=0)
        xn_ref[:, c * LANES:(c + 1) * LANES] = col.astype(BF16)


def _mix_call(x2d, ya, yb, proj2d, yd, w_out, g_ffn):
    n = x2d.shape[0]
    row = lambda i: (i, 0)
    grp = pl.BlockSpec((MIX_TM, GROUP_WIDTH), row)
    return pl.pallas_call(
        _mix_kernel,
        grid=(n // MIX_TM,),
        in_specs=[pl.BlockSpec((MIX_TM, D_MODEL), row), grp, grp,
                  pl.BlockSpec((MIX_TM, GROUP_WIDTH), lambda i: (i, SEC_YC)), grp,
                  _resident(w_out.shape), _resident((1, D_MODEL))],
        out_specs=[pl.BlockSpec((MIX_TM, D_MODEL), row), pl.BlockSpec((MIX_TM, D_MODEL), row)],
        out_shape=[jax.ShapeDtypeStruct((n, D_MODEL), F32), jax.ShapeDtypeStruct((n, D_MODEL), BF16)],
        scratch_shapes=[pltpu.VMEM((D_MODEL // LANES, MIX_TM, LANES), F32)],
        compiler_params=_params(("parallel",)),
        name="mix_out",
    )(x2d, ya, yb, proj2d, yd, w_out, g_ffn)


def _ffn_kernel(xn_ref, prev_ref, next_ref, x1_ref, p_ref, wup_ref, cw_ref, wd_ref, gple_ref, wgate_ref,
                wproj_ref, o_ref, xcat_ref, h0_ref, h1_ref, acc_ref, tok_ref, *, tiles_per_seq, n_tiles):
    step = pl.program_id(0)
    tm = xn_ref.shape[0]
    rows = tm + HALO
    sl = tm // STRANDS
    pitch = sl + SUB

    def up(k, h_ref):
        xcat = xcat_ref[...]
        for part in range(2):
            col = (k + part * N_CHUNKS) * FFN_TF
            if not isinstance(col, int):
                col = pl.multiple_of(col, FFN_TF)
            h = jnp.dot(xcat, wup_ref[:, pl.ds(col, FFN_TF)], preferred_element_type=F32)
            for b in range(STRANDS):
                h_ref[part, SUB + b * pitch:SUB + b * pitch + sl, :] = h[b * sl:(b + 1) * sl]
            h_ref[part, SUB + sl:SUB + pitch, :] = h[tm:tm + SUB]
            h_ref[part, (STRANDS - 1) * pitch:SUB + (STRANDS - 1) * pitch, :] = h[rows - SUB:rows]

    def strand(h_ref, part, b, shift=0):
        start = SUB + b * pitch + shift
        return h_ref[part, start:start + sl, :]

    def conv(h_ref, part, cw):
        out = []
        for b in range(STRANDS):
            before = strand(h_ref, part, b - 1) if b > 0 else strand(h_ref, part, STRANDS - 1, -1)
            after = strand(h_ref, part, b + 1) if b < STRANDS - 1 else strand(h_ref, part, 0, 1)
            out.append(before * cw[0:1, :] + strand(h_ref, part, b) * cw[1:2, :] + after * cw[2:3, :] + cw[3:4, :])
        return jnp.concatenate(out, axis=0)

    def down(k, h_ref):
        gt = conv(h_ref, 0, cw_ref[k])
        uu = conv(h_ref, 1, cw_ref[k + N_CHUNKS])
        act = (gt / (1.0 + jnp.exp(-gt)) * uu).astype(BF16)
        acc_ref[...] += jnp.dot(act, wd_ref[k], preferred_element_type=F32)

    def stage():
        pos = jnp.minimum(step, n_tiles - 1) % tiles_per_seq
        prev_blk = jnp.where(pos == 0, jnp.zeros_like(prev_ref[...]), prev_ref[...])
        next_blk = jnp.where(pos == tiles_per_seq - 1, jnp.zeros_like(next_ref[...]), next_ref[...])
        halo_row = lax.broadcasted_iota(jnp.int32, (HALO, D_MODEL), 0)
        xcat_ref[0:tm, :] = xn_ref[...]
        xcat_ref[tm:rows, :] = jnp.where(halo_row == 0, next_blk, prev_blk)
        up(0, h0_ref)

    @pl.when(step == 0)
    def _():
        stage()

    @pl.when(step > 0)
    def _():
        acc_ref[...] = jnp.zeros_like(acc_ref)

        def pair(t, carry):
            k = 2 * t
            up(k + 1, h1_ref)
            down(k, h0_ref)
            up(k + 2, h0_ref)
            down(k + 1, h1_ref)
            return carry

        lax.fori_loop(0, (N_CHUNKS - 1) // 2, pair, 0)
        down(N_CHUNKS - 1, h0_ref)

        stage()

        for c in range(D_MODEL // LANES):
            for b in range(STRANDS):
                tok_ref[c, pl.ds(b, sl, stride=STRANDS), :] = acc_ref[b * sl:(b + 1) * sl, c * LANES:(c + 1) * LANES]
        ffn = jnp.concatenate([tok_ref[c] for c in range(D_MODEL // LANES)], axis=-1)
        x2 = x1_ref[...] + ffn
        z = jnp.dot(_rms(x2, gple_ref[...]).astype(BF16), wgate_ref[...], preferred_element_type=F32)
        gate = 1.0 / (1.0 + jnp.exp(-z))
        inj = jnp.dot(p_ref[...].astype(BF16), wproj_ref[...], preferred_element_type=F32)
        o_ref[...] = x2 + inj * gate


def _ffn_call(xn, x1, p2d, wup, cw, wd, g_ple, w_gate, w_proj, seq):
    assert N_CHUNKS % 2 == 1
    assert MIX_TM == FFN_TM
    n = xn.shape[0]
    n_tiles = n // FFN_TM
    halo_per_tile = FFN_TM // HALO
    n_halo = n // HALO
    staged = lambda s: jnp.minimum(s, n_tiles - 1)
    done = lambda s: (jnp.maximum(s - 1, 0), 0)
    kern = functools.partial(_ffn_kernel, tiles_per_seq=seq // FFN_TM, n_tiles=n_tiles)
    return pl.pallas_call(
        kern,
        grid=(n_tiles + 1,),
        in_specs=[
            pl.BlockSpec((FFN_TM, D_MODEL), lambda s: (staged(s), 0)),
            pl.BlockSpec((HALO, D_MODEL), lambda s: (jnp.maximum(staged(s) * halo_per_tile - 1, 0), 0)),
            pl.BlockSpec((HALO, D_MODEL),
                         lambda s: (jnp.minimum((staged(s) + 1) * halo_per_tile, n_halo - 1), 0)),
            pl.BlockSpec((FFN_TM, D_MODEL), done),
            pl.BlockSpec((FFN_TM, PLE_DIM), done),
            _resident(wup.shape), _resident(cw.shape), _resident(wd.shape),
            _resident((1, D_MODEL)), _resident(w_gate.shape), _resident(w_proj.shape),
        ],
        out_specs=pl.BlockSpec((FFN_TM, D_MODEL), done),
        out_shape=jax.ShapeDtypeStruct((n, D_MODEL), F32),
        scratch_shapes=[pltpu.VMEM((FFN_TM + HALO, D_MODEL), BF16),
                        pltpu.VMEM((2, SUB + STRANDS * (FFN_TM // STRANDS + SUB), FFN_TF), F32),
                        pltpu.VMEM((2, SUB + STRANDS * (FFN_TM // STRANDS + SUB), FFN_TF), F32),
                        pltpu.VMEM((FFN_TM, D_MODEL), F32),
                        pltpu.VMEM((D_MODEL // LANES, FFN_TM, LANES), F32)],
        compiler_params=_params(("arbitrary",)),
        name="ffn_ple",
    )(xn, xn, xn, x1, p2d, wup, cw, wd, g_ple, w_gate, w_proj)


def _t5_bucket(rel):
    nb = REL_BUCKETS // 2
    ret = jnp.where(rel > 0, nb, 0)
    n = jnp.abs(rel)
    max_exact = nb // 2
    nf = jnp.maximum(n, 1).astype(F32)
    large = max_exact + (jnp.log(nf / max_exact) / math.log(REL_MAX_DIST / max_exact)
                         * (nb - max_exact)).astype(jnp.int32)
    large = jnp.minimum(large, nb - 1)
    return ret + jnp.where(n < max_exact, n, large)


def _band_bias(table, radius, dil, tq, kw):
    cases = []
    for off in (0, radius, kw - tq):
        rel = jnp.arange(kw)[None, :] - jnp.arange(tq)[:, None] - off
        onehot = (_t5_bucket(rel * dil)[..., None] == jnp.arange(REL_BUCKETS)).astype(F32)
        bias = jnp.einsum('qkn,nh->hqk', onehot, table.astype(F32), precision=lax.Precision.HIGHEST)
        bias = jnp.where((jnp.abs(rel) <= radius)[None], bias, NEG_INF)
        cases.append(bias.reshape(N_HEADS * tq, kw))
    return jnp.stack(cases, axis=0)


def _rope_tables(seq):
    half = HEAD_DIM // 2
    inv = ROPE_THETA ** (-jnp.arange(0, half, 2, dtype=F32) / half)
    t = jnp.arange(seq)
    ang_r = (t // GRID_W).astype(F32)[:, None] * inv[None, :]
    ang_c = (t % GRID_W).astype(F32)[:, None] * inv[None, :]
    cos_h = jnp.concatenate([jnp.cos(ang_r)] * 2 + [jnp.cos(ang_c)] * 2, axis=-1)
    sin_h = jnp.concatenate([-jnp.sin(ang_r), jnp.sin(ang_r), -jnp.sin(ang_c), jnp.sin(ang_c)], axis=-1)
    return jnp.tile(cos_h, (1, N_HEADS)), jnp.tile(sin_h, (1, N_HEADS))


def kernel(x, p, rel_bias, ln_mix_g, w_in, qk_gain, sink, c_norm_g, c_norm_b, c_ws, c_bs, out_gain, w_out,
           ln_ffn_g, w_up, conv_w, conv_b, w_down, ln_ple_g, w_ple_gate, w_ple_proj):
    b, s, _ = x.shape
    n = b * s
    depth = w_in.shape[0]
    cos_t, sin_t = _rope_tables(s)
    bd = (np.arange(GROUP_WIDTH)[:, None] // HEAD_DIM == np.arange(GROUP_WIDTH)[None, :] // HEAD_DIM)
    bd = jnp.asarray(bd / HEAD_DIM, BF16)
    table_a = rel_bias[:, :N_HEADS] * LOG2E
    table_b = rel_bias[:, N_HEADS:] * LOG2E
    q_scale = ATTN_SCALE * LOG2E
    bias_a = []
    for window, dil in DILATED_CFGS:
        radius = window // (2 * dil)
        length = s // dil
        tq = min(BAND_TQ, length)
        bias_a.append(_band_bias(table_a, radius, dil, tq, min(tq + 2 * radius, length)))
    tq_b = min(BAND_TQ, s)
    bias_b = _band_bias(table_b, SWA_RADIUS, 1, tq_b, min(tq_b + 2 * SWA_RADIUS, s))

    x2d = x.reshape(n, D_MODEL)
    for i in range(depth):
        w = w_in[i].astype(BF16)
        qg = qk_gain[i]
        vec = jnp.stack([
            jnp.tile(qg[0, 0], N_HEADS) * q_scale, jnp.tile(qg[0, 1], N_HEADS),
            jnp.tile(qg[1, 0], N_HEADS) * q_scale, jnp.tile(qg[1, 1], N_HEADS),
            jnp.tile(qg[2, 0], N_HEADS) * q_scale, jnp.tile(qg[2, 1], N_HEADS),
            c_norm_g[i], c_norm_b[i], out_gain[i, 2],
        ] + [jnp.zeros((GROUP_WIDTH,), F32)] * 7, axis=0)
        wcat = jnp.concatenate([c_ws[i, g] for g in range(C_GROUPS)], axis=1).astype(BF16)
        bsx = jnp.repeat(jnp.transpose(c_bs[i]), GROUP_WIDTH // C_GROUPS, axis=1)
        proj2d, d4, d16 = _proj_call(x2d, ln_mix_g[i][None], w, vec, cos_t, sin_t, bd, wcat, bsx, b, s)

        proj3d = proj2d.reshape(b, s, N_SEC * GROUP_WIDTH)
        ya = _dilated_call(proj3d, d4, d16, bias_a, out_gain[i, 0][None])
        sink_blk = jnp.full((N_HEADS * tq_b, LANES), NEG_INF, F32).at[:, 0].set(jnp.repeat(sink[i] * LOG2E, tq_b))
        yb = _swa_call(proj3d, bias_b, sink_blk, out_gain[i, 1][None])
        yd = _dense_call(proj3d, out_gain[i, 3][None])

        x1, xn = _mix_call(x2d, ya.reshape(n, GROUP_WIDTH), yb.reshape(n, GROUP_WIDTH), proj2d,
                           yd.reshape(n, GROUP_WIDTH), w_out[i].astype(BF16), ln_ffn_g[i][None])
        wup = w_up[i].astype(BF16)
        cw = jnp.concatenate([conv_w[i], conv_b[i][None]], axis=0)
        cw = jnp.transpose(cw.reshape(4, 2 * N_CHUNKS, FFN_TF), (1, 0, 2))
        wd = w_down[i].astype(BF16).reshape(N_CHUNKS, FFN_TF, D_MODEL)
        x2d = _ffn_call(xn, x1, p[i].reshape(n, PLE_DIM), wup, cw, wd, ln_ple_g[i][None],
                        w_ple_gate[i].astype(BF16), w_ple_proj[i].astype(BF16), s)
    return x2d.reshape(b, s, D_MODEL)
```

```python
import functools
import math

import jax
import jax.numpy as jnp
import numpy as np
from jax import lax
from jax.experimental import pallas as pl
from jax.experimental.pallas import tpu as pltpu

D_MODEL = 1024
HEAD_DIM = 64
GROUP_WIDTH = 256
N_HEADS = GROUP_WIDTH // HEAD_DIM
C_GROUPS = 4
C_CHUNK = 128
DILATED_CFGS = ((128, 1), (512, 4), (2048, 16))
SWA_RADIUS = 128
GRID_W = 64
ROPE_THETA = 10000.0
REL_BUCKETS = 32
REL_MAX_DIST = 1024
D_FF = 2816
PLE_DIM = 256
EPS = 1e-6
NEG_INF = -1e30
ATTN_SCALE = HEAD_DIM ** -0.5
LOG2E = math.log2(math.e)

BF16 = jnp.bfloat16
F32 = jnp.float32

SEC_AQ, SEC_AK, SEC_AV, SEC_BQ, SEC_BK, SEC_BV, SEC_YC, SEC_DQ, SEC_DK, SEC_DV = range(10)
N_SEC = 10
A_WIDTH = 3 * GROUP_WIDTH

VMEM_LIMIT = 56 * 1024 * 1024

KIN_TM = 512
KIN_ROWS = 512
BAND_TQ = 128
BAND_UNROLL = 16
SWA_UNROLL = 4
DENSE_TQ = 512
DENSE_ROWS = 1024
MIX_TM = 512
FFN_TM = 512
FFN_TF = 256
LANES = 128
SUB = 8
STRANDS = 8
HALO = 16
N_CHUNKS = D_FF // FFN_TF
FFN_UNROLL = True


def _params(sem):
    return pltpu.CompilerParams(dimension_semantics=sem, vmem_limit_bytes=VMEM_LIMIT)


def _resident(shape):
    zeros = (0,) * len(shape)
    return pl.BlockSpec(shape, lambda *_: zeros, pipeline_mode=pl.Buffered(1))


def _rms(t, gain):
    return t * lax.rsqrt(jnp.mean(t * t, axis=-1, keepdims=True) + EPS) * gain


def _head_id(shape):
    return lax.broadcasted_iota(jnp.int32, shape, 1) >> 6


def _proj_kernel(x_ref, g_ref, w_ref, vec_ref, cos_ref, sin_ref, bd_ref, wcat_ref, bsx_ref,
                 o_ref, d4_ref, d16_ref, a_ref):
    for t in range(x_ref.shape[0] // KIN_TM):
        _proj_tile(t, x_ref, g_ref, w_ref, vec_ref, cos_ref, sin_ref, bd_ref, wcat_ref, bsx_ref,
                   o_ref, d4_ref, d16_ref, a_ref)


def _proj_tile(t, x_ref, g_ref, w_ref, vec_ref, cos_ref, sin_ref, bd_ref, wcat_ref, bsx_ref,
               o_ref, d4_ref, d16_ref, a_ref):
    tm = KIN_TM
    r0 = t * tm
    x = x_ref[r0:r0 + tm, :]
    hn = _rms(x, g_ref[...]).astype(BF16)
    bd = bd_ref[...]
    slots = A_WIDTH // LANES

    def proj(sec):
        return jnp.dot(hn, w_ref[:, sec * GROUP_WIDTH:(sec + 1) * GROUP_WIDTH],
                       preferred_element_type=F32)

    def store(sec, val):
        o_ref[r0:r0 + tm, sec * GROUP_WIDTH:(sec + 1) * GROUP_WIDTH] = val.astype(BF16)

    def store_a(sec, val):
        store(sec, val)
        for half in range(GROUP_WIDTH // LANES):
            slot = sec * (GROUP_WIDTH // LANES) + half
            a_ref[t * slots + slot] = val[:, half * LANES:(half + 1) * LANES]
            cols = slice(slot * LANES, (slot + 1) * LANES)
            for dil, ref in ((4, d4_ref), (16, d16_ref)):
                for r in range(dil):
                    piece = a_ref[t * slots + slot, pl.ds(r, tm // dil, stride=dil), :]
                    ref[r, t * (tm // dil):(t + 1) * (tm // dil), cols] = piece.astype(BF16)

    def head_rms(t, row):
        width = t.shape[1]
        sq = t * t
        hi = sq.astype(BF16)
        lo = (sq - hi.astype(F32)).astype(BF16)
        blk = bd[0:width, 0:width]
        ms = (jnp.dot(hi, blk, preferred_element_type=F32) + jnp.dot(lo, blk, preferred_element_type=F32))
        return t * lax.rsqrt(ms + EPS) * vec_ref[row:row + 1, 0:width]

    def rope(t):
        blocks = []
        for c0 in range(0, t.shape[1], LANES):
            th = t[:, c0:c0 + LANES]
            lane = lax.broadcasted_iota(jnp.int32, th.shape, 1)
            partner = jnp.where((lane & 31) < 16, pltpu.roll(th, LANES - 16, axis=1), pltpu.roll(th, 16, axis=1))
            blocks.append(th * cos_ref[r0:r0 + tm, c0:c0 + LANES] + partner * sin_ref[r0:r0 + tm, c0:c0 + LANES])
        return blocks[0] if len(blocks) == 1 else jnp.concatenate(blocks, axis=-1)

    def rep_kv(t):
        swapped = pltpu.roll(t, HEAD_DIM, axis=1)
        low = lax.broadcasted_iota(jnp.int32, t.shape, 1) < HEAD_DIM
        return jnp.concatenate([jnp.where(low, t, swapped), jnp.where(low, swapped, t)], axis=-1)

    store_a(SEC_AQ, head_rms(proj(0), 0))
    store_a(SEC_AK, head_rms(proj(1), 1))
    store_a(SEC_AV, proj(2))
    store(SEC_BQ, head_rms(proj(3), 2))
    b_kv = proj(4)
    store(SEC_BK, rep_kv(head_rms(b_kv[:, 0:LANES], 3)))
    store(SEC_BV, rep_kv(b_kv[:, LANES:]))
    store(SEC_DQ, rope(head_rms(proj(7), 4)))
    d_kv = proj(8)
    store(SEC_DK, rep_kv(rope(head_rms(d_kv[:, 0:LANES], 5))))
    store(SEC_DV, jnp.concatenate([d_kv[:, LANES:]] * 2, axis=-1))

    u = jax.nn.gelu(proj(5))
    v = jax.nn.gelu(proj(6))
    mu = jnp.mean(v, axis=-1, keepdims=True)
    vc = v - mu
    var = jnp.mean(vc * vc, axis=-1, keepdims=True)
    vn = (vc * lax.rsqrt(var + EPS) * vec_ref[6:7, :] + vec_ref[7:8, :]).astype(BF16)
    gid = _head_id((C_CHUNK, GROUP_WIDTH))
    zero = jnp.zeros((C_CHUNK, GROUP_WIDTH), BF16)
    for c in range(tm // C_CHUNK):
        rows = slice(c * C_CHUNK, (c + 1) * C_CHUNK)
        vchunk = vn[rows]
        stacked = jnp.concatenate([jnp.where(gid == g, vchunk, zero) for g in range(C_GROUPS)], axis=0)
        mixed = jnp.dot(wcat_ref[...], stacked, preferred_element_type=F32) + bsx_ref[...]
        yc = _rms(u[rows] * mixed, vec_ref[8:9, :])
        o_ref[r0 + c * C_CHUNK:r0 + (c + 1) * C_CHUNK, SEC_YC * GROUP_WIDTH:(SEC_YC + 1) * GROUP_WIDTH] = yc.astype(BF16)


def _proj_call(x2d, g, w, vec, cos_t, sin_t, bd, wcat, bsx, batch, seq):
    n = x2d.shape[0]
    tps = seq // KIN_ROWS
    return pl.pallas_call(
        _proj_kernel,
        grid=(n // KIN_ROWS,),
        in_specs=[
            pl.BlockSpec((KIN_ROWS, D_MODEL), lambda i: (i, 0)),
            _resident((1, D_MODEL)),
            _resident(w.shape),
            _resident(vec.shape),
            pl.BlockSpec((KIN_ROWS, GROUP_WIDTH), lambda i: (i % tps, 0)),
            pl.BlockSpec((KIN_ROWS, GROUP_WIDTH), lambda i: (i % tps, 0)),
            _resident(bd.shape),
            _resident(wcat.shape),
            _resident(bsx.shape),
        ],
        out_specs=[
            pl.BlockSpec((KIN_ROWS, N_SEC * GROUP_WIDTH), lambda i: (i, 0)),
            pl.BlockSpec((None, 4, KIN_ROWS // 4, A_WIDTH), lambda i: (i // tps, 0, i % tps, 0)),
            pl.BlockSpec((None, 16, KIN_ROWS // 16, A_WIDTH), lambda i: (i // tps, 0, i % tps, 0)),
        ],
        out_shape=[
            jax.ShapeDtypeStruct((n, N_SEC * GROUP_WIDTH), BF16),
            jax.ShapeDtypeStruct((batch, 4, seq // 4, A_WIDTH), BF16),
            jax.ShapeDtypeStruct((batch, 16, seq // 16, A_WIDTH), BF16),
        ],
        scratch_shapes=[pltpu.VMEM((KIN_ROWS // KIN_TM * (A_WIDTH // LANES), KIN_TM, LANES), F32)],
        compiler_params=_params(("parallel",)),
        name="proj",
    )(x2d, g, w, vec, cos_t, sin_t, bd, wcat, bsx)


def _attend(q, kwin, vwin, bias_ref, case, sink_ref=None):
    tq = q.shape[0]
    hid = _head_id((tq, GROUP_WIDTH))
    qzero = jnp.zeros_like(q)
    q4 = jnp.concatenate([jnp.where(hid == h, q, qzero) for h in range(N_HEADS)], axis=0)
    s = lax.dot_general(q4, kwin, (((1,), (1,)), ((), ())), preferred_element_type=F32)
    s = s + bias_ref[case]
    if sink_ref is None:
        m = jnp.max(s, axis=-1, keepdims=True)
        p = jnp.exp2(s - m)
        den = jnp.sum(p, axis=-1, keepdims=True)
    else:
        sink = sink_ref[...]
        blocks = [s[:, c:c + LANES] for c in range(0, s.shape[1], LANES)]
        m = jnp.max(functools.reduce(jnp.maximum, blocks + [sink]), axis=-1, keepdims=True)
        p = jnp.exp2(s - m)
        pblocks = [p[:, c:c + LANES] for c in range(0, s.shape[1], LANES)]
        den = jnp.sum(functools.reduce(jnp.add, pblocks + [jnp.exp2(sink - m)]), axis=-1, keepdims=True)
    o4 = jnp.dot(p.astype(BF16), vwin, preferred_element_type=F32) * (1.0 / den)
    l4 = jnp.broadcast_to(m + jnp.log2(den), o4.shape)
    out = o4[0:tq]
    lse = l4[0:tq]
    for h in range(1, N_HEADS):
        out = jnp.where(hid == h, o4[h * tq:(h + 1) * tq], out)
        lse = jnp.where(hid == h, l4[h * tq:(h + 1) * tq], lse)
    return out, lse


def _band_window(j, tq, radius, length, kw):
    n_tiles = length // tq
    if n_tiles == 1:
        return 0, 0
    ks = pl.multiple_of(jnp.clip(j * tq - radius, 0, length - kw), 64)
    case = jnp.where(j == 0, 0, jnp.where(j == n_tiles - 1, 2, 1))
    return ks, case


def _dilated_kernel(q_ref, k_ref, v_ref, d4_ref, d16_ref, b1_ref, b4_ref, b16_ref, gain_ref, o_ref,
                    acc_ref, lse_ref, *, radii):
    seq = q_ref.shape[0]
    tq = BAND_TQ

    kw1 = min(tq + 2 * radii[0], seq)

    def body1(j, carry):
        q0 = pl.multiple_of(j * tq, tq)
        ks, case = _band_window(j, tq, radii[0], seq, kw1)
        o, l = _attend(q_ref[pl.ds(q0, tq), :], k_ref[pl.ds(ks, kw1), :], v_ref[pl.ds(ks, kw1), :], b1_ref, case)
        for half in range(GROUP_WIDTH // LANES):
            cols = slice(half * LANES, (half + 1) * LANES)
            acc_ref[half, pl.ds(q0, tq), :] = o[:, cols]
            lse_ref[half, pl.ds(q0, tq), :] = l[:, cols]
        return carry

    lax.fori_loop(0, seq // tq, body1, 0, unroll=BAND_UNROLL)

    for dil, ref, bias_ref, radius in ((4, d4_ref, b4_ref, radii[1]), (16, d16_ref, b16_ref, radii[2])):
        length = seq // dil
        tiles = length // tq
        kw = min(tq + 2 * radius, length)

        def body(it, carry, dil=dil, ref=ref, bias_ref=bias_ref, radius=radius, length=length, tiles=tiles, kw=kw):
            r = it // tiles
            j = it % tiles
            q0 = pl.multiple_of(j * tq, tq)
            ks, case = _band_window(j, tq, radius, length, kw)
            o, l = _attend(ref[r, pl.ds(q0, tq), 0:GROUP_WIDTH],
                           ref[r, pl.ds(ks, kw), GROUP_WIDTH:2 * GROUP_WIDTH],
                           ref[r, pl.ds(ks, kw), 2 * GROUP_WIDTH:3 * GROUP_WIDTH], bias_ref, case)
            rows = pl.ds(r + dil * q0, tq, stride=dil)
            for half in range(GROUP_WIDTH // LANES):
                cols = slice(half * LANES, (half + 1) * LANES)
                l_new = l[:, cols]
                l_old = lse_ref[half, rows, :]
                m = jnp.maximum(l_old, l_new)
                wa = jnp.exp2(l_old - m)
                wb = jnp.exp2(l_new - m)
                tot = wa + wb
                acc_ref[half, rows, :] = (wa * acc_ref[half, rows, :] + wb * o[:, cols]) * (1.0 / tot)
                lse_ref[half, rows, :] = m + jnp.log2(tot)
            return carry

        lax.fori_loop(0, dil * tiles, body, 0, unroll=BAND_UNROLL)

    acc = jnp.concatenate([acc_ref[0], acc_ref[1]], axis=-1)
    o_ref[...] = _rms(acc, gain_ref[...]).astype(BF16)


def _dilated_call(proj3d, d4, d16, biases, gain):
    b, seq, _ = proj3d.shape
    radii = tuple(window // (2 * dil) for window, dil in DILATED_CFGS)
    kern = functools.partial(_dilated_kernel, radii=radii)
    return pl.pallas_call(
        kern,
        grid=(b,),
        in_specs=[
            pl.BlockSpec((None, seq, GROUP_WIDTH), lambda bi: (bi, 0, SEC_AQ)),
            pl.BlockSpec((None, seq, GROUP_WIDTH), lambda bi: (bi, 0, SEC_AK)),
            pl.BlockSpec((None, seq, GROUP_WIDTH), lambda bi: (bi, 0, SEC_AV)),
            pl.BlockSpec((None,) + d4.shape[1:], lambda bi: (bi, 0, 0, 0)),
            pl.BlockSpec((None,) + d16.shape[1:], lambda bi: (bi, 0, 0, 0)),
            _resident(biases[0].shape), _resident(biases[1].shape), _resident(biases[2].shape),
            _resident((1, GROUP_WIDTH)),
        ],
        out_specs=pl.BlockSpec((None, seq, GROUP_WIDTH), lambda bi: (bi, 0, 0)),
        out_shape=jax.ShapeDtypeStruct((b, seq, GROUP_WIDTH), BF16),
        scratch_shapes=[pltpu.VMEM((GROUP_WIDTH // LANES, seq, LANES), F32)] * 2,
        compiler_params=_params(("parallel",)),
        name="dilated",
    )(proj3d, proj3d, proj3d, d4, d16, biases[0], biases[1], biases[2], gain)


def _swa_kernel(q_ref, k_ref, v_ref, bias_ref, sink_ref, gain_ref, o_ref):
    seq = q_ref.shape[0]
    tq = BAND_TQ
    kw = min(tq + 2 * SWA_RADIUS, seq)

    def body(j, carry):
        q0 = pl.multiple_of(j * tq, tq)
        ks, case = _band_window(j, tq, SWA_RADIUS, seq, kw)
        o, _ = _attend(q_ref[pl.ds(q0, tq), :], k_ref[pl.ds(ks, kw), :], v_ref[pl.ds(ks, kw), :],
                       bias_ref, case, sink_ref)
        o_ref[pl.ds(q0, tq), :] = _rms(o, gain_ref[...]).astype(BF16)
        return carry

    lax.fori_loop(0, seq // tq, body, 0, unroll=SWA_UNROLL)


def _swa_call(proj3d, bias, sink, gain):
    b, seq, _ = proj3d.shape
    return pl.pallas_call(
        _swa_kernel,
        grid=(b,),
        in_specs=[
            pl.BlockSpec((None, seq, GROUP_WIDTH), lambda bi: (bi, 0, SEC_BQ)),
            pl.BlockSpec((None, seq, GROUP_WIDTH), lambda bi: (bi, 0, SEC_BK)),
            pl.BlockSpec((None, seq, GROUP_WIDTH), lambda bi: (bi, 0, SEC_BV)),
            _resident(bias.shape),
            _resident(sink.shape),
            _resident((1, GROUP_WIDTH)),
        ],
        out_specs=pl.BlockSpec((None, seq, GROUP_WIDTH), lambda bi: (bi, 0, 0)),
        out_shape=jax.ShapeDtypeStruct((b, seq, GROUP_WIDTH), BF16),
        compiler_params=_params(("parallel",)),
        name="swa",
    )(proj3d, proj3d, proj3d, bias, sink, gain)


def _dense_kernel(q_ref, k_ref, v_ref, gain_ref, o_ref):
    k = k_ref[...]
    vext = jnp.concatenate([v_ref[...], jnp.ones((k.shape[0], LANES), BF16)], axis=-1)
    tq = DENSE_TQ
    hid = _head_id((tq, GROUP_WIDTH))
    low = lax.broadcasted_iota(jnp.int32, (tq, LANES), 1) < HEAD_DIM
    for t in range(q_ref.shape[0] // tq):
        q = q_ref[t * tq:(t + 1) * tq, :]
        qzero = jnp.zeros_like(q)
        heads = []
        for h in range(N_HEADS):
            qh = jnp.where(hid == h, q, qzero)
            s = lax.dot_general(qh, k, (((1,), (1,)), ((), ())), preferred_element_type=F32)
            m = jnp.max(s, axis=-1, keepdims=True)
            p = jnp.exp2((s - m).astype(BF16))
            o = jnp.dot(p, vext, preferred_element_type=F32)
            heads.append(o[:, 0:LANES] * (1.0 / o[:, LANES:]))
        blk0 = jnp.where(low, heads[0], pltpu.roll(heads[1], HEAD_DIM, axis=1))
        blk1 = jnp.where(low, pltpu.roll(heads[2], HEAD_DIM, axis=1), heads[3])
        acc = jnp.concatenate([blk0, blk1], axis=-1)
        o_ref[t * tq:(t + 1) * tq, :] = _rms(acc, gain_ref[...]).astype(BF16)


def _dense_call(proj3d, gain):
    b, seq, _ = proj3d.shape
    return pl.pallas_call(
        _dense_kernel,
        grid=(b, seq // DENSE_ROWS),
        in_specs=[
            pl.BlockSpec((None, DENSE_ROWS, GROUP_WIDTH), lambda bi, j: (bi, j, SEC_DQ)),
            pl.BlockSpec((None, seq, GROUP_WIDTH), lambda bi, j: (bi, 0, SEC_DK)),
            pl.BlockSpec((None, seq, LANES), lambda bi, j: (bi, 0, SEC_DV * (GROUP_WIDTH // LANES))),
            _resident((1, GROUP_WIDTH)),
        ],
        out_specs=pl.BlockSpec((None, DENSE_ROWS, GROUP_WIDTH), lambda bi, j: (bi, j, 0)),
        out_shape=jax.ShapeDtypeStruct((b, seq, GROUP_WIDTH), BF16),
        compiler_params=_params(("parallel", "arbitrary")),
        name="dense",
    )(proj3d, proj3d, proj3d, gain)


def _mix_kernel(x_ref, ya_ref, yb_ref, yc_ref, yd_ref, w_ref, g_ref, x1_ref, xn_ref, xs_ref):
    mixed = jnp.concatenate([ya_ref[...], yb_ref[...], yc_ref[...], yd_ref[...]], axis=-1)
    x1 = x_ref[...] + jnp.dot(mixed, w_ref[...], preferred_element_type=F32)
    x1_ref[...] = x1
    xn = _rms(x1, g_ref[...])
    tm = xn.shape[0]
    for c in range(D_MODEL // LANES):
        xs_ref[c] = xn[:, c * LANES:(c + 1) * LANES]
        col = jnp.concatenate([xs_ref[c, pl.ds(b, tm // STRANDS, stride=STRANDS), :] for b in range(STRANDS)],
                              axis=0)
        xn_ref[:, c * LANES:(c + 1) * LANES] = col.astype(BF16)


def _mix_call(x2d, ya, yb, proj2d, yd, w_out, g_ffn):
    n = x2d.shape[0]
    row = lambda i: (i, 0)
    grp = pl.BlockSpec((MIX_TM, GROUP_WIDTH), row)
    return pl.pallas_call(
        _mix_kernel,
        grid=(n // MIX_TM,),
        in_specs=[pl.BlockSpec((MIX_TM, D_MODEL), row), grp, grp,
                  pl.BlockSpec((MIX_TM, GROUP_WIDTH), lambda i: (i, SEC_YC)), grp,
                  _resident(w_out.shape), _resident((1, D_MODEL))],
        out_specs=[pl.BlockSpec((MIX_TM, D_MODEL), row), pl.BlockSpec((MIX_TM, D_MODEL), row)],
        out_shape=[jax.ShapeDtypeStruct((n, D_MODEL), F32), jax.ShapeDtypeStruct((n, D_MODEL), BF16)],
        scratch_shapes=[pltpu.VMEM((D_MODEL // LANES, MIX_TM, LANES), F32)],
        compiler_params=_params(("parallel",)),
        name="mix_out",
    )(x2d, ya, yb, proj2d, yd, w_out, g_ffn)


def _ffn_kernel(xn_ref, prev_ref, next_ref, x1_ref, p_ref, wup_ref, cw_ref, wd_ref, gple_ref, wgate_ref,
                wproj_ref, o_ref, xcat_ref, h0_ref, h1_ref, acc_ref, tok_ref, *, tiles_per_seq, n_tiles):
    step = pl.program_id(0)
    tm = xn_ref.shape[0]
    rows = tm + HALO
    sl = tm // STRANDS
    pitch = sl + SUB

    def up(k, h_ref):
        xcat = xcat_ref[...]
        for part in range(2):
            col = (k + part * N_CHUNKS) * FFN_TF
            if not isinstance(col, int):
                col = pl.multiple_of(col, FFN_TF)
            h = jnp.dot(xcat, wup_ref[:, pl.ds(col, FFN_TF)], preferred_element_type=F32)
            for b in range(STRANDS):
                h_ref[part, SUB + b * pitch:SUB + b * pitch + sl, :] = h[b * sl:(b + 1) * sl]
            h_ref[part, SUB + sl:SUB + pitch, :] = h[tm:tm + SUB]
            h_ref[part, (STRANDS - 1) * pitch:SUB + (STRANDS - 1) * pitch, :] = h[rows - SUB:rows]

    def strand(h_ref, part, b, shift=0):
        start = SUB + b * pitch + shift
        return h_ref[part, start:start + sl, :]

    def conv(h_ref, part, cw):
        out = []
        for b in range(STRANDS):
            before = strand(h_ref, part, b - 1) if b > 0 else strand(h_ref, part, STRANDS - 1, -1)
            after = strand(h_ref, part, b + 1) if b < STRANDS - 1 else strand(h_ref, part, 0, 1)
            out.append(before * cw[0:1, :] + strand(h_ref, part, b) * cw[1:2, :] + after * cw[2:3, :] + cw[3:4, :])
        return jnp.concatenate(out, axis=0)

    def down(k, h_ref):
        gt = conv(h_ref, 0, cw_ref[k])
        uu = conv(h_ref, 1, cw_ref[k + N_CHUNKS])
        act = (gt / (1.0 + jnp.exp(-gt)) * uu).astype(BF16)
        acc_ref[...] += jnp.dot(act, wd_ref[k], preferred_element_type=F32)

    def stage():
        pos = jnp.minimum(step, n_tiles - 1) % tiles_per_seq
        prev_blk = jnp.where(pos == 0, jnp.zeros_like(prev_ref[...]), prev_ref[...])
        next_blk = jnp.where(pos == tiles_per_seq - 1, jnp.zeros_like(next_ref[...]), next_ref[...])
        halo_row = lax.broadcasted_iota(jnp.int32, (HALO, D_MODEL), 0)
        xcat_ref[0:tm, :] = xn_ref[...]
        xcat_ref[tm:rows, :] = jnp.where(halo_row == 0, next_blk, prev_blk)
        up(0, h0_ref)

    @pl.when(step == 0)
    def _():
        stage()

    @pl.when(step > 0)
    def _():
        acc_ref[...] = jnp.zeros_like(acc_ref)

        def pair(t, carry):
            k = 2 * t
            up(k + 1, h1_ref)
            down(k, h0_ref)
            up(k + 2, h0_ref)
            down(k + 1, h1_ref)
            return carry

        lax.fori_loop(0, (N_CHUNKS - 1) // 2, pair, 0, unroll=FFN_UNROLL)
        down(N_CHUNKS - 1, h0_ref)

        stage()

        for c in range(D_MODEL // LANES):
            for b in range(STRANDS):
                tok_ref[c, pl.ds(b, sl, stride=STRANDS), :] = acc_ref[b * sl:(b + 1) * sl, c * LANES:(c + 1) * LANES]
        ffn = jnp.concatenate([tok_ref[c] for c in range(D_MODEL // LANES)], axis=-1)
        x2 = x1_ref[...] + ffn
        z = jnp.dot(_rms(x2, gple_ref[...]).astype(BF16), wgate_ref[...], preferred_element_type=F32)
        gate = 1.0 / (1.0 + jnp.exp(-z))
        inj = jnp.dot(p_ref[...].astype(BF16), wproj_ref[...], preferred_element_type=F32)
        o_ref[...] = x2 + inj * gate


def _ffn_call(xn, x1, p2d, wup, cw, wd, g_ple, w_gate, w_proj, seq):
    assert N_CHUNKS % 2 == 1
    assert MIX_TM == FFN_TM
    n = xn.shape[0]
    n_tiles = n // FFN_TM
    halo_per_tile = FFN_TM // HALO
    n_halo = n // HALO
    staged = lambda s: jnp.minimum(s, n_tiles - 1)
    done = lambda s: (jnp.maximum(s - 1, 0), 0)
    kern = functools.partial(_ffn_kernel, tiles_per_seq=seq // FFN_TM, n_tiles=n_tiles)
    return pl.pallas_call(
        kern,
        grid=(n_tiles + 1,),
        in_specs=[
            pl.BlockSpec((FFN_TM, D_MODEL), lambda s: (staged(s), 0)),
            pl.BlockSpec((HALO, D_MODEL), lambda s: (jnp.maximum(staged(s) * halo_per_tile - 1, 0), 0)),
            pl.BlockSpec((HALO, D_MODEL),
                         lambda s: (jnp.minimum((staged(s) + 1) * halo_per_tile, n_halo - 1), 0)),
            pl.BlockSpec((FFN_TM, D_MODEL), done),
            pl.BlockSpec((FFN_TM, PLE_DIM), done),
            _resident(wup.shape), _resident(cw.shape), _resident(wd.shape),
            _resident((1, D_MODEL)), _resident(w_gate.shape), _resident(w_proj.shape),
        ],
        out_specs=pl.BlockSpec((FFN_TM, D_MODEL), done),
        out_shape=jax.ShapeDtypeStruct((n, D_MODEL), F32),
        scratch_shapes=[pltpu.VMEM((FFN_TM + HALO, D_MODEL), BF16),
                        pltpu.VMEM((2, SUB + STRANDS * (FFN_TM // STRANDS + SUB), FFN_TF), F32),
                        pltpu.VMEM((2, SUB + STRANDS * (FFN_TM // STRANDS + SUB), FFN_TF), F32),
                        pltpu.VMEM((FFN_TM, D_MODEL), F32),
                        pltpu.VMEM((D_MODEL // LANES, FFN_TM, LANES), F32)],
        compiler_params=_params(("arbitrary",)),
        name="ffn_ple",
    )(xn, xn, xn, x1, p2d, wup, cw, wd, g_ple, w_gate, w_proj)


def _t5_bucket(rel):
    nb = REL_BUCKETS // 2
    ret = jnp.where(rel > 0, nb, 0)
    n = jnp.abs(rel)
    max_exact = nb // 2
    nf = jnp.maximum(n, 1).astype(F32)
    large = max_exact + (jnp.log(nf / max_exact) / math.log(REL_MAX_DIST / max_exact)
                         * (nb - max_exact)).astype(jnp.int32)
    large = jnp.minimum(large, nb - 1)
    return ret + jnp.where(n < max_exact, n, large)


def _band_bias(table, radius, dil, tq, kw):
    cases = []
    for off in (0, radius, kw - tq):
        rel = jnp.arange(kw)[None, :] - jnp.arange(tq)[:, None] - off
        onehot = (_t5_bucket(rel * dil)[..., None] == jnp.arange(REL_BUCKETS)).astype(F32)
        bias = jnp.einsum('qkn,nh->hqk', onehot, table.astype(F32), precision=lax.Precision.HIGHEST)
        bias = jnp.where((jnp.abs(rel) <= radius)[None], bias, NEG_INF)
        cases.append(bias.reshape(N_HEADS * tq, kw))
    return jnp.stack(cases, axis=0)


def _rope_tables(seq):
    half = HEAD_DIM // 2
    inv = ROPE_THETA ** (-jnp.arange(0, half, 2, dtype=F32) / half)
    t = jnp.arange(seq)
    ang_r = (t // GRID_W).astype(F32)[:, None] * inv[None, :]
    ang_c = (t % GRID_W).astype(F32)[:, None] * inv[None, :]
    cos_h = jnp.concatenate([jnp.cos(ang_r)] * 2 + [jnp.cos(ang_c)] * 2, axis=-1)
    sin_h = jnp.concatenate([-jnp.sin(ang_r), jnp.sin(ang_r), -jnp.sin(ang_c), jnp.sin(ang_c)], axis=-1)
    return jnp.tile(cos_h, (1, N_HEADS)), jnp.tile(sin_h, (1, N_HEADS))


def kernel(x, p, rel_bias, ln_mix_g, w_in, qk_gain, sink, c_norm_g, c_norm_b, c_ws, c_bs, out_gain, w_out,
           ln_ffn_g, w_up, conv_w, conv_b, w_down, ln_ple_g, w_ple_gate, w_ple_proj):
    b, s, _ = x.shape
    n = b * s
    depth = w_in.shape[0]
    cos_t, sin_t = _rope_tables(s)
    bd = (np.arange(GROUP_WIDTH)[:, None] // HEAD_DIM == np.arange(GROUP_WIDTH)[None, :] // HEAD_DIM)
    bd = jnp.asarray(bd / HEAD_DIM, BF16)
    table_a = rel_bias[:, :N_HEADS] * LOG2E
    table_b = rel_bias[:, N_HEADS:] * LOG2E
    q_scale = ATTN_SCALE * LOG2E
    bias_a = []
    for window, dil in DILATED_CFGS:
        radius = window // (2 * dil)
        length = s // dil
        tq = min(BAND_TQ, length)
        bias_a.append(_band_bias(table_a, radius, dil, tq, min(tq + 2 * radius, length)))
    tq_b = min(BAND_TQ, s)
    bias_b = _band_bias(table_b, SWA_RADIUS, 1, tq_b, min(tq_b + 2 * SWA_RADIUS, s))

    x2d = x.reshape(n, D_MODEL)
    for i in range(depth):
        w = w_in[i].astype(BF16)
        qg = qk_gain[i]
        vec = jnp.stack([
            jnp.tile(qg[0, 0], N_HEADS) * q_scale, jnp.tile(qg[0, 1], N_HEADS),
            jnp.tile(qg[1, 0], N_HEADS) * q_scale, jnp.tile(qg[1, 1], N_HEADS),
            jnp.tile(qg[2, 0], N_HEADS) * q_scale, jnp.tile(qg[2, 1], N_HEADS),
            c_norm_g[i], c_norm_b[i], out_gain[i, 2],
        ] + [jnp.zeros((GROUP_WIDTH,), F32)] * 7, axis=0)
        wcat = jnp.concatenate([c_ws[i, g] for g in range(C_GROUPS)], axis=1).astype(BF16)
        bsx = jnp.repeat(jnp.transpose(c_bs[i]), GROUP_WIDTH // C_GROUPS, axis=1)
        proj2d, d4, d16 = _proj_call(x2d, ln_mix_g[i][None], w, vec, cos_t, sin_t, bd, wcat, bsx, b, s)

        proj3d = proj2d.reshape(b, s, N_SEC * GROUP_WIDTH)
        ya = _dilated_call(proj3d, d4, d16, bias_a, out_gain[i, 0][None])
        sink_blk = jnp.full((N_HEADS * tq_b, LANES), NEG_INF, F32).at[:, 0].set(jnp.repeat(sink[i] * LOG2E, tq_b))
        yb = _swa_call(proj3d, bias_b, sink_blk, out_gain[i, 1][None])
        yd = _dense_call(proj3d, out_gain[i, 3][None])

        x1, xn = _mix_call(x2d, ya.reshape(n, GROUP_WIDTH), yb.reshape(n, GROUP_WIDTH), proj2d,
                           yd.reshape(n, GROUP_WIDTH), w_out[i].astype(BF16), ln_ffn_g[i][None])
        wup = w_up[i].astype(BF16)
        cw = jnp.concatenate([conv_w[i], conv_b[i][None]], axis=0)
        cw = jnp.transpose(cw.reshape(4, 2 * N_CHUNKS, FFN_TF), (1, 0, 2))
        wd = w_down[i].astype(BF16).reshape(N_CHUNKS, FFN_TF, D_MODEL)
        x2d = _ffn_call(xn, x1, p[i].reshape(n, PLE_DIM), wup, cw, wd, ln_ple_g[i][None],
                        w_ple_gate[i].astype(BF16), w_ple_proj[i].astype(BF16), s)
    return x2d.reshape(b, s, D_MODEL)
```

```python
import functools
import math

import jax
import jax.numpy as jnp
import numpy as np
from jax import lax
from jax.experimental import pallas as pl
from jax.experimental.pallas import tpu as pltpu

D_MODEL = 1024
HEAD_DIM = 64
GROUP_WIDTH = 256
N_HEADS = GROUP_WIDTH // HEAD_DIM
C_GROUPS = 4
C_CHUNK = 128
DILATED_CFGS = ((128, 1), (512, 4), (2048, 16))
SWA_RADIUS = 128
GRID_W = 64
ROPE_THETA = 10000.0
REL_BUCKETS = 32
REL_MAX_DIST = 1024
D_FF = 2816
PLE_DIM = 256
EPS = 1e-6
NEG_INF = -1e30
ATTN_SCALE = HEAD_DIM ** -0.5
LOG2E = math.log2(math.e)

BF16 = jnp.bfloat16
F32 = jnp.float32

SEC_AQ, SEC_AK, SEC_AV, SEC_BQ, SEC_BK, SEC_BV, SEC_YC, SEC_DQ, SEC_DK, SEC_DV = range(10)
N_SEC = 10
A_WIDTH = 3 * GROUP_WIDTH

VMEM_LIMIT = 56 * 1024 * 1024

KIN_TM = 512
KIN_ROWS = 512
BAND_TQ = 128
BAND_UNROLL = 16
SWA_UNROLL = 8
DENSE_TQ = 512
DENSE_ROWS = 1024
MIX_TM = 512
FFN_TM = 512
FFN_TF = 256
LANES = 128
SUB = 8
STRANDS = 8
HALO = 16
N_CHUNKS = D_FF // FFN_TF
FFN_UNROLL = True


def _params(sem):
    return pltpu.CompilerParams(dimension_semantics=sem, vmem_limit_bytes=VMEM_LIMIT)


def _resident(shape):
    zeros = (0,) * len(shape)
    return pl.BlockSpec(shape, lambda *_: zeros, pipeline_mode=pl.Buffered(1))


def _rms(t, gain):
    return t * lax.rsqrt(jnp.mean(t * t, axis=-1, keepdims=True) + EPS) * gain


def _head_id(shape):
    return lax.broadcasted_iota(jnp.int32, shape, 1) >> 6


def _proj_kernel(x_ref, g_ref, w_ref, vec_ref, cos_ref, sin_ref, bd_ref, wcat_ref, bsx_ref,
                 o_ref, d4_ref, d16_ref, a_ref):
    for t in range(x_ref.shape[0] // KIN_TM):
        _proj_tile(t, x_ref, g_ref, w_ref, vec_ref, cos_ref, sin_ref, bd_ref, wcat_ref, bsx_ref,
                   o_ref, d4_ref, d16_ref, a_ref)


def _proj_tile(t, x_ref, g_ref, w_ref, vec_ref, cos_ref, sin_ref, bd_ref, wcat_ref, bsx_ref,
               o_ref, d4_ref, d16_ref, a_ref):
    tm = KIN_TM
    r0 = t * tm
    x = x_ref[r0:r0 + tm, :]
    hn = _rms(x, g_ref[...]).astype(BF16)
    bd = bd_ref[...]
    slots = A_WIDTH // LANES

    def proj(sec):
        return jnp.dot(hn, w_ref[:, sec * GROUP_WIDTH:(sec + 1) * GROUP_WIDTH],
                       preferred_element_type=F32)

    def store(sec, val):
        o_ref[r0:r0 + tm, sec * GROUP_WIDTH:(sec + 1) * GROUP_WIDTH] = val.astype(BF16)

    def store_a(sec, val):
        store(sec, val)
        for half in range(GROUP_WIDTH // LANES):
            slot = sec * (GROUP_WIDTH // LANES) + half
            a_ref[t * slots + slot] = val[:, half * LANES:(half + 1) * LANES]
            cols = slice(slot * LANES, (slot + 1) * LANES)
            for dil, ref in ((4, d4_ref), (16, d16_ref)):
                for r in range(dil):
                    piece = a_ref[t * slots + slot, pl.ds(r, tm // dil, stride=dil), :]
                    ref[r, t * (tm // dil):(t + 1) * (tm // dil), cols] = piece.astype(BF16)

    def head_rms(t, row):
        width = t.shape[1]
        sq = t * t
        hi = sq.astype(BF16)
        lo = (sq - hi.astype(F32)).astype(BF16)
        blk = bd[0:width, 0:width]
        ms = (jnp.dot(hi, blk, preferred_element_type=F32) + jnp.dot(lo, blk, preferred_element_type=F32))
        return t * lax.rsqrt(ms + EPS) * vec_ref[row:row + 1, 0:width]

    def rope(t):
        blocks = []
        for c0 in range(0, t.shape[1], LANES):
            th = t[:, c0:c0 + LANES]
            lane = lax.broadcasted_iota(jnp.int32, th.shape, 1)
            partner = jnp.where((lane & 31) < 16, pltpu.roll(th, LANES - 16, axis=1), pltpu.roll(th, 16, axis=1))
            blocks.append(th * cos_ref[r0:r0 + tm, c0:c0 + LANES] + partner * sin_ref[r0:r0 + tm, c0:c0 + LANES])
        return blocks[0] if len(blocks) == 1 else jnp.concatenate(blocks, axis=-1)

    def rep_kv(t):
        swapped = pltpu.roll(t, HEAD_DIM, axis=1)
        low = lax.broadcasted_iota(jnp.int32, t.shape, 1) < HEAD_DIM
        return jnp.concatenate([jnp.where(low, t, swapped), jnp.where(low, swapped, t)], axis=-1)

    store_a(SEC_AQ, head_rms(proj(0), 0))
    store_a(SEC_AK, head_rms(proj(1), 1))
    store_a(SEC_AV, proj(2))
    store(SEC_BQ, head_rms(proj(3), 2))
    b_kv = proj(4)
    store(SEC_BK, rep_kv(head_rms(b_kv[:, 0:LANES], 3)))
    store(SEC_BV, rep_kv(b_kv[:, LANES:]))
    store(SEC_DQ, rope(head_rms(proj(7), 4)))
    d_kv = proj(8)
    store(SEC_DK, rep_kv(rope(head_rms(d_kv[:, 0:LANES], 5))))
    store(SEC_DV, jnp.concatenate([d_kv[:, LANES:]] * 2, axis=-1))

    u = jax.nn.gelu(proj(5))
    v = jax.nn.gelu(proj(6))
    mu = jnp.mean(v, axis=-1, keepdims=True)
    vc = v - mu
    var = jnp.mean(vc * vc, axis=-1, keepdims=True)
    vn = (vc * lax.rsqrt(var + EPS) * vec_ref[6:7, :] + vec_ref[7:8, :]).astype(BF16)
    gid = _head_id((C_CHUNK, GROUP_WIDTH))
    zero = jnp.zeros((C_CHUNK, GROUP_WIDTH), BF16)
    for c in range(tm // C_CHUNK):
        rows = slice(c * C_CHUNK, (c + 1) * C_CHUNK)
        vchunk = vn[rows]
        stacked = jnp.concatenate([jnp.where(gid == g, vchunk, zero) for g in range(C_GROUPS)], axis=0)
        mixed = jnp.dot(wcat_ref[...], stacked, preferred_element_type=F32) + bsx_ref[...]
        yc = _rms(u[rows] * mixed, vec_ref[8:9, :])
        o_ref[r0 + c * C_CHUNK:r0 + (c + 1) * C_CHUNK, SEC_YC * GROUP_WIDTH:(SEC_YC + 1) * GROUP_WIDTH] = yc.astype(BF16)


def _proj_call(x2d, g, w, vec, cos_t, sin_t, bd, wcat, bsx, batch, seq):
    n = x2d.shape[0]
    tps = seq // KIN_ROWS
    return pl.pallas_call(
        _proj_kernel,
        grid=(n // KIN_ROWS,),
        in_specs=[
            pl.BlockSpec((KIN_ROWS, D_MODEL), lambda i: (i, 0)),
            _resident((1, D_MODEL)),
            _resident(w.shape),
            _resident(vec.shape),
            pl.BlockSpec((KIN_ROWS, GROUP_WIDTH), lambda i: (i % tps, 0)),
            pl.BlockSpec((KIN_ROWS, GROUP_WIDTH), lambda i: (i % tps, 0)),
            _resident(bd.shape),
            _resident(wcat.shape),
            _resident(bsx.shape),
        ],
        out_specs=[
            pl.BlockSpec((KIN_ROWS, N_SEC * GROUP_WIDTH), lambda i: (i, 0)),
            pl.BlockSpec((None, 4, KIN_ROWS // 4, A_WIDTH), lambda i: (i // tps, 0, i % tps, 0)),
            pl.BlockSpec((None, 16, KIN_ROWS // 16, A_WIDTH), lambda i: (i // tps, 0, i % tps, 0)),
        ],
        out_shape=[
            jax.ShapeDtypeStruct((n, N_SEC * GROUP_WIDTH), BF16),
            jax.ShapeDtypeStruct((batch, 4, seq // 4, A_WIDTH), BF16),
            jax.ShapeDtypeStruct((batch, 16, seq // 16, A_WIDTH), BF16),
        ],
        scratch_shapes=[pltpu.VMEM((KIN_ROWS // KIN_TM * (A_WIDTH // LANES), KIN_TM, LANES), F32)],
        compiler_params=_params(("parallel",)),
        name="proj",
    )(x2d, g, w, vec, cos_t, sin_t, bd, wcat, bsx)


def _attend(q, kwin, vwin, bias_ref, case, sink_ref=None):
    tq = q.shape[0]
    hid = _head_id((tq, GROUP_WIDTH))
    qzero = jnp.zeros_like(q)
    q4 = jnp.concatenate([jnp.where(hid == h, q, qzero) for h in range(N_HEADS)], axis=0)
    s = lax.dot_general(q4, kwin, (((1,), (1,)), ((), ())), preferred_element_type=F32)
    s = s + bias_ref[case]
    if sink_ref is None:
        m = jnp.max(s, axis=-1, keepdims=True)
        p = jnp.exp2(s - m)
        den = jnp.sum(p, axis=-1, keepdims=True)
    else:
        sink = sink_ref[...]
        blocks = [s[:, c:c + LANES] for c in range(0, s.shape[1], LANES)]
        m = jnp.max(functools.reduce(jnp.maximum, blocks + [sink]), axis=-1, keepdims=True)
        p = jnp.exp2(s - m)
        pblocks = [p[:, c:c + LANES] for c in range(0, s.shape[1], LANES)]
        den = jnp.sum(functools.reduce(jnp.add, pblocks + [jnp.exp2(sink - m)]), axis=-1, keepdims=True)
    o4 = jnp.dot(p.astype(BF16), vwin, preferred_element_type=F32) * (1.0 / den)
    l4 = jnp.broadcast_to(m + jnp.log2(den), o4.shape)
    out = o4[0:tq]
    lse = l4[0:tq]
    for h in range(1, N_HEADS):
        out = jnp.where(hid == h, o4[h * tq:(h + 1) * tq], out)
        lse = jnp.where(hid == h, l4[h * tq:(h + 1) * tq], lse)
    return out, lse


def _band_window(j, tq, radius, length, kw):
    n_tiles = length // tq
    if n_tiles == 1:
        return 0, 0
    ks = pl.multiple_of(jnp.clip(j * tq - radius, 0, length - kw), 64)
    case = jnp.where(j == 0, 0, jnp.where(j == n_tiles - 1, 2, 1))
    return ks, case


def _dilated_kernel(q_ref, k_ref, v_ref, d4_ref, d16_ref, b1_ref, b4_ref, b16_ref, gain_ref, o_ref,
                    acc_ref, lse_ref, *, radii):
    seq = q_ref.shape[0]
    tq = BAND_TQ

    kw1 = min(tq + 2 * radii[0], seq)

    def body1(j, carry):
        q0 = pl.multiple_of(j * tq, tq)
        ks, case = _band_window(j, tq, radii[0], seq, kw1)
        o, l = _attend(q_ref[pl.ds(q0, tq), :], k_ref[pl.ds(ks, kw1), :], v_ref[pl.ds(ks, kw1), :], b1_ref, case)
        for half in range(GROUP_WIDTH // LANES):
            cols = slice(half * LANES, (half + 1) * LANES)
            acc_ref[half, pl.ds(q0, tq), :] = o[:, cols]
            lse_ref[half, pl.ds(q0, tq), :] = l[:, cols]
        return carry

    lax.fori_loop(0, seq // tq, body1, 0, unroll=BAND_UNROLL)

    for dil, ref, bias_ref, radius in ((4, d4_ref, b4_ref, radii[1]), (16, d16_ref, b16_ref, radii[2])):
        length = seq // dil
        tiles = length // tq
        kw = min(tq + 2 * radius, length)

        def body(it, carry, dil=dil, ref=ref, bias_ref=bias_ref, radius=radius, length=length, tiles=tiles, kw=kw):
            r = it // tiles
            j = it % tiles
            q0 = pl.multiple_of(j * tq, tq)
            ks, case = _band_window(j, tq, radius, length, kw)
            o, l = _attend(ref[r, pl.ds(q0, tq), 0:GROUP_WIDTH],
                           ref[r, pl.ds(ks, kw), GROUP_WIDTH:2 * GROUP_WIDTH],
                           ref[r, pl.ds(ks, kw), 2 * GROUP_WIDTH:3 * GROUP_WIDTH], bias_ref, case)
            rows = pl.ds(r + dil * q0, tq, stride=dil)
            for half in range(GROUP_WIDTH // LANES):
                cols = slice(half * LANES, (half + 1) * LANES)
                l_new = l[:, cols]
                l_old = lse_ref[half, rows, :]
                m = jnp.maximum(l_old, l_new)
                wa = jnp.exp2(l_old - m)
                wb = jnp.exp2(l_new - m)
                tot = wa + wb
                acc_ref[half, rows, :] = (wa * acc_ref[half, rows, :] + wb * o[:, cols]) * (1.0 / tot)
                lse_ref[half, rows, :] = m + jnp.log2(tot)
            return carry

        lax.fori_loop(0, dil * tiles, body, 0, unroll=BAND_UNROLL)

    acc = jnp.concatenate([acc_ref[0], acc_ref[1]], axis=-1)
    o_ref[...] = _rms(acc, gain_ref[...]).astype(BF16)


def _dilated_call(proj3d, d4, d16, biases, gain):
    b, seq, _ = proj3d.shape
    radii = tuple(window // (2 * dil) for window, dil in DILATED_CFGS)
    kern = functools.partial(_dilated_kernel, radii=radii)
    return pl.pallas_call(
        kern,
        grid=(b,),
        in_specs=[
            pl.BlockSpec((None, seq, GROUP_WIDTH), lambda bi: (bi, 0, SEC_AQ)),
            pl.BlockSpec((None, seq, GROUP_WIDTH), lambda bi: (bi, 0, SEC_AK)),
            pl.BlockSpec((None, seq, GROUP_WIDTH), lambda bi: (bi, 0, SEC_AV)),
            pl.BlockSpec((None,) + d4.shape[1:], lambda bi: (bi, 0, 0, 0)),
            pl.BlockSpec((None,) + d16.shape[1:], lambda bi: (bi, 0, 0, 0)),
            _resident(biases[0].shape), _resident(biases[1].shape), _resident(biases[2].shape),
            _resident((1, GROUP_WIDTH)),
        ],
        out_specs=pl.BlockSpec((None, seq, GROUP_WIDTH), lambda bi: (bi, 0, 0)),
        out_shape=jax.ShapeDtypeStruct((b, seq, GROUP_WIDTH), BF16),
        scratch_shapes=[pltpu.VMEM((GROUP_WIDTH // LANES, seq, LANES), F32)] * 2,
        compiler_params=_params(("parallel",)),
        name="dilated",
    )(proj3d, proj3d, proj3d, d4, d16, biases[0], biases[1], biases[2], gain)


def _swa_kernel(q_ref, k_ref, v_ref, bias_ref, sink_ref, gain_ref, o_ref):
    seq = q_ref.shape[0]
    tq = BAND_TQ
    kw = min(tq + 2 * SWA_RADIUS, seq)

    def body(j, carry):
        q0 = pl.multiple_of(j * tq, tq)
        ks, case = _band_window(j, tq, SWA_RADIUS, seq, kw)
        o, _ = _attend(q_ref[pl.ds(q0, tq), :], k_ref[pl.ds(ks, kw), :], v_ref[pl.ds(ks, kw), :],
                       bias_ref, case, sink_ref)
        o_ref[pl.ds(q0, tq), :] = _rms(o, gain_ref[...]).astype(BF16)
        return carry

    lax.fori_loop(0, seq // tq, body, 0, unroll=SWA_UNROLL)


def _swa_call(proj3d, bias, sink, gain):
    b, seq, _ = proj3d.shape
    return pl.pallas_call(
        _swa_kernel,
        grid=(b,),
        in_specs=[
            pl.BlockSpec((None, seq, GROUP_WIDTH), lambda bi: (bi, 0, SEC_BQ)),
            pl.BlockSpec((None, seq, GROUP_WIDTH), lambda bi: (bi, 0, SEC_BK)),
            pl.BlockSpec((None, seq, GROUP_WIDTH), lambda bi: (bi, 0, SEC_BV)),
            _resident(bias.shape),
            _resident(sink.shape),
            _resident((1, GROUP_WIDTH)),
        ],
        out_specs=pl.BlockSpec((None, seq, GROUP_WIDTH), lambda bi: (bi, 0, 0)),
        out_shape=jax.ShapeDtypeStruct((b, seq, GROUP_WIDTH), BF16),
        compiler_params=_params(("parallel",)),
        name="swa",
    )(proj3d, proj3d, proj3d, bias, sink, gain)


def _dense_kernel(q_ref, k_ref, v_ref, gain_ref, o_ref):
    k = k_ref[...]
    vext = jnp.concatenate([v_ref[...], jnp.ones((k.shape[0], LANES), BF16)], axis=-1)
    tq = DENSE_TQ
    hid = _head_id((tq, GROUP_WIDTH))
    low = lax.broadcasted_iota(jnp.int32, (tq, LANES), 1) < HEAD_DIM
    for t in range(q_ref.shape[0] // tq):
        q = q_ref[t * tq:(t + 1) * tq, :]
        qzero = jnp.zeros_like(q)
        heads = []
        for h in range(N_HEADS):
            qh = jnp.where(hid == h, q, qzero)
            s = lax.dot_general(qh, k, (((1,), (1,)), ((), ())), preferred_element_type=F32)
            m = jnp.max(s, axis=-1, keepdims=True)
            p = jnp.exp2((s - m).astype(BF16))
            o = jnp.dot(p, vext, preferred_element_type=F32)
            heads.append(o[:, 0:LANES] * (1.0 / o[:, LANES:]))
        blk0 = jnp.where(low, heads[0], pltpu.roll(heads[1], HEAD_DIM, axis=1))
        blk1 = jnp.where(low, pltpu.roll(heads[2], HEAD_DIM, axis=1), heads[3])
        acc = jnp.concatenate([blk0, blk1], axis=-1)
        o_ref[t * tq:(t + 1) * tq, :] = _rms(acc, gain_ref[...]).astype(BF16)


def _dense_call(proj3d, gain):
    b, seq, _ = proj3d.shape
    return pl.pallas_call(
        _dense_kernel,
        grid=(b, seq // DENSE_ROWS),
        in_specs=[
            pl.BlockSpec((None, DENSE_ROWS, GROUP_WIDTH), lambda bi, j: (bi, j, SEC_DQ)),
            pl.BlockSpec((None, seq, GROUP_WIDTH), lambda bi, j: (bi, 0, SEC_DK)),
            pl.BlockSpec((None, seq, LANES), lambda bi, j: (bi, 0, SEC_DV * (GROUP_WIDTH // LANES))),
            _resident((1, GROUP_WIDTH)),
        ],
        out_specs=pl.BlockSpec((None, DENSE_ROWS, GROUP_WIDTH), lambda bi, j: (bi, j, 0)),
        out_shape=jax.ShapeDtypeStruct((b, seq, GROUP_WIDTH), BF16),
        compiler_params=_params(("parallel", "arbitrary")),
        name="dense",
    )(proj3d, proj3d, proj3d, gain)


def _mix_kernel(x_ref, ya_ref, yb_ref, yc_ref, yd_ref, w_ref, g_ref, x1_ref, xn_ref, xs_ref):
    mixed = jnp.concatenate([ya_ref[...], yb_ref[...], yc_ref[...], yd_ref[...]], axis=-1)
    x1 = x_ref[...] + jnp.dot(mixed, w_ref[...], preferred_element_type=F32)
    x1_ref[...] = x1
    xn = _rms(x1, g_ref[...])
    tm = xn.shape[0]
    for c in range(D_MODEL // LANES):
        xs_ref[c] = xn[:, c * LANES:(c + 1) * LANES]
        col = jnp.concatenate([xs_ref[c, pl.ds(b, tm // STRANDS, stride=STRANDS), :] for b in range(STRANDS)],
                              axis=0)
        xn_ref[:, c * LANES:(c + 1) * LANES] = col.astype(BF16)


def _mix_call(x2d, ya, yb, proj2d, yd, w_out, g_ffn):
    n = x2d.shape[0]
    row = lambda i: (i, 0)
    grp = pl.BlockSpec((MIX_TM, GROUP_WIDTH), row)
    return pl.pallas_call(
        _mix_kernel,
        grid=(n // MIX_TM,),
        in_specs=[pl.BlockSpec((MIX_TM, D_MODEL), row), grp, grp,
                  pl.BlockSpec((MIX_TM, GROUP_WIDTH), lambda i: (i, SEC_YC)), grp,
                  _resident(w_out.shape), _resident((1, D_MODEL))],
        out_specs=[pl.BlockSpec((MIX_TM, D_MODEL), row), pl.BlockSpec((MIX_TM, D_MODEL), row)],
        out_shape=[jax.ShapeDtypeStruct((n, D_MODEL), F32), jax.ShapeDtypeStruct((n, D_MODEL), BF16)],
        scratch_shapes=[pltpu.VMEM((D_MODEL // LANES, MIX_TM, LANES), F32)],
        compiler_params=_params(("parallel",)),
        name="mix_out",
    )(x2d, ya, yb, proj2d, yd, w_out, g_ffn)


def _ffn_kernel(xn_ref, prev_ref, next_ref, x1_ref, p_ref, wup_ref, cw_ref, wd_ref, gple_ref, wgate_ref,
                wproj_ref, o_ref, xcat_ref, h0_ref, h1_ref, acc_ref, tok_ref, *, tiles_per_seq, n_tiles):
    step = pl.program_id(0)
    tm = xn_ref.shape[0]
    rows = tm + HALO
    sl = tm // STRANDS
    pitch = sl + SUB

    def up(k, h_ref):
        xcat = xcat_ref[...]
        for part in range(2):
            col = (k + part * N_CHUNKS) * FFN_TF
            if not isinstance(col, int):
                col = pl.multiple_of(col, FFN_TF)
            h = jnp.dot(xcat, wup_ref[:, pl.ds(col, FFN_TF)], preferred_element_type=F32)
            for b in range(STRANDS):
                h_ref[part, SUB + b * pitch:SUB + b * pitch + sl, :] = h[b * sl:(b + 1) * sl]
            h_ref[part, SUB + sl:SUB + pitch, :] = h[tm:tm + SUB]
            h_ref[part, (STRANDS - 1) * pitch:SUB + (STRANDS - 1) * pitch, :] = h[rows - SUB:rows]

    def strand(h_ref, part, b, shift=0):
        start = SUB + b * pitch + shift
        return h_ref[part, start:start + sl, :]

    def conv(h_ref, part, cw):
        out = []
        for b in range(STRANDS):
            before = strand(h_ref, part, b - 1) if b > 0 else strand(h_ref, part, STRANDS - 1, -1)
            after = strand(h_ref, part, b + 1) if b < STRANDS - 1 else strand(h_ref, part, 0, 1)
            out.append(before * cw[0:1, :] + strand(h_ref, part, b) * cw[1:2, :] + after * cw[2:3, :] + cw[3:4, :])
        return jnp.concatenate(out, axis=0)

    def down(k, h_ref):
        gt = conv(h_ref, 0, cw_ref[k])
        uu = conv(h_ref, 1, cw_ref[k + N_CHUNKS])
        act = (gt / (1.0 + jnp.exp(-gt)) * uu).astype(BF16)
        acc_ref[...] += jnp.dot(act, wd_ref[k], preferred_element_type=F32)

    def stage():
        pos = jnp.minimum(step, n_tiles - 1) % tiles_per_seq
        prev_blk = jnp.where(pos == 0, jnp.zeros_like(prev_ref[...]), prev_ref[...])
        next_blk = jnp.where(pos == tiles_per_seq - 1, jnp.zeros_like(next_ref[...]), next_ref[...])
        halo_row = lax.broadcasted_iota(jnp.int32, (HALO, D_MODEL), 0)
        xcat_ref[0:tm, :] = xn_ref[...]
        xcat_ref[tm:rows, :] = jnp.where(halo_row == 0, next_blk, prev_blk)
        up(0, h0_ref)

    @pl.when(step == 0)
    def _():
        stage()

    @pl.when(step > 0)
    def _():
        acc_ref[...] = jnp.zeros_like(acc_ref)

        def pair(t, carry):
            k = 2 * t
            up(k + 1, h1_ref)
            down(k, h0_ref)
            up(k + 2, h0_ref)
            down(k + 1, h1_ref)
            return carry

        lax.fori_loop(0, (N_CHUNKS - 1) // 2, pair, 0, unroll=FFN_UNROLL)
        down(N_CHUNKS - 1, h0_ref)

        stage()

        for c in range(D_MODEL // LANES):
            for b in range(STRANDS):
                tok_ref[c, pl.ds(b, sl, stride=STRANDS), :] = acc_ref[b * sl:(b + 1) * sl, c * LANES:(c + 1) * LANES]
        ffn = jnp.concatenate([tok_ref[c] for c in range(D_MODEL // LANES)], axis=-1)
        x2 = x1_ref[...] + ffn
        z = jnp.dot(_rms(x2, gple_ref[...]).astype(BF16), wgate_ref[...], preferred_element_type=F32)
        gate = 1.0 / (1.0 + jnp.exp(-z))
        inj = jnp.dot(p_ref[...].astype(BF16), wproj_ref[...], preferred_element_type=F32)
        o_ref[...] = x2 + inj * gate


def _ffn_call(xn, x1, p2d, wup, cw, wd, g_ple, w_gate, w_proj, seq):
    assert N_CHUNKS % 2 == 1
    assert MIX_TM == FFN_TM
    n = xn.shape[0]
    n_tiles = n // FFN_TM
    halo_per_tile = FFN_TM // HALO
    n_halo = n // HALO
    staged = lambda s: jnp.minimum(s, n_tiles - 1)
    done = lambda s: (jnp.maximum(s - 1, 0), 0)
    kern = functools.partial(_ffn_kernel, tiles_per_seq=seq // FFN_TM, n_tiles=n_tiles)
    return pl.pallas_call(
        kern,
        grid=(n_tiles + 1,),
        in_specs=[
            pl.BlockSpec((FFN_TM, D_MODEL), lambda s: (staged(s), 0)),
            pl.BlockSpec((HALO, D_MODEL), lambda s: (jnp.maximum(staged(s) * halo_per_tile - 1, 0), 0)),
            pl.BlockSpec((HALO, D_MODEL),
                         lambda s: (jnp.minimum((staged(s) + 1) * halo_per_tile, n_halo - 1), 0)),
            pl.BlockSpec((FFN_TM, D_MODEL), done),
            pl.BlockSpec((FFN_TM, PLE_DIM), done),
            _resident(wup.shape), _resident(cw.shape), _resident(wd.shape),
            _resident((1, D_MODEL)), _resident(w_gate.shape), _resident(w_proj.shape),
        ],
        out_specs=pl.BlockSpec((FFN_TM, D_MODEL), done),
        out_shape=jax.ShapeDtypeStruct((n, D_MODEL), F32),
        scratch_shapes=[pltpu.VMEM((FFN_TM + HALO, D_MODEL), BF16),
                        pltpu.VMEM((2, SUB + STRANDS * (FFN_TM // STRANDS + SUB), FFN_TF), F32),
                        pltpu.VMEM((2, SUB + STRANDS * (FFN_TM // STRANDS + SUB), FFN_TF), F32),
                        pltpu.VMEM((FFN_TM, D_MODEL), F32),
                        pltpu.VMEM((D_MODEL // LANES, FFN_TM, LANES), F32)],
        compiler_params=_params(("arbitrary",)),
        name="ffn_ple",
    )(xn, xn, xn, x1, p2d, wup, cw, wd, g_ple, w_gate, w_proj)


def _t5_bucket(rel):
    nb = REL_BUCKETS // 2
    ret = jnp.where(rel > 0, nb, 0)
    n = jnp.abs(rel)
    max_exact = nb // 2
    nf = jnp.maximum(n, 1).astype(F32)
    large = max_exact + (jnp.log(nf / max_exact) / math.log(REL_MAX_DIST / max_exact)
                         * (nb - max_exact)).astype(jnp.int32)
    large = jnp.minimum(large, nb - 1)
    return ret + jnp.where(n < max_exact, n, large)


def _band_bias(table, radius, dil, tq, kw):
    cases = []
    for off in (0, radius, kw - tq):
        rel = jnp.arange(kw)[None, :] - jnp.arange(tq)[:, None] - off
        onehot = (_t5_bucket(rel * dil)[..., None] == jnp.arange(REL_BUCKETS)).astype(F32)
        bias = jnp.einsum('qkn,nh->hqk', onehot, table.astype(F32), precision=lax.Precision.HIGHEST)
        bias = jnp.where((jnp.abs(rel) <= radius)[None], bias, NEG_INF)
        cases.append(bias.reshape(N_HEADS * tq, kw))
    return jnp.stack(cases, axis=0)


def _rope_tables(seq):
    half = HEAD_DIM // 2
    inv = ROPE_THETA ** (-jnp.arange(0, half, 2, dtype=F32) / half)
    t = jnp.arange(seq)
    ang_r = (t // GRID_W).astype(F32)[:, None] * inv[None, :]
    ang_c = (t % GRID_W).astype(F32)[:, None] * inv[None, :]
    cos_h = jnp.concatenate([jnp.cos(ang_r)] * 2 + [jnp.cos(ang_c)] * 2, axis=-1)
    sin_h = jnp.concatenate([-jnp.sin(ang_r), jnp.sin(ang_r), -jnp.sin(ang_c), jnp.sin(ang_c)], axis=-1)
    return jnp.tile(cos_h, (1, N_HEADS)), jnp.tile(sin_h, (1, N_HEADS))


def kernel(x, p, rel_bias, ln_mix_g, w_in, qk_gain, sink, c_norm_g, c_norm_b, c_ws, c_bs, out_gain, w_out,
           ln_ffn_g, w_up, conv_w, conv_b, w_down, ln_ple_g, w_ple_gate, w_ple_proj):
    b, s, _ = x.shape
    n = b * s
    depth = w_in.shape[0]
    cos_t, sin_t = _rope_tables(s)
    bd = (np.arange(GROUP_WIDTH)[:, None] // HEAD_DIM == np.arange(GROUP_WIDTH)[None, :] // HEAD_DIM)
    bd = jnp.asarray(bd / HEAD_DIM, BF16)
    table_a = rel_bias[:, :N_HEADS] * LOG2E
    table_b = rel_bias[:, N_HEADS:] * LOG2E
    q_scale = ATTN_SCALE * LOG2E
    bias_a = []
    for window, dil in DILATED_CFGS:
        radius = window // (2 * dil)
        length = s // dil
        tq = min(BAND_TQ, length)
        bias_a.append(_band_bias(table_a, radius, dil, tq, min(tq + 2 * radius, length)))
    tq_b = min(BAND_TQ, s)
    bias_b = _band_bias(table_b, SWA_RADIUS, 1, tq_b, min(tq_b + 2 * SWA_RADIUS, s))

    x2d = x.reshape(n, D_MODEL)
    for i in range(depth):
        w = w_in[i].astype(BF16)
        qg = qk_gain[i]
        vec = jnp.stack([
            jnp.tile(qg[0, 0], N_HEADS) * q_scale, jnp.tile(qg[0, 1], N_HEADS),
            jnp.tile(qg[1, 0], N_HEADS) * q_scale, jnp.tile(qg[1, 1], N_HEADS),
            jnp.tile(qg[2, 0], N_HEADS) * q_scale, jnp.tile(qg[2, 1], N_HEADS),
            c_norm_g[i], c_norm_b[i], out_gain[i, 2],
        ] + [jnp.zeros((GROUP_WIDTH,), F32)] * 7, axis=0)
        wcat = jnp.concatenate([c_ws[i, g] for g in range(C_GROUPS)], axis=1).astype(BF16)
        bsx = jnp.repeat(jnp.transpose(c_bs[i]), GROUP_WIDTH // C_GROUPS, axis=1)
        proj2d, d4, d16 = _proj_call(x2d, ln_mix_g[i][None], w, vec, cos_t, sin_t, bd, wcat, bsx, b, s)

        proj3d = proj2d.reshape(b, s, N_SEC * GROUP_WIDTH)
        ya = _dilated_call(proj3d, d4, d16, bias_a, out_gain[i, 0][None])
        sink_blk = jnp.full((N_HEADS * tq_b, LANES), NEG_INF, F32).at[:, 0].set(jnp.repeat(sink[i] * LOG2E, tq_b))
        yb = _swa_call(proj3d, bias_b, sink_blk, out_gain[i, 1][None])
        yd = _dense_call(proj3d, out_gain[i, 3][None])

        x1, xn = _mix_call(x2d, ya.reshape(n, GROUP_WIDTH), yb.reshape(n, GROUP_WIDTH), proj2d,
                           yd.reshape(n, GROUP_WIDTH), w_out[i].astype(BF16), ln_ffn_g[i][None])
        wup = w_up[i].astype(BF16)
        cw = jnp.concatenate([conv_w[i], conv_b[i][None]], axis=0)
        cw = jnp.transpose(cw.reshape(4, 2 * N_CHUNKS, FFN_TF), (1, 0, 2))
        wd = w_down[i].astype(BF16).reshape(N_CHUNKS, FFN_TF, D_MODEL)
        x2d = _ffn_call(xn, x1, p[i].reshape(n, PLE_DIM), wup, cw, wd, ln_ple_g[i][None],
                        w_ple_gate[i].astype(BF16), w_ple_proj[i].astype(BF16), s)
    return x2d.reshape(b, s, D_MODEL)
```

```python
import functools
import math

import jax
import jax.numpy as jnp
import numpy as np
from jax import lax
from jax.experimental import pallas as pl
from jax.experimental.pallas import tpu as pltpu

D_MODEL = 1024
HEAD_DIM = 64
GROUP_WIDTH = 256
N_HEADS = GROUP_WIDTH // HEAD_DIM
C_GROUPS = 4
C_CHUNK = 128
DILATED_CFGS = ((128, 1), (512, 4), (2048, 16))
SWA_RADIUS = 128
GRID_W = 64
ROPE_THETA = 10000.0
REL_BUCKETS = 32
REL_MAX_DIST = 1024
D_FF = 2816
PLE_DIM = 256
EPS = 1e-6
NEG_INF = -1e30
ATTN_SCALE = HEAD_DIM ** -0.5
LOG2E = math.log2(math.e)

BF16 = jnp.bfloat16
F32 = jnp.float32

SEC_AQ, SEC_AK, SEC_AV, SEC_BQ, SEC_BK, SEC_BV, SEC_YC, SEC_DQ, SEC_DK, SEC_DV = range(10)
N_SEC = 10
A_WIDTH = 3 * GROUP_WIDTH

VMEM_LIMIT = 56 * 1024 * 1024

KIN_TM = 512
KIN_ROWS = 512
BAND_TQ = 128
BAND_UNROLL = 16
SWA_UNROLL = 16
DENSE_TQ = 256
DENSE_ROWS = 1024
MIX_TM = 512
FFN_TM = 512
FFN_TF = 256
LANES = 128
SUB = 8
STRANDS = 8
HALO = 16
N_CHUNKS = D_FF // FFN_TF
FFN_UNROLL = True


def _params(sem):
    return pltpu.CompilerParams(dimension_semantics=sem, vmem_limit_bytes=VMEM_LIMIT)


def _resident(shape):
    zeros = (0,) * len(shape)
    return pl.BlockSpec(shape, lambda *_: zeros, pipeline_mode=pl.Buffered(1))


def _rms(t, gain):
    return t * lax.rsqrt(jnp.mean(t * t, axis=-1, keepdims=True) + EPS) * gain


def _head_id(shape):
    return lax.broadcasted_iota(jnp.int32, shape, 1) >> 6


def _proj_kernel(x_ref, g_ref, w_ref, vec_ref, cos_ref, sin_ref, bd_ref, wcat_ref, bsx_ref,
                 o_ref, d4_ref, d16_ref, a_ref):
    for t in range(x_ref.shape[0] // KIN_TM):
        _proj_tile(t, x_ref, g_ref, w_ref, vec_ref, cos_ref, sin_ref, bd_ref, wcat_ref, bsx_ref,
                   o_ref, d4_ref, d16_ref, a_ref)


def _proj_tile(t, x_ref, g_ref, w_ref, vec_ref, cos_ref, sin_ref, bd_ref, wcat_ref, bsx_ref,
               o_ref, d4_ref, d16_ref, a_ref):
    tm = KIN_TM
    r0 = t * tm
    x = x_ref[r0:r0 + tm, :]
    hn = _rms(x, g_ref[...]).astype(BF16)
    bd = bd_ref[...]
    slots = A_WIDTH // LANES

    def proj(sec):
        return jnp.dot(hn, w_ref[:, sec * GROUP_WIDTH:(sec + 1) * GROUP_WIDTH],
                       preferred_element_type=F32)

    def store(sec, val):
        o_ref[r0:r0 + tm, sec * GROUP_WIDTH:(sec + 1) * GROUP_WIDTH] = val.astype(BF16)

    def store_a(sec, val):
        store(sec, val)
        for half in range(GROUP_WIDTH // LANES):
            slot = sec * (GROUP_WIDTH // LANES) + half
            a_ref[t * slots + slot] = val[:, half * LANES:(half + 1) * LANES]
            cols = slice(slot * LANES, (slot + 1) * LANES)
            for dil, ref in ((4, d4_ref), (16, d16_ref)):
                for r in range(dil):
                    piece = a_ref[t * slots + slot, pl.ds(r, tm // dil, stride=dil), :]
                    ref[r, t * (tm // dil):(t + 1) * (tm // dil), cols] = piece.astype(BF16)

    def head_rms(t, row):
        width = t.shape[1]
        sq = t * t
        hi = sq.astype(BF16)
        lo = (sq - hi.astype(F32)).astype(BF16)
        blk = bd[0:width, 0:width]
        ms = (jnp.dot(hi, blk, preferred_element_type=F32) + jnp.dot(lo, blk, preferred_element_type=F32))
        return t * lax.rsqrt(ms + EPS) * vec_ref[row:row + 1, 0:width]

    def rope(t):
        blocks = []
        for c0 in range(0, t.shape[1], LANES):
            th = t[:, c0:c0 + LANES]
            lane = lax.broadcasted_iota(jnp.int32, th.shape, 1)
            partner = jnp.where((lane & 31) < 16, pltpu.roll(th, LANES - 16, axis=1), pltpu.roll(th, 16, axis=1))
            blocks.append(th * cos_ref[r0:r0 + tm, c0:c0 + LANES] + partner * sin_ref[r0:r0 + tm, c0:c0 + LANES])
        return blocks[0] if len(blocks) == 1 else jnp.concatenate(blocks, axis=-1)

    def rep_kv(t):
        swapped = pltpu.roll(t, HEAD_DIM, axis=1)
        low = lax.broadcasted_iota(jnp.int32, t.shape, 1) < HEAD_DIM
        return jnp.concatenate([jnp.where(low, t, swapped), jnp.where(low, swapped, t)], axis=-1)

    store_a(SEC_AQ, head_rms(proj(0), 0))
    store_a(SEC_AK, head_rms(proj(1), 1))
    store_a(SEC_AV, proj(2))
    store(SEC_BQ, head_rms(proj(3), 2))
    b_kv = proj(4)
    store(SEC_BK, rep_kv(head_rms(b_kv[:, 0:LANES], 3)))
    store(SEC_BV, rep_kv(b_kv[:, LANES:]))
    store(SEC_DQ, rope(head_rms(proj(7), 4)))
    d_kv = proj(8)
    store(SEC_DK, rep_kv(rope(head_rms(d_kv[:, 0:LANES], 5))))
    store(SEC_DV, jnp.concatenate([d_kv[:, LANES:]] * 2, axis=-1))

    u = jax.nn.gelu(proj(5))
    v = jax.nn.gelu(proj(6))
    mu = jnp.mean(v, axis=-1, keepdims=True)
    vc = v - mu
    var = jnp.mean(vc * vc, axis=-1, keepdims=True)
    vn = (vc * lax.rsqrt(var + EPS) * vec_ref[6:7, :] + vec_ref[7:8, :]).astype(BF16)
    gid = _head_id((C_CHUNK, GROUP_WIDTH))
    zero = jnp.zeros((C_CHUNK, GROUP_WIDTH), BF16)
    for c in range(tm // C_CHUNK):
        rows = slice(c * C_CHUNK, (c + 1) * C_CHUNK)
        vchunk = vn[rows]
        stacked = jnp.concatenate([jnp.where(gid == g, vchunk, zero) for g in range(C_GROUPS)], axis=0)
        mixed = jnp.dot(wcat_ref[...], stacked, preferred_element_type=F32) + bsx_ref[...]
        yc = _rms(u[rows] * mixed, vec_ref[8:9, :])
        o_ref[r0 + c * C_CHUNK:r0 + (c + 1) * C_CHUNK, SEC_YC * GROUP_WIDTH:(SEC_YC + 1) * GROUP_WIDTH] = yc.astype(BF16)


def _proj_call(x2d, g, w, vec, cos_t, sin_t, bd, wcat, bsx, batch, seq):
    n = x2d.shape[0]
    tps = seq // KIN_ROWS
    return pl.pallas_call(
        _proj_kernel,
        grid=(n // KIN_ROWS,),
        in_specs=[
            pl.BlockSpec((KIN_ROWS, D_MODEL), lambda i: (i, 0)),
            _resident((1, D_MODEL)),
            _resident(w.shape),
            _resident(vec.shape),
            pl.BlockSpec((KIN_ROWS, GROUP_WIDTH), lambda i: (i % tps, 0)),
            pl.BlockSpec((KIN_ROWS, GROUP_WIDTH), lambda i: (i % tps, 0)),
            _resident(bd.shape),
            _resident(wcat.shape),
            _resident(bsx.shape),
        ],
        out_specs=[
            pl.BlockSpec((KIN_ROWS, N_SEC * GROUP_WIDTH), lambda i: (i, 0)),
            pl.BlockSpec((None, 4, KIN_ROWS // 4, A_WIDTH), lambda i: (i // tps, 0, i % tps, 0)),
            pl.BlockSpec((None, 16, KIN_ROWS // 16, A_WIDTH), lambda i: (i // tps, 0, i % tps, 0)),
        ],
        out_shape=[
            jax.ShapeDtypeStruct((n, N_SEC * GROUP_WIDTH), BF16),
            jax.ShapeDtypeStruct((batch, 4, seq // 4, A_WIDTH), BF16),
            jax.ShapeDtypeStruct((batch, 16, seq // 16, A_WIDTH), BF16),
        ],
        scratch_shapes=[pltpu.VMEM((KIN_ROWS // KIN_TM * (A_WIDTH // LANES), KIN_TM, LANES), F32)],
        compiler_params=_params(("parallel",)),
        name="proj",
    )(x2d, g, w, vec, cos_t, sin_t, bd, wcat, bsx)


def _attend(q, kwin, vwin, bias_ref, case, sink_ref=None):
    tq = q.shape[0]
    hid = _head_id((tq, GROUP_WIDTH))
    qzero = jnp.zeros_like(q)
    q4 = jnp.concatenate([jnp.where(hid == h, q, qzero) for h in range(N_HEADS)], axis=0)
    s = lax.dot_general(q4, kwin, (((1,), (1,)), ((), ())), preferred_element_type=F32)
    s = s + bias_ref[case]
    if sink_ref is None:
        m = jnp.max(s, axis=-1, keepdims=True)
        p = jnp.exp2(s - m)
        den = jnp.sum(p, axis=-1, keepdims=True)
    else:
        sink = sink_ref[...]
        blocks = [s[:, c:c + LANES] for c in range(0, s.shape[1], LANES)]
        m = jnp.max(functools.reduce(jnp.maximum, blocks + [sink]), axis=-1, keepdims=True)
        p = jnp.exp2(s - m)
        pblocks = [p[:, c:c + LANES] for c in range(0, s.shape[1], LANES)]
        den = jnp.sum(functools.reduce(jnp.add, pblocks + [jnp.exp2(sink - m)]), axis=-1, keepdims=True)
    o4 = jnp.dot(p.astype(BF16), vwin, preferred_element_type=F32) * (1.0 / den)
    l4 = jnp.broadcast_to(m + jnp.log2(den), o4.shape)
    out = o4[0:tq]
    lse = l4[0:tq]
    for h in range(1, N_HEADS):
        out = jnp.where(hid == h, o4[h * tq:(h + 1) * tq], out)
        lse = jnp.where(hid == h, l4[h * tq:(h + 1) * tq], lse)
    return out, lse


def _band_window(j, tq, radius, length, kw):
    n_tiles = length // tq
    if n_tiles == 1:
        return 0, 0
    ks = pl.multiple_of(jnp.clip(j * tq - radius, 0, length - kw), 64)
    case = jnp.where(j == 0, 0, jnp.where(j == n_tiles - 1, 2, 1))
    return ks, case


def _dilated_kernel(q_ref, k_ref, v_ref, d4_ref, d16_ref, b1_ref, b4_ref, b16_ref, gain_ref, o_ref,
                    acc_ref, lse_ref, *, radii):
    seq = q_ref.shape[0]
    tq = BAND_TQ

    kw1 = min(tq + 2 * radii[0], seq)

    def body1(j, carry):
        q0 = pl.multiple_of(j * tq, tq)
        ks, case = _band_window(j, tq, radii[0], seq, kw1)
        o, l = _attend(q_ref[pl.ds(q0, tq), :], k_ref[pl.ds(ks, kw1), :], v_ref[pl.ds(ks, kw1), :], b1_ref, case)
        for half in range(GROUP_WIDTH // LANES):
            cols = slice(half * LANES, (half + 1) * LANES)
            acc_ref[half, pl.ds(q0, tq), :] = o[:, cols]
            lse_ref[half, pl.ds(q0, tq), :] = l[:, cols]
        return carry

    lax.fori_loop(0, seq // tq, body1, 0, unroll=BAND_UNROLL)

    for dil, ref, bias_ref, radius in ((4, d4_ref, b4_ref, radii[1]), (16, d16_ref, b16_ref, radii[2])):
        length = seq // dil
        tiles = length // tq
        kw = min(tq + 2 * radius, length)

        def body(it, carry, dil=dil, ref=ref, bias_ref=bias_ref, radius=radius, length=length, tiles=tiles, kw=kw):
            r = it // tiles
            j = it % tiles
            q0 = pl.multiple_of(j * tq, tq)
            ks, case = _band_window(j, tq, radius, length, kw)
            o, l = _attend(ref[r, pl.ds(q0, tq), 0:GROUP_WIDTH],
                           ref[r, pl.ds(ks, kw), GROUP_WIDTH:2 * GROUP_WIDTH],
                           ref[r, pl.ds(ks, kw), 2 * GROUP_WIDTH:3 * GROUP_WIDTH], bias_ref, case)
            rows = pl.ds(r + dil * q0, tq, stride=dil)
            for half in range(GROUP_WIDTH // LANES):
                cols = slice(half * LANES, (half + 1) * LANES)
                l_new = l[:, cols]
                l_old = lse_ref[half, rows, :]
                m = jnp.maximum(l_old, l_new)
                wa = jnp.exp2(l_old - m)
                wb = jnp.exp2(l_new - m)
                tot = wa + wb
                acc_ref[half, rows, :] = (wa * acc_ref[half, rows, :] + wb * o[:, cols]) * (1.0 / tot)
                lse_ref[half, rows, :] = m + jnp.log2(tot)
            return carry

        lax.fori_loop(0, dil * tiles, body, 0, unroll=BAND_UNROLL)

    acc = jnp.concatenate([acc_ref[0], acc_ref[1]], axis=-1)
    o_ref[...] = _rms(acc, gain_ref[...]).astype(BF16)


def _dilated_call(proj3d, d4, d16, biases, gain):
    b, seq, _ = proj3d.shape
    radii = tuple(window // (2 * dil) for window, dil in DILATED_CFGS)
    kern = functools.partial(_dilated_kernel, radii=radii)
    return pl.pallas_call(
        kern,
        grid=(b,),
        in_specs=[
            pl.BlockSpec((None, seq, GROUP_WIDTH), lambda bi: (bi, 0, SEC_AQ)),
            pl.BlockSpec((None, seq, GROUP_WIDTH), lambda bi: (bi, 0, SEC_AK)),
            pl.BlockSpec((None, seq, GROUP_WIDTH), lambda bi: (bi, 0, SEC_AV)),
            pl.BlockSpec((None,) + d4.shape[1:], lambda bi: (bi, 0, 0, 0)),
            pl.BlockSpec((None,) + d16.shape[1:], lambda bi: (bi, 0, 0, 0)),
            _resident(biases[0].shape), _resident(biases[1].shape), _resident(biases[2].shape),
            _resident((1, GROUP_WIDTH)),
        ],
        out_specs=pl.BlockSpec((None, seq, GROUP_WIDTH), lambda bi: (bi, 0, 0)),
        out_shape=jax.ShapeDtypeStruct((b, seq, GROUP_WIDTH), BF16),
        scratch_shapes=[pltpu.VMEM((GROUP_WIDTH // LANES, seq, LANES), F32)] * 2,
        compiler_params=_params(("parallel",)),
        name="dilated",
    )(proj3d, proj3d, proj3d, d4, d16, biases[0], biases[1], biases[2], gain)


def _swa_kernel(q_ref, k_ref, v_ref, bias_ref, sink_ref, gain_ref, o_ref):
    seq = q_ref.shape[0]
    tq = BAND_TQ
    kw = min(tq + 2 * SWA_RADIUS, seq)

    def body(j, carry):
        q0 = pl.multiple_of(j * tq, tq)
        ks, case = _band_window(j, tq, SWA_RADIUS, seq, kw)
        o, _ = _attend(q_ref[pl.ds(q0, tq), :], k_ref[pl.ds(ks, kw), :], v_ref[pl.ds(ks, kw), :],
                       bias_ref, case, sink_ref)
        o_ref[pl.ds(q0, tq), :] = _rms(o, gain_ref[...]).astype(BF16)
        return carry

    lax.fori_loop(0, seq // tq, body, 0, unroll=SWA_UNROLL)


def _swa_call(proj3d, bias, sink, gain):
    b, seq, _ = proj3d.shape
    return pl.pallas_call(
        _swa_kernel,
        grid=(b,),
        in_specs=[
            pl.BlockSpec((None, seq, GROUP_WIDTH), lambda bi: (bi, 0, SEC_BQ)),
            pl.BlockSpec((None, seq, GROUP_WIDTH), lambda bi: (bi, 0, SEC_BK)),
            pl.BlockSpec((None, seq, GROUP_WIDTH), lambda bi: (bi, 0, SEC_BV)),
            _resident(bias.shape),
            _resident(sink.shape),
            _resident((1, GROUP_WIDTH)),
        ],
        out_specs=pl.BlockSpec((None, seq, GROUP_WIDTH), lambda bi: (bi, 0, 0)),
        out_shape=jax.ShapeDtypeStruct((b, seq, GROUP_WIDTH), BF16),
        compiler_params=_params(("parallel",)),
        name="swa",
    )(proj3d, proj3d, proj3d, bias, sink, gain)


def _dense_kernel(q_ref, k_ref, v_ref, gain_ref, o_ref):
    k = k_ref[...]
    vext = jnp.concatenate([v_ref[...], jnp.ones((k.shape[0], LANES), BF16)], axis=-1)
    tq = DENSE_TQ
    hid = _head_id((tq, GROUP_WIDTH))
    low = lax.broadcasted_iota(jnp.int32, (tq, LANES), 1) < HEAD_DIM
    for t in range(q_ref.shape[0] // tq):
        q = q_ref[t * tq:(t + 1) * tq, :]
        qzero = jnp.zeros_like(q)
        heads = []
        for h in range(N_HEADS):
            qh = jnp.where(hid == h, q, qzero)
            s = lax.dot_general(qh, k, (((1,), (1,)), ((), ())), preferred_element_type=F32)
            m = jnp.max(s, axis=-1, keepdims=True)
            p = jnp.exp2((s - m).astype(BF16))
            o = jnp.dot(p, vext, preferred_element_type=F32)
            heads.append(o[:, 0:LANES] * (1.0 / o[:, LANES:]))
        blk0 = jnp.where(low, heads[0], pltpu.roll(heads[1], HEAD_DIM, axis=1))
        blk1 = jnp.where(low, pltpu.roll(heads[2], HEAD_DIM, axis=1), heads[3])
        acc = jnp.concatenate([blk0, blk1], axis=-1)
        o_ref[t * tq:(t + 1) * tq, :] = _rms(acc, gain_ref[...]).astype(BF16)


def _dense_call(proj3d, gain):
    b, seq, _ = proj3d.shape
    return pl.pallas_call(
        _dense_kernel,
        grid=(b, seq // DENSE_ROWS),
        in_specs=[
            pl.BlockSpec((None, DENSE_ROWS, GROUP_WIDTH), lambda bi, j: (bi, j, SEC_DQ)),
            pl.BlockSpec((None, seq, GROUP_WIDTH), lambda bi, j: (bi, 0, SEC_DK)),
            pl.BlockSpec((None, seq, LANES), lambda bi, j: (bi, 0, SEC_DV * (GROUP_WIDTH // LANES))),
            _resident((1, GROUP_WIDTH)),
        ],
        out_specs=pl.BlockSpec((None, DENSE_ROWS, GROUP_WIDTH), lambda bi, j: (bi, j, 0)),
        out_shape=jax.ShapeDtypeStruct((b, seq, GROUP_WIDTH), BF16),
        compiler_params=_params(("parallel", "arbitrary")),
        name="dense",
    )(proj3d, proj3d, proj3d, gain)


def _mix_kernel(x_ref, ya_ref, yb_ref, yc_ref, yd_ref, w_ref, g_ref, x1_ref, xn_ref, xs_ref):
    mixed = jnp.concatenate([ya_ref[...], yb_ref[...], yc_ref[...], yd_ref[...]], axis=-1)
    x1 = x_ref[...] + jnp.dot(mixed, w_ref[...], preferred_element_type=F32)
    x1_ref[...] = x1
    xn = _rms(x1, g_ref[...])
    tm = xn.shape[0]
    for c in range(D_MODEL // LANES):
        xs_ref[c] = xn[:, c * LANES:(c + 1) * LANES]
        col = jnp.concatenate([xs_ref[c, pl.ds(b, tm // STRANDS, stride=STRANDS), :] for b in range(STRANDS)],
                              axis=0)
        xn_ref[:, c * LANES:(c + 1) * LANES] = col.astype(BF16)


def _mix_call(x2d, ya, yb, proj2d, yd, w_out, g_ffn):
    n = x2d.shape[0]
    row = lambda i: (i, 0)
    grp = pl.BlockSpec((MIX_TM, GROUP_WIDTH), row)
    return pl.pallas_call(
        _mix_kernel,
        grid=(n // MIX_TM,),
        in_specs=[pl.BlockSpec((MIX_TM, D_MODEL), row), grp, grp,
                  pl.BlockSpec((MIX_TM, GROUP_WIDTH), lambda i: (i, SEC_YC)), grp,
                  _resident(w_out.shape), _resident((1, D_MODEL))],
        out_specs=[pl.BlockSpec((MIX_TM, D_MODEL), row), pl.BlockSpec((MIX_TM, D_MODEL), row)],
        out_shape=[jax.ShapeDtypeStruct((n, D_MODEL), F32), jax.ShapeDtypeStruct((n, D_MODEL), BF16)],
        scratch_shapes=[pltpu.VMEM((D_MODEL // LANES, MIX_TM, LANES), F32)],
        compiler_params=_params(("parallel",)),
        name="mix_out",
    )(x2d, ya, yb, proj2d, yd, w_out, g_ffn)


def _ffn_kernel(xn_ref, prev_ref, next_ref, x1_ref, p_ref, wup_ref, cw_ref, wd_ref, gple_ref, wgate_ref,
                wproj_ref, o_ref, xcat_ref, h0_ref, h1_ref, acc_ref, tok_ref, *, tiles_per_seq, n_tiles):
    step = pl.program_id(0)
    tm = xn_ref.shape[0]
    rows = tm + HALO
    sl = tm // STRANDS
    pitch = sl + SUB

    def up(k, h_ref):
        xcat = xcat_ref[...]
        for part in range(2):
            col = (k + part * N_CHUNKS) * FFN_TF
            if not isinstance(col, int):
                col = pl.multiple_of(col, FFN_TF)
            h = jnp.dot(xcat, wup_ref[:, pl.ds(col, FFN_TF)], preferred_element_type=F32)
            for b in range(STRANDS):
                h_ref[part, SUB + b * pitch:SUB + b * pitch + sl, :] = h[b * sl:(b + 1) * sl]
            h_ref[part, SUB + sl:SUB + pitch, :] = h[tm:tm + SUB]
            h_ref[part, (STRANDS - 1) * pitch:SUB + (STRANDS - 1) * pitch, :] = h[rows - SUB:rows]

    def strand(h_ref, part, b, shift=0):
        start = SUB + b * pitch + shift
        return h_ref[part, start:start + sl, :]

    def conv(h_ref, part, cw):
        out = []
        for b in range(STRANDS):
            before = strand(h_ref, part, b - 1) if b > 0 else strand(h_ref, part, STRANDS - 1, -1)
            after = strand(h_ref, part, b + 1) if b < STRANDS - 1 else strand(h_ref, part, 0, 1)
            out.append(before * cw[0:1, :] + strand(h_ref, part, b) * cw[1:2, :] + after * cw[2:3, :] + cw[3:4, :])
        return jnp.concatenate(out, axis=0)

    def down(k, h_ref):
        gt = conv(h_ref, 0, cw_ref[k])
        uu = conv(h_ref, 1, cw_ref[k + N_CHUNKS])
        act = (gt / (1.0 + jnp.exp(-gt)) * uu).astype(BF16)
        acc_ref[...] += jnp.dot(act, wd_ref[k], preferred_element_type=F32)

    def stage():
        pos = jnp.minimum(step, n_tiles - 1) % tiles_per_seq
        prev_blk = jnp.where(pos == 0, jnp.zeros_like(prev_ref[...]), prev_ref[...])
        next_blk = jnp.where(pos == tiles_per_seq - 1, jnp.zeros_like(next_ref[...]), next_ref[...])
        halo_row = lax.broadcasted_iota(jnp.int32, (HALO, D_MODEL), 0)
        xcat_ref[0:tm, :] = xn_ref[...]
        xcat_ref[tm:rows, :] = jnp.where(halo_row == 0, next_blk, prev_blk)
        up(0, h0_ref)

    @pl.when(step == 0)
    def _():
        stage()

    @pl.when(step > 0)
    def _():
        acc_ref[...] = jnp.zeros_like(acc_ref)

        def pair(t, carry):
            k = 2 * t
            up(k + 1, h1_ref)
            down(k, h0_ref)
            up(k + 2, h0_ref)
            down(k + 1, h1_ref)
            return carry

        lax.fori_loop(0, (N_CHUNKS - 1) // 2, pair, 0, unroll=FFN_UNROLL)
        down(N_CHUNKS - 1, h0_ref)

        stage()

        for c in range(D_MODEL // LANES):
            for b in range(STRANDS):
                tok_ref[c, pl.ds(b, sl, stride=STRANDS), :] = acc_ref[b * sl:(b + 1) * sl, c * LANES:(c + 1) * LANES]
        ffn = jnp.concatenate([tok_ref[c] for c in range(D_MODEL // LANES)], axis=-1)
        x2 = x1_ref[...] + ffn
        z = jnp.dot(_rms(x2, gple_ref[...]).astype(BF16), wgate_ref[...], preferred_element_type=F32)
        gate = 1.0 / (1.0 + jnp.exp(-z))
        inj = jnp.dot(p_ref[...].astype(BF16), wproj_ref[...], preferred_element_type=F32)
        o_ref[...] = x2 + inj * gate


def _ffn_call(xn, x1, p2d, wup, cw, wd, g_ple, w_gate, w_proj, seq):
    assert N_CHUNKS % 2 == 1
    assert MIX_TM == FFN_TM
    n = xn.shape[0]
    n_tiles = n // FFN_TM
    halo_per_tile = FFN_TM // HALO
    n_halo = n // HALO
    staged = lambda s: jnp.minimum(s, n_tiles - 1)
    done = lambda s: (jnp.maximum(s - 1, 0), 0)
    kern = functools.partial(_ffn_kernel, tiles_per_seq=seq // FFN_TM, n_tiles=n_tiles)
    return pl.pallas_call(
        kern,
        grid=(n_tiles + 1,),
        in_specs=[
            pl.BlockSpec((FFN_TM, D_MODEL), lambda s: (staged(s), 0)),
            pl.BlockSpec((HALO, D_MODEL), lambda s: (jnp.maximum(staged(s) * halo_per_tile - 1, 0), 0)),
            pl.BlockSpec((HALO, D_MODEL),
                         lambda s: (jnp.minimum((staged(s) + 1) * halo_per_tile, n_halo - 1), 0)),
            pl.BlockSpec((FFN_TM, D_MODEL), done),
            pl.BlockSpec((FFN_TM, PLE_DIM), done),
            _resident(wup.shape), _resident(cw.shape), _resident(wd.shape),
            _resident((1, D_MODEL)), _resident(w_gate.shape), _resident(w_proj.shape),
        ],
        out_specs=pl.BlockSpec((FFN_TM, D_MODEL), done),
        out_shape=jax.ShapeDtypeStruct((n, D_MODEL), F32),
        scratch_shapes=[pltpu.VMEM((FFN_TM + HALO, D_MODEL), BF16),
                        pltpu.VMEM((2, SUB + STRANDS * (FFN_TM // STRANDS + SUB), FFN_TF), F32),
                        pltpu.VMEM((2, SUB + STRANDS * (FFN_TM // STRANDS + SUB), FFN_TF), F32),
                        pltpu.VMEM((FFN_TM, D_MODEL), F32),
                        pltpu.VMEM((D_MODEL // LANES, FFN_TM, LANES), F32)],
        compiler_params=_params(("arbitrary",)),
        name="ffn_ple",
    )(xn, xn, xn, x1, p2d, wup, cw, wd, g_ple, w_gate, w_proj)


def _t5_bucket(rel):
    nb = REL_BUCKETS // 2
    ret = jnp.where(rel > 0, nb, 0)
    n = jnp.abs(rel)
    max_exact = nb // 2
    nf = jnp.maximum(n, 1).astype(F32)
    large = max_exact + (jnp.log(nf / max_exact) / math.log(REL_MAX_DIST / max_exact)
                         * (nb - max_exact)).astype(jnp.int32)
    large = jnp.minimum(large, nb - 1)
    return ret + jnp.where(n < max_exact, n, large)


def _band_bias(table, radius, dil, tq, kw):
    cases = []
    for off in (0, radius, kw - tq):
        rel = jnp.arange(kw)[None, :] - jnp.arange(tq)[:, None] - off
        onehot = (_t5_bucket(rel * dil)[..., None] == jnp.arange(REL_BUCKETS)).astype(F32)
        bias = jnp.einsum('qkn,nh->hqk', onehot, table.astype(F32), precision=lax.Precision.HIGHEST)
        bias = jnp.where((jnp.abs(rel) <= radius)[None], bias, NEG_INF)
        cases.append(bias.reshape(N_HEADS * tq, kw))
    return jnp.stack(cases, axis=0)


def _rope_tables(seq):
    half = HEAD_DIM // 2
    inv = ROPE_THETA ** (-jnp.arange(0, half, 2, dtype=F32) / half)
    t = jnp.arange(seq)
    ang_r = (t // GRID_W).astype(F32)[:, None] * inv[None, :]
    ang_c = (t % GRID_W).astype(F32)[:, None] * inv[None, :]
    cos_h = jnp.concatenate([jnp.cos(ang_r)] * 2 + [jnp.cos(ang_c)] * 2, axis=-1)
    sin_h = jnp.concatenate([-jnp.sin(ang_r), jnp.sin(ang_r), -jnp.sin(ang_c), jnp.sin(ang_c)], axis=-1)
    return jnp.tile(cos_h, (1, N_HEADS)), jnp.tile(sin_h, (1, N_HEADS))


def kernel(x, p, rel_bias, ln_mix_g, w_in, qk_gain, sink, c_norm_g, c_norm_b, c_ws, c_bs, out_gain, w_out,
           ln_ffn_g, w_up, conv_w, conv_b, w_down, ln_ple_g, w_ple_gate, w_ple_proj):
    b, s, _ = x.shape
    n = b * s
    depth = w_in.shape[0]
    cos_t, sin_t = _rope_tables(s)
    bd = (np.arange(GROUP_WIDTH)[:, None] // HEAD_DIM == np.arange(GROUP_WIDTH)[None, :] // HEAD_DIM)
    bd = jnp.asarray(bd / HEAD_DIM, BF16)
    table_a = rel_bias[:, :N_HEADS] * LOG2E
    table_b = rel_bias[:, N_HEADS:] * LOG2E
    q_scale = ATTN_SCALE * LOG2E
    bias_a = []
    for window, dil in DILATED_CFGS:
        radius = window // (2 * dil)
        length = s // dil
        tq = min(BAND_TQ, length)
        bias_a.append(_band_bias(table_a, radius, dil, tq, min(tq + 2 * radius, length)))
    tq_b = min(BAND_TQ, s)
    bias_b = _band_bias(table_b, SWA_RADIUS, 1, tq_b, min(tq_b + 2 * SWA_RADIUS, s))

    x2d = x.reshape(n, D_MODEL)
    for i in range(depth):
        w = w_in[i].astype(BF16)
        qg = qk_gain[i]
        vec = jnp.stack([
            jnp.tile(qg[0, 0], N_HEADS) * q_scale, jnp.tile(qg[0, 1], N_HEADS),
            jnp.tile(qg[1, 0], N_HEADS) * q_scale, jnp.tile(qg[1, 1], N_HEADS),
            jnp.tile(qg[2, 0], N_HEADS) * q_scale, jnp.tile(qg[2, 1], N_HEADS),
            c_norm_g[i], c_norm_b[i], out_gain[i, 2],
        ] + [jnp.zeros((GROUP_WIDTH,), F32)] * 7, axis=0)
        wcat = jnp.concatenate([c_ws[i, g] for g in range(C_GROUPS)], axis=1).astype(BF16)
        bsx = jnp.repeat(jnp.transpose(c_bs[i]), GROUP_WIDTH // C_GROUPS, axis=1)
        proj2d, d4, d16 = _proj_call(x2d, ln_mix_g[i][None], w, vec, cos_t, sin_t, bd, wcat, bsx, b, s)

        proj3d = proj2d.reshape(b, s, N_SEC * GROUP_WIDTH)
        ya = _dilated_call(proj3d, d4, d16, bias_a, out_gain[i, 0][None])
        sink_blk = jnp.full((N_HEADS * tq_b, LANES), NEG_INF, F32).at[:, 0].set(jnp.repeat(sink[i] * LOG2E, tq_b))
        yb = _swa_call(proj3d, bias_b, sink_blk, out_gain[i, 1][None])
        yd = _dense_call(proj3d, out_gain[i, 3][None])

        x1, xn = _mix_call(x2d, ya.reshape(n, GROUP_WIDTH), yb.reshape(n, GROUP_WIDTH), proj2d,
                           yd.reshape(n, GROUP_WIDTH), w_out[i].astype(BF16), ln_ffn_g[i][None])
        wup = w_up[i].astype(BF16)
        cw = jnp.concatenate([conv_w[i], conv_b[i][None]], axis=0)
        cw = jnp.transpose(cw.reshape(4, 2 * N_CHUNKS, FFN_TF), (1, 0, 2))
        wd = w_down[i].astype(BF16).reshape(N_CHUNKS, FFN_TF, D_MODEL)
        x2d = _ffn_call(xn, x1, p[i].reshape(n, PLE_DIM), wup, cw, wd, ln_ple_g[i][None],
                        w_ple_gate[i].astype(BF16), w_ple_proj[i].astype(BF16), s)
    return x2d.reshape(b, s, D_MODEL)
```

```python
import functools
import math

import jax
import jax.numpy as jnp
import numpy as np
from jax import lax
from jax.experimental import pallas as pl
from jax.experimental.pallas import tpu as pltpu

D_MODEL = 1024
HEAD_DIM = 64
GROUP_WIDTH = 256
N_HEADS = GROUP_WIDTH // HEAD_DIM
C_GROUPS = 4
C_CHUNK = 128
DILATED_CFGS = ((128, 1), (512, 4), (2048, 16))
SWA_RADIUS = 128
GRID_W = 64
ROPE_THETA = 10000.0
REL_BUCKETS = 32
REL_MAX_DIST = 1024
D_FF = 2816
PLE_DIM = 256
EPS = 1e-6
NEG_INF = -1e30
ATTN_SCALE = HEAD_DIM ** -0.5
LOG2E = math.log2(math.e)

BF16 = jnp.bfloat16
F32 = jnp.float32

SEC_AQ, SEC_AK, SEC_AV, SEC_BQ, SEC_BK, SEC_BV, SEC_YC, SEC_DQ, SEC_DK, SEC_DV = range(10)
N_SEC = 10
A_WIDTH = 3 * GROUP_WIDTH

VMEM_LIMIT = 56 * 1024 * 1024

KIN_TM = 1024
KIN_ROWS = 1024
BAND_TQ = 128
BAND_UNROLL = 16
SWA_UNROLL = 16
DENSE_TQ = 256
DENSE_ROWS = 1024
MIX_TM = 512
FFN_TM = 512
FFN_TF = 256
LANES = 128
SUB = 8
STRANDS = 8
HALO = 16
N_CHUNKS = D_FF // FFN_TF
FFN_UNROLL = True


def _params(sem):
    return pltpu.CompilerParams(dimension_semantics=sem, vmem_limit_bytes=VMEM_LIMIT)


def _resident(shape):
    zeros = (0,) * len(shape)
    return pl.BlockSpec(shape, lambda *_: zeros, pipeline_mode=pl.Buffered(1))


def _rms(t, gain):
    return t * lax.rsqrt(jnp.mean(t * t, axis=-1, keepdims=True) + EPS) * gain


def _head_id(shape):
    return lax.broadcasted_iota(jnp.int32, shape, 1) >> 6


def _proj_kernel(x_ref, g_ref, w_ref, vec_ref, cos_ref, sin_ref, bd_ref, wcat_ref, bsx_ref,
                 o_ref, d4_ref, d16_ref, a_ref):
    for t in range(x_ref.shape[0] // KIN_TM):
        _proj_tile(t, x_ref, g_ref, w_ref, vec_ref, cos_ref, sin_ref, bd_ref, wcat_ref, bsx_ref,
                   o_ref, d4_ref, d16_ref, a_ref)


def _proj_tile(t, x_ref, g_ref, w_ref, vec_ref, cos_ref, sin_ref, bd_ref, wcat_ref, bsx_ref,
               o_ref, d4_ref, d16_ref, a_ref):
    tm = KIN_TM
    r0 = t * tm
    x = x_ref[r0:r0 + tm, :]
    hn = _rms(x, g_ref[...]).astype(BF16)
    bd = bd_ref[...]
    slots = A_WIDTH // LANES

    def proj(sec):
        return jnp.dot(hn, w_ref[:, sec * GROUP_WIDTH:(sec + 1) * GROUP_WIDTH],
                       preferred_element_type=F32)

    def store(sec, val):
        o_ref[r0:r0 + tm, sec * GROUP_WIDTH:(sec + 1) * GROUP_WIDTH] = val.astype(BF16)

    def store_a(sec, val):
        store(sec, val)
        for half in range(GROUP_WIDTH // LANES):
            slot = sec * (GROUP_WIDTH // LANES) + half
            a_ref[t * slots + slot] = val[:, half * LANES:(half + 1) * LANES]
            cols = slice(slot * LANES, (slot + 1) * LANES)
            for dil, ref in ((4, d4_ref), (16, d16_ref)):
                for r in range(dil):
                    piece = a_ref[t * slots + slot, pl.ds(r, tm // dil, stride=dil), :]
                    ref[r, t * (tm // dil):(t + 1) * (tm // dil), cols] = piece.astype(BF16)

    def head_rms(t, row):
        width = t.shape[1]
        sq = t * t
        hi = sq.astype(BF16)
        lo = (sq - hi.astype(F32)).astype(BF16)
        blk = bd[0:width, 0:width]
        ms = (jnp.dot(hi, blk, preferred_element_type=F32) + jnp.dot(lo, blk, preferred_element_type=F32))
        return t * lax.rsqrt(ms + EPS) * vec_ref[row:row + 1, 0:width]

    def rope(t):
        blocks = []
        for c0 in range(0, t.shape[1], LANES):
            th = t[:, c0:c0 + LANES]
            lane = lax.broadcasted_iota(jnp.int32, th.shape, 1)
            partner = jnp.where((lane & 31) < 16, pltpu.roll(th, LANES - 16, axis=1), pltpu.roll(th, 16, axis=1))
            blocks.append(th * cos_ref[r0:r0 + tm, c0:c0 + LANES] + partner * sin_ref[r0:r0 + tm, c0:c0 + LANES])
        return blocks[0] if len(blocks) == 1 else jnp.concatenate(blocks, axis=-1)

    def rep_kv(t):
        swapped = pltpu.roll(t, HEAD_DIM, axis=1)
        low = lax.broadcasted_iota(jnp.int32, t.shape, 1) < HEAD_DIM
        return jnp.concatenate([jnp.where(low, t, swapped), jnp.where(low, swapped, t)], axis=-1)

    store_a(SEC_AQ, head_rms(proj(0), 0))
    store_a(SEC_AK, head_rms(proj(1), 1))
    store_a(SEC_AV, proj(2))
    store(SEC_BQ, head_rms(proj(3), 2))
    b_kv = proj(4)
    store(SEC_BK, rep_kv(head_rms(b_kv[:, 0:LANES], 3)))
    store(SEC_BV, rep_kv(b_kv[:, LANES:]))
    store(SEC_DQ, rope(head_rms(proj(7), 4)))
    d_kv = proj(8)
    store(SEC_DK, rep_kv(rope(head_rms(d_kv[:, 0:LANES], 5))))
    store(SEC_DV, jnp.concatenate([d_kv[:, LANES:]] * 2, axis=-1))

    u = jax.nn.gelu(proj(5))
    v = jax.nn.gelu(proj(6))
    mu = jnp.mean(v, axis=-1, keepdims=True)
    vc = v - mu
    var = jnp.mean(vc * vc, axis=-1, keepdims=True)
    vn = (vc * lax.rsqrt(var + EPS) * vec_ref[6:7, :] + vec_ref[7:8, :]).astype(BF16)
    gid = _head_id((C_CHUNK, GROUP_WIDTH))
    zero = jnp.zeros((C_CHUNK, GROUP_WIDTH), BF16)
    for c in range(tm // C_CHUNK):
        rows = slice(c * C_CHUNK, (c + 1) * C_CHUNK)
        vchunk = vn[rows]
        stacked = jnp.concatenate([jnp.where(gid == g, vchunk, zero) for g in range(C_GROUPS)], axis=0)
        mixed = jnp.dot(wcat_ref[...], stacked, preferred_element_type=F32) + bsx_ref[...]
        yc = _rms(u[rows] * mixed, vec_ref[8:9, :])
        o_ref[r0 + c * C_CHUNK:r0 + (c + 1) * C_CHUNK, SEC_YC * GROUP_WIDTH:(SEC_YC + 1) * GROUP_WIDTH] = yc.astype(BF16)


def _proj_call(x2d, g, w, vec, cos_t, sin_t, bd, wcat, bsx, batch, seq):
    n = x2d.shape[0]
    tps = seq // KIN_ROWS
    return pl.pallas_call(
        _proj_kernel,
        grid=(n // KIN_ROWS,),
        in_specs=[
            pl.BlockSpec((KIN_ROWS, D_MODEL), lambda i: (i, 0)),
            _resident((1, D_MODEL)),
            _resident(w.shape),
            _resident(vec.shape),
            pl.BlockSpec((KIN_ROWS, GROUP_WIDTH), lambda i: (i % tps, 0)),
            pl.BlockSpec((KIN_ROWS, GROUP_WIDTH), lambda i: (i % tps, 0)),
            _resident(bd.shape),
            _resident(wcat.shape),
            _resident(bsx.shape),
        ],
        out_specs=[
            pl.BlockSpec((KIN_ROWS, N_SEC * GROUP_WIDTH), lambda i: (i, 0)),
            pl.BlockSpec((None, 4, KIN_ROWS // 4, A_WIDTH), lambda i: (i // tps, 0, i % tps, 0)),
            pl.BlockSpec((None, 16, KIN_ROWS // 16, A_WIDTH), lambda i: (i // tps, 0, i % tps, 0)),
        ],
        out_shape=[
            jax.ShapeDtypeStruct((n, N_SEC * GROUP_WIDTH), BF16),
            jax.ShapeDtypeStruct((batch, 4, seq // 4, A_WIDTH), BF16),
            jax.ShapeDtypeStruct((batch, 16, seq // 16, A_WIDTH), BF16),
        ],
        scratch_shapes=[pltpu.VMEM((KIN_ROWS // KIN_TM * (A_WIDTH // LANES), KIN_TM, LANES), F32)],
        compiler_params=_params(("parallel",)),
        name="proj",
    )(x2d, g, w, vec, cos_t, sin_t, bd, wcat, bsx)


def _attend(q, kwin, vwin, bias_ref, case, sink_ref=None):
    tq = q.shape[0]
    hid = _head_id((tq, GROUP_WIDTH))
    qzero = jnp.zeros_like(q)
    q4 = jnp.concatenate([jnp.where(hid == h, q, qzero) for h in range(N_HEADS)], axis=0)
    s = lax.dot_general(q4, kwin, (((1,), (1,)), ((), ())), preferred_element_type=F32)
    s = s + bias_ref[case]
    if sink_ref is None:
        m = jnp.max(s, axis=-1, keepdims=True)
        p = jnp.exp2(s - m)
        den = jnp.sum(p, axis=-1, keepdims=True)
    else:
        sink = sink_ref[...]
        blocks = [s[:, c:c + LANES] for c in range(0, s.shape[1], LANES)]
        m = jnp.max(functools.reduce(jnp.maximum, blocks + [sink]), axis=-1, keepdims=True)
        p = jnp.exp2(s - m)
        pblocks = [p[:, c:c + LANES] for c in range(0, s.shape[1], LANES)]
        den = jnp.sum(functools.reduce(jnp.add, pblocks + [jnp.exp2(sink - m)]), axis=-1, keepdims=True)
    o4 = jnp.dot(p.astype(BF16), vwin, preferred_element_type=F32) * (1.0 / den)
    l4 = jnp.broadcast_to(m + jnp.log2(den), o4.shape)
    out = o4[0:tq]
    lse = l4[0:tq]
    for h in range(1, N_HEADS):
        out = jnp.where(hid == h, o4[h * tq:(h + 1) * tq], out)
        lse = jnp.where(hid == h, l4[h * tq:(h + 1) * tq], lse)
    return out, lse


def _band_window(j, tq, radius, length, kw):
    n_tiles = length // tq
    if n_tiles == 1:
        return 0, 0
    ks = pl.multiple_of(jnp.clip(j * tq - radius, 0, length - kw), 64)
    case = jnp.where(j == 0, 0, jnp.where(j == n_tiles - 1, 2, 1))
    return ks, case


def _dilated_kernel(q_ref, k_ref, v_ref, d4_ref, d16_ref, b1_ref, b4_ref, b16_ref, gain_ref, o_ref,
                    acc_ref, lse_ref, *, radii):
    seq = q_ref.shape[0]
    tq = BAND_TQ

    kw1 = min(tq + 2 * radii[0], seq)

    def body1(j, carry):
        q0 = pl.multiple_of(j * tq, tq)
        ks, case = _band_window(j, tq, radii[0], seq, kw1)
        o, l = _attend(q_ref[pl.ds(q0, tq), :], k_ref[pl.ds(ks, kw1), :], v_ref[pl.ds(ks, kw1), :], b1_ref, case)
        for half in range(GROUP_WIDTH // LANES):
            cols = slice(half * LANES, (half + 1) * LANES)
            acc_ref[half, pl.ds(q0, tq), :] = o[:, cols]
            lse_ref[half, pl.ds(q0, tq), :] = l[:, cols]
        return carry

    lax.fori_loop(0, seq // tq, body1, 0, unroll=BAND_UNROLL)

    for dil, ref, bias_ref, radius in ((4, d4_ref, b4_ref, radii[1]), (16, d16_ref, b16_ref, radii[2])):
        length = seq // dil
        tiles = length // tq
        kw = min(tq + 2 * radius, length)

        def body(it, carry, dil=dil, ref=ref, bias_ref=bias_ref, radius=radius, length=length, tiles=tiles, kw=kw):
            r = it // tiles
            j = it % tiles
            q0 = pl.multiple_of(j * tq, tq)
            ks, case = _band_window(j, tq, radius, length, kw)
            o, l = _attend(ref[r, pl.ds(q0, tq), 0:GROUP_WIDTH],
                           ref[r, pl.ds(ks, kw), GROUP_WIDTH:2 * GROUP_WIDTH],
                           ref[r, pl.ds(ks, kw), 2 * GROUP_WIDTH:3 * GROUP_WIDTH], bias_ref, case)
            rows = pl.ds(r + dil * q0, tq, stride=dil)
            for half in range(GROUP_WIDTH // LANES):
                cols = slice(half * LANES, (half + 1) * LANES)
                l_new = l[:, cols]
                l_old = lse_ref[half, rows, :]
                m = jnp.maximum(l_old, l_new)
                wa = jnp.exp2(l_old - m)
                wb = jnp.exp2(l_new - m)
                tot = wa + wb
                acc_ref[half, rows, :] = (wa * acc_ref[half, rows, :] + wb * o[:, cols]) * (1.0 / tot)
                lse_ref[half, rows, :] = m + jnp.log2(tot)
            return carry

        lax.fori_loop(0, dil * tiles, body, 0, unroll=BAND_UNROLL)

    acc = jnp.concatenate([acc_ref[0], acc_ref[1]], axis=-1)
    o_ref[...] = _rms(acc, gain_ref[...]).astype(BF16)


def _dilated_call(proj3d, d4, d16, biases, gain):
    b, seq, _ = proj3d.shape
    radii = tuple(window // (2 * dil) for window, dil in DILATED_CFGS)
    kern = functools.partial(_dilated_kernel, radii=radii)
    return pl.pallas_call(
        kern,
        grid=(b,),
        in_specs=[
            pl.BlockSpec((None, seq, GROUP_WIDTH), lambda bi: (bi, 0, SEC_AQ)),
            pl.BlockSpec((None, seq, GROUP_WIDTH), lambda bi: (bi, 0, SEC_AK)),
            pl.BlockSpec((None, seq, GROUP_WIDTH), lambda bi: (bi, 0, SEC_AV)),
            pl.BlockSpec((None,) + d4.shape[1:], lambda bi: (bi, 0, 0, 0)),
            pl.BlockSpec((None,) + d16.shape[1:], lambda bi: (bi, 0, 0, 0)),
            _resident(biases[0].shape), _resident(biases[1].shape), _resident(biases[2].shape),
            _resident((1, GROUP_WIDTH)),
        ],
        out_specs=pl.BlockSpec((None, seq, GROUP_WIDTH), lambda bi: (bi, 0, 0)),
        out_shape=jax.ShapeDtypeStruct((b, seq, GROUP_WIDTH), BF16),
        scratch_shapes=[pltpu.VMEM((GROUP_WIDTH // LANES, seq, LANES), F32)] * 2,
        compiler_params=_params(("parallel",)),
        name="dilated",
    )(proj3d, proj3d, proj3d, d4, d16, biases[0], biases[1], biases[2], gain)


def _swa_kernel(q_ref, k_ref, v_ref, bias_ref, sink_ref, gain_ref, o_ref):
    seq = q_ref.shape[0]
    tq = BAND_TQ
    kw = min(tq + 2 * SWA_RADIUS, seq)

    def body(j, carry):
        q0 = pl.multiple_of(j * tq, tq)
        ks, case = _band_window(j, tq, SWA_RADIUS, seq, kw)
        o, _ = _attend(q_ref[pl.ds(q0, tq), :], k_ref[pl.ds(ks, kw), :], v_ref[pl.ds(ks, kw), :],
                       bias_ref, case, sink_ref)
        o_ref[pl.ds(q0, tq), :] = _rms(o, gain_ref[...]).astype(BF16)
        return carry

    lax.fori_loop(0, seq // tq, body, 0, unroll=SWA_UNROLL)


def _swa_call(proj3d, bias, sink, gain):
    b, seq, _ = proj3d.shape
    return pl.pallas_call(
        _swa_kernel,
        grid=(b,),
        in_specs=[
            pl.BlockSpec((None, seq, GROUP_WIDTH), lambda bi: (bi, 0, SEC_BQ)),
            pl.BlockSpec((None, seq, GROUP_WIDTH), lambda bi: (bi, 0, SEC_BK)),
            pl.BlockSpec((None, seq, GROUP_WIDTH), lambda bi: (bi, 0, SEC_BV)),
            _resident(bias.shape),
            _resident(sink.shape),
            _resident((1, GROUP_WIDTH)),
        ],
        out_specs=pl.BlockSpec((None, seq, GROUP_WIDTH), lambda bi: (bi, 0, 0)),
        out_shape=jax.ShapeDtypeStruct((b, seq, GROUP_WIDTH), BF16),
        compiler_params=_params(("parallel",)),
        name="swa",
    )(proj3d, proj3d, proj3d, bias, sink, gain)


def _dense_kernel(q_ref, k_ref, v_ref, gain_ref, o_ref):
    k = k_ref[...]
    vext = jnp.concatenate([v_ref[...], jnp.ones((k.shape[0], LANES), BF16)], axis=-1)
    tq = DENSE_TQ
    hid = _head_id((tq, GROUP_WIDTH))
    low = lax.broadcasted_iota(jnp.int32, (tq, LANES), 1) < HEAD_DIM
    for t in range(q_ref.shape[0] // tq):
        q = q_ref[t * tq:(t + 1) * tq, :]
        qzero = jnp.zeros_like(q)
        heads = []
        for h in range(N_HEADS):
            qh = jnp.where(hid == h, q, qzero)
            s = lax.dot_general(qh, k, (((1,), (1,)), ((), ())), preferred_element_type=F32)
            m = jnp.max(s, axis=-1, keepdims=True)
            p = jnp.exp2((s - m).astype(BF16))
            o = jnp.dot(p, vext, preferred_element_type=F32)
            heads.append(o[:, 0:LANES] * (1.0 / o[:, LANES:]))
        blk0 = jnp.where(low, heads[0], pltpu.roll(heads[1], HEAD_DIM, axis=1))
        blk1 = jnp.where(low, pltpu.roll(heads[2], HEAD_DIM, axis=1), heads[3])
        acc = jnp.concatenate([blk0, blk1], axis=-1)
        o_ref[t * tq:(t + 1) * tq, :] = _rms(acc, gain_ref[...]).astype(BF16)


def _dense_call(proj3d, gain):
    b, seq, _ = proj3d.shape
    return pl.pallas_call(
        _dense_kernel,
        grid=(b, seq // DENSE_ROWS),
        in_specs=[
            pl.BlockSpec((None, DENSE_ROWS, GROUP_WIDTH), lambda bi, j: (bi, j, SEC_DQ)),
            pl.BlockSpec((None, seq, GROUP_WIDTH), lambda bi, j: (bi, 0, SEC_DK)),
            pl.BlockSpec((None, seq, LANES), lambda bi, j: (bi, 0, SEC_DV * (GROUP_WIDTH // LANES))),
            _resident((1, GROUP_WIDTH)),
        ],
        out_specs=pl.BlockSpec((None, DENSE_ROWS, GROUP_WIDTH), lambda bi, j: (bi, j, 0)),
        out_shape=jax.ShapeDtypeStruct((b, seq, GROUP_WIDTH), BF16),
        compiler_params=_params(("parallel", "arbitrary")),
        name="dense",
    )(proj3d, proj3d, proj3d, gain)


def _mix_kernel(x_ref, ya_ref, yb_ref, yc_ref, yd_ref, w_ref, g_ref, x1_ref, xn_ref, xs_ref):
    mixed = jnp.concatenate([ya_ref[...], yb_ref[...], yc_ref[...], yd_ref[...]], axis=-1)
    x1 = x_ref[...] + jnp.dot(mixed, w_ref[...], preferred_element_type=F32)
    x1_ref[...] = x1
    xn = _rms(x1, g_ref[...])
    tm = xn.shape[0]
    for c in range(D_MODEL // LANES):
        xs_ref[c] = xn[:, c * LANES:(c + 1) * LANES]
        col = jnp.concatenate([xs_ref[c, pl.ds(b, tm // STRANDS, stride=STRANDS), :] for b in range(STRANDS)],
                              axis=0)
        xn_ref[:, c * LANES:(c + 1) * LANES] = col.astype(BF16)


def _mix_call(x2d, ya, yb, proj2d, yd, w_out, g_ffn):
    n = x2d.shape[0]
    row = lambda i: (i, 0)
    grp = pl.BlockSpec((MIX_TM, GROUP_WIDTH), row)
    return pl.pallas_call(
        _mix_kernel,
        grid=(n // MIX_TM,),
        in_specs=[pl.BlockSpec((MIX_TM, D_MODEL), row), grp, grp,
                  pl.BlockSpec((MIX_TM, GROUP_WIDTH), lambda i: (i, SEC_YC)), grp,
                  _resident(w_out.shape), _resident((1, D_MODEL))],
        out_specs=[pl.BlockSpec((MIX_TM, D_MODEL), row), pl.BlockSpec((MIX_TM, D_MODEL), row)],
        out_shape=[jax.ShapeDtypeStruct((n, D_MODEL), F32), jax.ShapeDtypeStruct((n, D_MODEL), BF16)],
        scratch_shapes=[pltpu.VMEM((D_MODEL // LANES, MIX_TM, LANES), F32)],
        compiler_params=_params(("parallel",)),
        name="mix_out",
    )(x2d, ya, yb, proj2d, yd, w_out, g_ffn)


def _ffn_kernel(xn_ref, prev_ref, next_ref, x1_ref, p_ref, wup_ref, cw_ref, wd_ref, gple_ref, wgate_ref,
                wproj_ref, o_ref, xcat_ref, h0_ref, h1_ref, acc_ref, tok_ref, *, tiles_per_seq, n_tiles):
    step = pl.program_id(0)
    tm = xn_ref.shape[0]
    rows = tm + HALO
    sl = tm // STRANDS
    pitch = sl + SUB

    def up(k, h_ref):
        xcat = xcat_ref[...]
        for part in range(2):
            col = (k + part * N_CHUNKS) * FFN_TF
            if not isinstance(col, int):
                col = pl.multiple_of(col, FFN_TF)
            h = jnp.dot(xcat, wup_ref[:, pl.ds(col, FFN_TF)], preferred_element_type=F32)
            for b in range(STRANDS):
                h_ref[part, SUB + b * pitch:SUB + b * pitch + sl, :] = h[b * sl:(b + 1) * sl]
            h_ref[part, SUB + sl:SUB + pitch, :] = h[tm:tm + SUB]
            h_ref[part, (STRANDS - 1) * pitch:SUB + (STRANDS - 1) * pitch, :] = h[rows - SUB:rows]

    def strand(h_ref, part, b, shift=0):
        start = SUB + b * pitch + shift
        return h_ref[part, start:start + sl, :]

    def conv(h_ref, part, cw):
        out = []
        for b in range(STRANDS):
            before = strand(h_ref, part, b - 1) if b > 0 else strand(h_ref, part, STRANDS - 1, -1)
            after = strand(h_ref, part, b + 1) if b < STRANDS - 1 else strand(h_ref, part, 0, 1)
            out.append(before * cw[0:1, :] + strand(h_ref, part, b) * cw[1:2, :] + after * cw[2:3, :] + cw[3:4, :])
        return jnp.concatenate(out, axis=0)

    def down(k, h_ref):
        gt = conv(h_ref, 0, cw_ref[k])
        uu = conv(h_ref, 1, cw_ref[k + N_CHUNKS])
        act = (gt / (1.0 + jnp.exp(-gt)) * uu).astype(BF16)
        acc_ref[...] += jnp.dot(act, wd_ref[k], preferred_element_type=F32)

    def stage():
        pos = jnp.minimum(step, n_tiles - 1) % tiles_per_seq
        prev_blk = jnp.where(pos == 0, jnp.zeros_like(prev_ref[...]), prev_ref[...])
        next_blk = jnp.where(pos == tiles_per_seq - 1, jnp.zeros_like(next_ref[...]), next_ref[...])
        halo_row = lax.broadcasted_iota(jnp.int32, (HALO, D_MODEL), 0)
        xcat_ref[0:tm, :] = xn_ref[...]
        xcat_ref[tm:rows, :] = jnp.where(halo_row == 0, next_blk, prev_blk)
        up(0, h0_ref)

    @pl.when(step == 0)
    def _():
        stage()

    @pl.when(step > 0)
    def _():
        acc_ref[...] = jnp.zeros_like(acc_ref)

        def pair(t, carry):
            k = 2 * t
            up(k + 1, h1_ref)
            down(k, h0_ref)
            up(k + 2, h0_ref)
            down(k + 1, h1_ref)
            return carry

        lax.fori_loop(0, (N_CHUNKS - 1) // 2, pair, 0, unroll=FFN_UNROLL)
        down(N_CHUNKS - 1, h0_ref)

        stage()

        for c in range(D_MODEL // LANES):
            for b in range(STRANDS):
                tok_ref[c, pl.ds(b, sl, stride=STRANDS), :] = acc_ref[b * sl:(b + 1) * sl, c * LANES:(c + 1) * LANES]
        ffn = jnp.concatenate([tok_ref[c] for c in range(D_MODEL // LANES)], axis=-1)
        x2 = x1_ref[...] + ffn
        z = jnp.dot(_rms(x2, gple_ref[...]).astype(BF16), wgate_ref[...], preferred_element_type=F32)
        gate = 1.0 / (1.0 + jnp.exp(-z))
        inj = jnp.dot(p_ref[...].astype(BF16), wproj_ref[...], preferred_element_type=F32)
        o_ref[...] = x2 + inj * gate


def _ffn_call(xn, x1, p2d, wup, cw, wd, g_ple, w_gate, w_proj, seq):
    assert N_CHUNKS % 2 == 1
    assert MIX_TM == FFN_TM
    n = xn.shape[0]
    n_tiles = n // FFN_TM
    halo_per_tile = FFN_TM // HALO
    n_halo = n // HALO
    staged = lambda s: jnp.minimum(s, n_tiles - 1)
    done = lambda s: (jnp.maximum(s - 1, 0), 0)
    kern = functools.partial(_ffn_kernel, tiles_per_seq=seq // FFN_TM, n_tiles=n_tiles)
    return pl.pallas_call(
        kern,
        grid=(n_tiles + 1,),
        in_specs=[
            pl.BlockSpec((FFN_TM, D_MODEL), lambda s: (staged(s), 0)),
            pl.BlockSpec((HALO, D_MODEL), lambda s: (jnp.maximum(staged(s) * halo_per_tile - 1, 0), 0)),
            pl.BlockSpec((HALO, D_MODEL),
                         lambda s: (jnp.minimum((staged(s) + 1) * halo_per_tile, n_halo - 1), 0)),
            pl.BlockSpec((FFN_TM, D_MODEL), done),
            pl.BlockSpec((FFN_TM, PLE_DIM), done),
            _resident(wup.shape), _resident(cw.shape), _resident(wd.shape),
            _resident((1, D_MODEL)), _resident(w_gate.shape), _resident(w_proj.shape),
        ],
        out_specs=pl.BlockSpec((FFN_TM, D_MODEL), done),
        out_shape=jax.ShapeDtypeStruct((n, D_MODEL), F32),
        scratch_shapes=[pltpu.VMEM((FFN_TM + HALO, D_MODEL), BF16),
                        pltpu.VMEM((2, SUB + STRANDS * (FFN_TM // STRANDS + SUB), FFN_TF), F32),
                        pltpu.VMEM((2, SUB + STRANDS * (FFN_TM // STRANDS + SUB), FFN_TF), F32),
                        pltpu.VMEM((FFN_TM, D_MODEL), F32),
                        pltpu.VMEM((D_MODEL // LANES, FFN_TM, LANES), F32)],
        compiler_params=_params(("arbitrary",)),
        name="ffn_ple",
    )(xn, xn, xn, x1, p2d, wup, cw, wd, g_ple, w_gate, w_proj)


def _t5_bucket(rel):
    nb = REL_BUCKETS // 2
    ret = jnp.where(rel > 0, nb, 0)
    n = jnp.abs(rel)
    max_exact = nb // 2
    nf = jnp.maximum(n, 1).astype(F32)
    large = max_exact + (jnp.log(nf / max_exact) / math.log(REL_MAX_DIST / max_exact)
                         * (nb - max_exact)).astype(jnp.int32)
    large = jnp.minimum(large, nb - 1)
    return ret + jnp.where(n < max_exact, n, large)


def _band_bias(table, radius, dil, tq, kw):
    cases = []
    for off in (0, radius, kw - tq):
        rel = jnp.arange(kw)[None, :] - jnp.arange(tq)[:, None] - off
        onehot = (_t5_bucket(rel * dil)[..., None] == jnp.arange(REL_BUCKETS)).astype(F32)
        bias = jnp.einsum('qkn,nh->hqk', onehot, table.astype(F32), precision=lax.Precision.HIGHEST)
        bias = jnp.where((jnp.abs(rel) <= radius)[None], bias, NEG_INF)
        cases.append(bias.reshape(N_HEADS * tq, kw))
    return jnp.stack(cases, axis=0)


def _rope_tables(seq):
    half = HEAD_DIM // 2
    inv = ROPE_THETA ** (-jnp.arange(0, half, 2, dtype=F32) / half)
    t = jnp.arange(seq)
    ang_r = (t // GRID_W).astype(F32)[:, None] * inv[None, :]
    ang_c = (t % GRID_W).astype(F32)[:, None] * inv[None, :]
    cos_h = jnp.concatenate([jnp.cos(ang_r)] * 2 + [jnp.cos(ang_c)] * 2, axis=-1)
    sin_h = jnp.concatenate([-jnp.sin(ang_r), jnp.sin(ang_r), -jnp.sin(ang_c), jnp.sin(ang_c)], axis=-1)
    return jnp.tile(cos_h, (1, N_HEADS)), jnp.tile(sin_h, (1, N_HEADS))


def kernel(x, p, rel_bias, ln_mix_g, w_in, qk_gain, sink, c_norm_g, c_norm_b, c_ws, c_bs, out_gain, w_out,
           ln_ffn_g, w_up, conv_w, conv_b, w_down, ln_ple_g, w_ple_gate, w_ple_proj):
    b, s, _ = x.shape
    n = b * s
    depth = w_in.shape[0]
    cos_t, sin_t = _rope_tables(s)
    bd = (np.arange(GROUP_WIDTH)[:, None] // HEAD_DIM == np.arange(GROUP_WIDTH)[None, :] // HEAD_DIM)
    bd = jnp.asarray(bd / HEAD_DIM, BF16)
    table_a = rel_bias[:, :N_HEADS] * LOG2E
    table_b = rel_bias[:, N_HEADS:] * LOG2E
    q_scale = ATTN_SCALE * LOG2E
    bias_a = []
    for window, dil in DILATED_CFGS:
        radius = window // (2 * dil)
        length = s // dil
        tq = min(BAND_TQ, length)
        bias_a.append(_band_bias(table_a, radius, dil, tq, min(tq + 2 * radius, length)))
    tq_b = min(BAND_TQ, s)
    bias_b = _band_bias(table_b, SWA_RADIUS, 1, tq_b, min(tq_b + 2 * SWA_RADIUS, s))

    x2d = x.reshape(n, D_MODEL)
    for i in range(depth):
        w = w_in[i].astype(BF16)
        qg = qk_gain[i]
        vec = jnp.stack([
            jnp.tile(qg[0, 0], N_HEADS) * q_scale, jnp.tile(qg[0, 1], N_HEADS),
            jnp.tile(qg[1, 0], N_HEADS) * q_scale, jnp.tile(qg[1, 1], N_HEADS),
            jnp.tile(qg[2, 0], N_HEADS) * q_scale, jnp.tile(qg[2, 1], N_HEADS),
            c_norm_g[i], c_norm_b[i], out_gain[i, 2],
        ] + [jnp.zeros((GROUP_WIDTH,), F32)] * 7, axis=0)
        wcat = jnp.concatenate([c_ws[i, g] for g in range(C_GROUPS)], axis=1).astype(BF16)
        bsx = jnp.repeat(jnp.transpose(c_bs[i]), GROUP_WIDTH // C_GROUPS, axis=1)
        proj2d, d4, d16 = _proj_call(x2d, ln_mix_g[i][None], w, vec, cos_t, sin_t, bd, wcat, bsx, b, s)

        proj3d = proj2d.reshape(b, s, N_SEC * GROUP_WIDTH)
        ya = _dilated_call(proj3d, d4, d16, bias_a, out_gain[i, 0][None])
        sink_blk = jnp.full((N_HEADS * tq_b, LANES), NEG_INF, F32).at[:, 0].set(jnp.repeat(sink[i] * LOG2E, tq_b))
        yb = _swa_call(proj3d, bias_b, sink_blk, out_gain[i, 1][None])
        yd = _dense_call(proj3d, out_gain[i, 3][None])

        x1, xn = _mix_call(x2d, ya.reshape(n, GROUP_WIDTH), yb.reshape(n, GROUP_WIDTH), proj2d,
                           yd.reshape(n, GROUP_WIDTH), w_out[i].astype(BF16), ln_ffn_g[i][None])
        wup = w_up[i].astype(BF16)
        cw = jnp.concatenate([conv_w[i], conv_b[i][None]], axis=0)
        cw = jnp.transpose(cw.reshape(4, 2 * N_CHUNKS, FFN_TF), (1, 0, 2))
        wd = w_down[i].astype(BF16).reshape(N_CHUNKS, FFN_TF, D_MODEL)
        x2d = _ffn_call(xn, x1, p[i].reshape(n, PLE_DIM), wup, cw, wd, ln_ple_g[i][None],
                        w_ple_gate[i].astype(BF16), w_ple_proj[i].astype(BF16), s)
    return x2d.reshape(b, s, D_MODEL)
```

```python
import functools
import math

import jax
import jax.numpy as jnp
import numpy as np
from jax import lax
from jax.experimental import pallas as pl
from jax.experimental.pallas import tpu as pltpu

D_MODEL = 1024
HEAD_DIM = 64
GROUP_WIDTH = 256
N_HEADS = GROUP_WIDTH // HEAD_DIM
C_GROUPS = 4
C_CHUNK = 128
DILATED_CFGS = ((128, 1), (512, 4), (2048, 16))
SWA_RADIUS = 128
GRID_W = 64
ROPE_THETA = 10000.0
REL_BUCKETS = 32
REL_MAX_DIST = 1024
D_FF = 2816
PLE_DIM = 256
EPS = 1e-6
NEG_INF = -1e30
ATTN_SCALE = HEAD_DIM ** -0.5
LOG2E = math.log2(math.e)

BF16 = jnp.bfloat16
F32 = jnp.float32

SEC_AQ, SEC_AK, SEC_AV, SEC_BQ, SEC_BK, SEC_BV, SEC_YC, SEC_DQ, SEC_DK, SEC_DV = range(10)
N_SEC = 10
A_WIDTH = 3 * GROUP_WIDTH

VMEM_LIMIT = 56 * 1024 * 1024

KIN_TM = 1024
KIN_ROWS = 1024
BAND_TQ = 128
BAND_UNROLL = 16
SWA_UNROLL = 16
DENSE_TQ = 256
DENSE_ROWS = 1024
MIX_TM = 1024
FFN_TM = 512
FFN_TF = 256
LANES = 128
SUB = 8
STRANDS = 8
HALO = 16
N_CHUNKS = D_FF // FFN_TF
FFN_UNROLL = True


def _params(sem):
    return pltpu.CompilerParams(dimension_semantics=sem, vmem_limit_bytes=VMEM_LIMIT)


def _resident(shape):
    zeros = (0,) * len(shape)
    return pl.BlockSpec(shape, lambda *_: zeros, pipeline_mode=pl.Buffered(1))


def _rms(t, gain):
    return t * lax.rsqrt(jnp.mean(t * t, axis=-1, keepdims=True) + EPS) * gain


def _head_id(shape):
    return lax.broadcasted_iota(jnp.int32, shape, 1) >> 6


def _proj_kernel(x_ref, g_ref, w_ref, vec_ref, cos_ref, sin_ref, bd_ref, wcat_ref, bsx_ref,
                 o_ref, d4_ref, d16_ref, a_ref):
    for t in range(x_ref.shape[0] // KIN_TM):
        _proj_tile(t, x_ref, g_ref, w_ref, vec_ref, cos_ref, sin_ref, bd_ref, wcat_ref, bsx_ref,
                   o_ref, d4_ref, d16_ref, a_ref)


def _proj_tile(t, x_ref, g_ref, w_ref, vec_ref, cos_ref, sin_ref, bd_ref, wcat_ref, bsx_ref,
               o_ref, d4_ref, d16_ref, a_ref):
    tm = KIN_TM
    r0 = t * tm
    x = x_ref[r0:r0 + tm, :]
    hn = _rms(x, g_ref[...]).astype(BF16)
    bd = bd_ref[...]
    slots = A_WIDTH // LANES

    def proj(sec):
        return jnp.dot(hn, w_ref[:, sec * GROUP_WIDTH:(sec + 1) * GROUP_WIDTH],
                       preferred_element_type=F32)

    def store(sec, val):
        o_ref[r0:r0 + tm, sec * GROUP_WIDTH:(sec + 1) * GROUP_WIDTH] = val.astype(BF16)

    def store_a(sec, val):
        store(sec, val)
        for half in range(GROUP_WIDTH // LANES):
            slot = sec * (GROUP_WIDTH // LANES) + half
            a_ref[t * slots + slot] = val[:, half * LANES:(half + 1) * LANES]
            cols = slice(slot * LANES, (slot + 1) * LANES)
            for dil, ref in ((4, d4_ref), (16, d16_ref)):
                for r in range(dil):
                    piece = a_ref[t * slots + slot, pl.ds(r, tm // dil, stride=dil), :]
                    ref[r, t * (tm // dil):(t + 1) * (tm // dil), cols] = piece.astype(BF16)

    def head_rms(t, row):
        width = t.shape[1]
        sq = t * t
        hi = sq.astype(BF16)
        lo = (sq - hi.astype(F32)).astype(BF16)
        blk = bd[0:width, 0:width]
        ms = (jnp.dot(hi, blk, preferred_element_type=F32) + jnp.dot(lo, blk, preferred_element_type=F32))
        return t * lax.rsqrt(ms + EPS) * vec_ref[row:row + 1, 0:width]

    def rope(t):
        blocks = []
        for c0 in range(0, t.shape[1], LANES):
            th = t[:, c0:c0 + LANES]
            lane = lax.broadcasted_iota(jnp.int32, th.shape, 1)
            partner = jnp.where((lane & 31) < 16, pltpu.roll(th, LANES - 16, axis=1), pltpu.roll(th, 16, axis=1))
            blocks.append(th * cos_ref[r0:r0 + tm, c0:c0 + LANES] + partner * sin_ref[r0:r0 + tm, c0:c0 + LANES])
        return blocks[0] if len(blocks) == 1 else jnp.concatenate(blocks, axis=-1)

    def rep_kv(t):
        swapped = pltpu.roll(t, HEAD_DIM, axis=1)
        low = lax.broadcasted_iota(jnp.int32, t.shape, 1) < HEAD_DIM
        return jnp.concatenate([jnp.where(low, t, swapped), jnp.where(low, swapped, t)], axis=-1)

    store_a(SEC_AQ, head_rms(proj(0), 0))
    store_a(SEC_AK, head_rms(proj(1), 1))
    store_a(SEC_AV, proj(2))
    store(SEC_BQ, head_rms(proj(3), 2))
    b_kv = proj(4)
    store(SEC_BK, rep_kv(head_rms(b_kv[:, 0:LANES], 3)))
    store(SEC_BV, rep_kv(b_kv[:, LANES:]))
    store(SEC_DQ, rope(head_rms(proj(7), 4)))
    d_kv = proj(8)
    store(SEC_DK, rep_kv(rope(head_rms(d_kv[:, 0:LANES], 5))))
    store(SEC_DV, jnp.concatenate([d_kv[:, LANES:]] * 2, axis=-1))

    u = jax.nn.gelu(proj(5))
    v = jax.nn.gelu(proj(6))
    mu = jnp.mean(v, axis=-1, keepdims=True)
    vc = v - mu
    var = jnp.mean(vc * vc, axis=-1, keepdims=True)
    vn = (vc * lax.rsqrt(var + EPS) * vec_ref[6:7, :] + vec_ref[7:8, :]).astype(BF16)
    gid = _head_id((C_CHUNK, GROUP_WIDTH))
    zero = jnp.zeros((C_CHUNK, GROUP_WIDTH), BF16)
    for c in range(tm // C_CHUNK):
        rows = slice(c * C_CHUNK, (c + 1) * C_CHUNK)
        vchunk = vn[rows]
        stacked = jnp.concatenate([jnp.where(gid == g, vchunk, zero) for g in range(C_GROUPS)], axis=0)
        mixed = jnp.dot(wcat_ref[...], stacked, preferred_element_type=F32) + bsx_ref[...]
        yc = _rms(u[rows] * mixed, vec_ref[8:9, :])
        o_ref[r0 + c * C_CHUNK:r0 + (c + 1) * C_CHUNK, SEC_YC * GROUP_WIDTH:(SEC_YC + 1) * GROUP_WIDTH] = yc.astype(BF16)


def _proj_call(x2d, g, w, vec, cos_t, sin_t, bd, wcat, bsx, batch, seq):
    n = x2d.shape[0]
    tps = seq // KIN_ROWS
    return pl.pallas_call(
        _proj_kernel,
        grid=(n // KIN_ROWS,),
        in_specs=[
            pl.BlockSpec((KIN_ROWS, D_MODEL), lambda i: (i, 0)),
            _resident((1, D_MODEL)),
            _resident(w.shape),
            _resident(vec.shape),
            pl.BlockSpec((KIN_ROWS, GROUP_WIDTH), lambda i: (i % tps, 0)),
            pl.BlockSpec((KIN_ROWS, GROUP_WIDTH), lambda i: (i % tps, 0)),
            _resident(bd.shape),
            _resident(wcat.shape),
            _resident(bsx.shape),
        ],
        out_specs=[
            pl.BlockSpec((KIN_ROWS, N_SEC * GROUP_WIDTH), lambda i: (i, 0)),
            pl.BlockSpec((None, 4, KIN_ROWS // 4, A_WIDTH), lambda i: (i // tps, 0, i % tps, 0)),
            pl.BlockSpec((None, 16, KIN_ROWS // 16, A_WIDTH), lambda i: (i // tps, 0, i % tps, 0)),
        ],
        out_shape=[
            jax.ShapeDtypeStruct((n, N_SEC * GROUP_WIDTH), BF16),
            jax.ShapeDtypeStruct((batch, 4, seq // 4, A_WIDTH), BF16),
            jax.ShapeDtypeStruct((batch, 16, seq // 16, A_WIDTH), BF16),
        ],
        scratch_shapes=[pltpu.VMEM((KIN_ROWS // KIN_TM * (A_WIDTH // LANES), KIN_TM, LANES), F32)],
        compiler_params=_params(("parallel",)),
        name="proj",
    )(x2d, g, w, vec, cos_t, sin_t, bd, wcat, bsx)


def _attend(q, kwin, vwin, bias_ref, case, sink_ref=None):
    tq = q.shape[0]
    hid = _head_id((tq, GROUP_WIDTH))
    qzero = jnp.zeros_like(q)
    q4 = jnp.concatenate([jnp.where(hid == h, q, qzero) for h in range(N_HEADS)], axis=0)
    s = lax.dot_general(q4, kwin, (((1,), (1,)), ((), ())), preferred_element_type=F32)
    s = s + bias_ref[case]
    if sink_ref is None:
        m = jnp.max(s, axis=-1, keepdims=True)
        p = jnp.exp2(s - m)
        den = jnp.sum(p, axis=-1, keepdims=True)
    else:
        sink = sink_ref[...]
        blocks = [s[:, c:c + LANES] for c in range(0, s.shape[1], LANES)]
        m = jnp.max(functools.reduce(jnp.maximum, blocks + [sink]), axis=-1, keepdims=True)
        p = jnp.exp2(s - m)
        pblocks = [p[:, c:c + LANES] for c in range(0, s.shape[1], LANES)]
        den = jnp.sum(functools.reduce(jnp.add, pblocks + [jnp.exp2(sink - m)]), axis=-1, keepdims=True)
    o4 = jnp.dot(p.astype(BF16), vwin, preferred_element_type=F32) * (1.0 / den)
    l4 = jnp.broadcast_to(m + jnp.log2(den), o4.shape)
    out = o4[0:tq]
    lse = l4[0:tq]
    for h in range(1, N_HEADS):
        out = jnp.where(hid == h, o4[h * tq:(h + 1) * tq], out)
        lse = jnp.where(hid == h, l4[h * tq:(h + 1) * tq], lse)
    return out, lse


def _band_window(j, tq, radius, length, kw):
    n_tiles = length // tq
    if n_tiles == 1:
        return 0, 0
    ks = pl.multiple_of(jnp.clip(j * tq - radius, 0, length - kw), 64)
    case = jnp.where(j == 0, 0, jnp.where(j == n_tiles - 1, 2, 1))
    return ks, case


def _dilated_kernel(q_ref, k_ref, v_ref, d4_ref, d16_ref, b1_ref, b4_ref, b16_ref, gain_ref, o_ref,
                    acc_ref, lse_ref, *, radii):
    seq = q_ref.shape[0]
    tq = BAND_TQ

    kw1 = min(tq + 2 * radii[0], seq)

    def body1(j, carry):
        q0 = pl.multiple_of(j * tq, tq)
        ks, case = _band_window(j, tq, radii[0], seq, kw1)
        o, l = _attend(q_ref[pl.ds(q0, tq), :], k_ref[pl.ds(ks, kw1), :], v_ref[pl.ds(ks, kw1), :], b1_ref, case)
        for half in range(GROUP_WIDTH // LANES):
            cols = slice(half * LANES, (half + 1) * LANES)
            acc_ref[half, pl.ds(q0, tq), :] = o[:, cols]
            lse_ref[half, pl.ds(q0, tq), :] = l[:, cols]
        return carry

    lax.fori_loop(0, seq // tq, body1, 0, unroll=BAND_UNROLL)

    for dil, ref, bias_ref, radius in ((4, d4_ref, b4_ref, radii[1]), (16, d16_ref, b16_ref, radii[2])):
        length = seq // dil
        tiles = length // tq
        kw = min(tq + 2 * radius, length)

        def body(it, carry, dil=dil, ref=ref, bias_ref=bias_ref, radius=radius, length=length, tiles=tiles, kw=kw):
            r = it // tiles
            j = it % tiles
            q0 = pl.multiple_of(j * tq, tq)
            ks, case = _band_window(j, tq, radius, length, kw)
            o, l = _attend(ref[r, pl.ds(q0, tq), 0:GROUP_WIDTH],
                           ref[r, pl.ds(ks, kw), GROUP_WIDTH:2 * GROUP_WIDTH],
                           ref[r, pl.ds(ks, kw), 2 * GROUP_WIDTH:3 * GROUP_WIDTH], bias_ref, case)
            rows = pl.ds(r + dil * q0, tq, stride=dil)
            for half in range(GROUP_WIDTH // LANES):
                cols = slice(half * LANES, (half + 1) * LANES)
                l_new = l[:, cols]
                l_old = lse_ref[half, rows, :]
                m = jnp.maximum(l_old, l_new)
                wa = jnp.exp2(l_old - m)
                wb = jnp.exp2(l_new - m)
                tot = wa + wb
                acc_ref[half, rows, :] = (wa * acc_ref[half, rows, :] + wb * o[:, cols]) * (1.0 / tot)
                lse_ref[half, rows, :] = m + jnp.log2(tot)
            return carry

        lax.fori_loop(0, dil * tiles, body, 0, unroll=BAND_UNROLL)

    acc = jnp.concatenate([acc_ref[0], acc_ref[1]], axis=-1)
    o_ref[...] = _rms(acc, gain_ref[...]).astype(BF16)


def _dilated_call(proj3d, d4, d16, biases, gain):
    b, seq, _ = proj3d.shape
    radii = tuple(window // (2 * dil) for window, dil in DILATED_CFGS)
    kern = functools.partial(_dilated_kernel, radii=radii)
    return pl.pallas_call(
        kern,
        grid=(b,),
        in_specs=[
            pl.BlockSpec((None, seq, GROUP_WIDTH), lambda bi: (bi, 0, SEC_AQ)),
            pl.BlockSpec((None, seq, GROUP_WIDTH), lambda bi: (bi, 0, SEC_AK)),
            pl.BlockSpec((None, seq, GROUP_WIDTH), lambda bi: (bi, 0, SEC_AV)),
            pl.BlockSpec((None,) + d4.shape[1:], lambda bi: (bi, 0, 0, 0)),
            pl.BlockSpec((None,) + d16.shape[1:], lambda bi: (bi, 0, 0, 0)),
            _resident(biases[0].shape), _resident(biases[1].shape), _resident(biases[2].shape),
            _resident((1, GROUP_WIDTH)),
        ],
        out_specs=pl.BlockSpec((None, seq, GROUP_WIDTH), lambda bi: (bi, 0, 0)),
        out_shape=jax.ShapeDtypeStruct((b, seq, GROUP_WIDTH), BF16),
        scratch_shapes=[pltpu.VMEM((GROUP_WIDTH // LANES, seq, LANES), F32)] * 2,
        compiler_params=_params(("parallel",)),
        name="dilated",
    )(proj3d, proj3d, proj3d, d4, d16, biases[0], biases[1], biases[2], gain)


def _swa_kernel(q_ref, k_ref, v_ref, bias_ref, sink_ref, gain_ref, o_ref):
    seq = q_ref.shape[0]
    tq = BAND_TQ
    kw = min(tq + 2 * SWA_RADIUS, seq)

    def body(j, carry):
        q0 = pl.multiple_of(j * tq, tq)
        ks, case = _band_window(j, tq, SWA_RADIUS, seq, kw)
        o, _ = _attend(q_ref[pl.ds(q0, tq), :], k_ref[pl.ds(ks, kw), :], v_ref[pl.ds(ks, kw), :],
                       bias_ref, case, sink_ref)
        o_ref[pl.ds(q0, tq), :] = _rms(o, gain_ref[...]).astype(BF16)
        return carry

    lax.fori_loop(0, seq // tq, body, 0, unroll=SWA_UNROLL)


def _swa_call(proj3d, bias, sink, gain):
    b, seq, _ = proj3d.shape
    return pl.pallas_call(
        _swa_kernel,
        grid=(b,),
        in_specs=[
            pl.BlockSpec((None, seq, GROUP_WIDTH), lambda bi: (bi, 0, SEC_BQ)),
            pl.BlockSpec((None, seq, GROUP_WIDTH), lambda bi: (bi, 0, SEC_BK)),
            pl.BlockSpec((None, seq, GROUP_WIDTH), lambda bi: (bi, 0, SEC_BV)),
            _resident(bias.shape),
            _resident(sink.shape),
            _resident((1, GROUP_WIDTH)),
        ],
        out_specs=pl.BlockSpec((None, seq, GROUP_WIDTH), lambda bi: (bi, 0, 0)),
        out_shape=jax.ShapeDtypeStruct((b, seq, GROUP_WIDTH), BF16),
        compiler_params=_params(("parallel",)),
        name="swa",
    )(proj3d, proj3d, proj3d, bias, sink, gain)


def _dense_kernel(q_ref, k_ref, v_ref, gain_ref, o_ref):
    k = k_ref[...]
    vext = jnp.concatenate([v_ref[...], jnp.ones((k.shape[0], LANES), BF16)], axis=-1)
    tq = DENSE_TQ
    hid = _head_id((tq, GROUP_WIDTH))
    low = lax.broadcasted_iota(jnp.int32, (tq, LANES), 1) < HEAD_DIM
    for t in range(q_ref.shape[0] // tq):
        q = q_ref[t * tq:(t + 1) * tq, :]
        qzero = jnp.zeros_like(q)
        heads = []
        for h in range(N_HEADS):
            qh = jnp.where(hid == h, q, qzero)
            s = lax.dot_general(qh, k, (((1,), (1,)), ((), ())), preferred_element_type=F32)
            m = jnp.max(s, axis=-1, keepdims=True)
            p = jnp.exp2((s - m).astype(BF16))
            o = jnp.dot(p, vext, preferred_element_type=F32)
            heads.append(o[:, 0:LANES] * (1.0 / o[:, LANES:]))
        blk0 = jnp.where(low, heads[0], pltpu.roll(heads[1], HEAD_DIM, axis=1))
        blk1 = jnp.where(low, pltpu.roll(heads[2], HEAD_DIM, axis=1), heads[3])
        acc = jnp.concatenate([blk0, blk1], axis=-1)
        o_ref[t * tq:(t + 1) * tq, :] = _rms(acc, gain_ref[...]).astype(BF16)


def _dense_call(proj3d, gain):
    b, seq, _ = proj3d.shape
    return pl.pallas_call(
        _dense_kernel,
        grid=(b, seq // DENSE_ROWS),
        in_specs=[
            pl.BlockSpec((None, DENSE_ROWS, GROUP_WIDTH), lambda bi, j: (bi, j, SEC_DQ)),
            pl.BlockSpec((None, seq, GROUP_WIDTH), lambda bi, j: (bi, 0, SEC_DK)),
            pl.BlockSpec((None, seq, LANES), lambda bi, j: (bi, 0, SEC_DV * (GROUP_WIDTH // LANES))),
            _resident((1, GROUP_WIDTH)),
        ],
        out_specs=pl.BlockSpec((None, DENSE_ROWS, GROUP_WIDTH), lambda bi, j: (bi, j, 0)),
        out_shape=jax.ShapeDtypeStruct((b, seq, GROUP_WIDTH), BF16),
        compiler_params=_params(("parallel", "arbitrary")),
        name="dense",
    )(proj3d, proj3d, proj3d, gain)


def _mix_kernel(x_ref, ya_ref, yb_ref, yc_ref, yd_ref, w_ref, g_ref, x1_ref, xn_ref, xs_ref):
    mixed = jnp.concatenate([ya_ref[...], yb_ref[...], yc_ref[...], yd_ref[...]], axis=-1)
    x1 = x_ref[...] + jnp.dot(mixed, w_ref[...], preferred_element_type=F32)
    x1_ref[...] = x1
    xn = _rms(x1, g_ref[...])
    sl = FFN_TM // STRANDS
    for c in range(D_MODEL // LANES):
        xs_ref[c] = xn[:, c * LANES:(c + 1) * LANES]
        col = jnp.concatenate([xs_ref[c, pl.ds(r0 + b, sl, stride=STRANDS), :]
                               for r0 in range(0, xn.shape[0], FFN_TM) for b in range(STRANDS)], axis=0)
        xn_ref[:, c * LANES:(c + 1) * LANES] = col.astype(BF16)


def _mix_call(x2d, ya, yb, proj2d, yd, w_out, g_ffn):
    n = x2d.shape[0]
    row = lambda i: (i, 0)
    grp = pl.BlockSpec((MIX_TM, GROUP_WIDTH), row)
    return pl.pallas_call(
        _mix_kernel,
        grid=(n // MIX_TM,),
        in_specs=[pl.BlockSpec((MIX_TM, D_MODEL), row), grp, grp,
                  pl.BlockSpec((MIX_TM, GROUP_WIDTH), lambda i: (i, SEC_YC)), grp,
                  _resident(w_out.shape), _resident((1, D_MODEL))],
        out_specs=[pl.BlockSpec((MIX_TM, D_MODEL), row), pl.BlockSpec((MIX_TM, D_MODEL), row)],
        out_shape=[jax.ShapeDtypeStruct((n, D_MODEL), F32), jax.ShapeDtypeStruct((n, D_MODEL), BF16)],
        scratch_shapes=[pltpu.VMEM((D_MODEL // LANES, MIX_TM, LANES), F32)],
        compiler_params=_params(("parallel",)),
        name="mix_out",
    )(x2d, ya, yb, proj2d, yd, w_out, g_ffn)


def _ffn_kernel(xn_ref, prev_ref, next_ref, x1_ref, p_ref, wup_ref, cw_ref, wd_ref, gple_ref, wgate_ref,
                wproj_ref, o_ref, xcat_ref, h0_ref, h1_ref, acc_ref, tok_ref, *, tiles_per_seq, n_tiles):
    step = pl.program_id(0)
    tm = xn_ref.shape[0]
    rows = tm + HALO
    sl = tm // STRANDS
    pitch = sl + SUB

    def up(k, h_ref):
        xcat = xcat_ref[...]
        for part in range(2):
            col = (k + part * N_CHUNKS) * FFN_TF
            if not isinstance(col, int):
                col = pl.multiple_of(col, FFN_TF)
            h = jnp.dot(xcat, wup_ref[:, pl.ds(col, FFN_TF)], preferred_element_type=F32)
            for b in range(STRANDS):
                h_ref[part, SUB + b * pitch:SUB + b * pitch + sl, :] = h[b * sl:(b + 1) * sl]
            h_ref[part, SUB + sl:SUB + pitch, :] = h[tm:tm + SUB]
            h_ref[part, (STRANDS - 1) * pitch:SUB + (STRANDS - 1) * pitch, :] = h[rows - SUB:rows]

    def strand(h_ref, part, b, shift=0):
        start = SUB + b * pitch + shift
        return h_ref[part, start:start + sl, :]

    def conv(h_ref, part, cw):
        out = []
        for b in range(STRANDS):
            before = strand(h_ref, part, b - 1) if b > 0 else strand(h_ref, part, STRANDS - 1, -1)
            after = strand(h_ref, part, b + 1) if b < STRANDS - 1 else strand(h_ref, part, 0, 1)
            out.append(before * cw[0:1, :] + strand(h_ref, part, b) * cw[1:2, :] + after * cw[2:3, :] + cw[3:4, :])
        return jnp.concatenate(out, axis=0)

    def down(k, h_ref):
        gt = conv(h_ref, 0, cw_ref[k])
        uu = conv(h_ref, 1, cw_ref[k + N_CHUNKS])
        act = (gt / (1.0 + jnp.exp(-gt)) * uu).astype(BF16)
        acc_ref[...] += jnp.dot(act, wd_ref[k], preferred_element_type=F32)

    def stage():
        pos = jnp.minimum(step, n_tiles - 1) % tiles_per_seq
        prev_blk = jnp.where(pos == 0, jnp.zeros_like(prev_ref[...]), prev_ref[...])
        next_blk = jnp.where(pos == tiles_per_seq - 1, jnp.zeros_like(next_ref[...]), next_ref[...])
        halo_row = lax.broadcasted_iota(jnp.int32, (HALO, D_MODEL), 0)
        xcat_ref[0:tm, :] = xn_ref[...]
        xcat_ref[tm:rows, :] = jnp.where(halo_row == 0, next_blk, prev_blk)
        up(0, h0_ref)

    @pl.when(step == 0)
    def _():
        stage()

    @pl.when(step > 0)
    def _():
        acc_ref[...] = jnp.zeros_like(acc_ref)

        def pair(t, carry):
            k = 2 * t
            up(k + 1, h1_ref)
            down(k, h0_ref)
            up(k + 2, h0_ref)
            down(k + 1, h1_ref)
            return carry

        lax.fori_loop(0, (N_CHUNKS - 1) // 2, pair, 0, unroll=FFN_UNROLL)
        down(N_CHUNKS - 1, h0_ref)

        stage()

        for c in range(D_MODEL // LANES):
            for b in range(STRANDS):
                tok_ref[c, pl.ds(b, sl, stride=STRANDS), :] = acc_ref[b * sl:(b + 1) * sl, c * LANES:(c + 1) * LANES]
        ffn = jnp.concatenate([tok_ref[c] for c in range(D_MODEL // LANES)], axis=-1)
        x2 = x1_ref[...] + ffn
        z = jnp.dot(_rms(x2, gple_ref[...]).astype(BF16), wgate_ref[...], preferred_element_type=F32)
        gate = 1.0 / (1.0 + jnp.exp(-z))
        inj = jnp.dot(p_ref[...].astype(BF16), wproj_ref[...], preferred_element_type=F32)
        o_ref[...] = x2 + inj * gate


def _ffn_call(xn, x1, p2d, wup, cw, wd, g_ple, w_gate, w_proj, seq):
    assert N_CHUNKS % 2 == 1
    assert MIX_TM % FFN_TM == 0
    n = xn.shape[0]
    n_tiles = n // FFN_TM
    halo_per_tile = FFN_TM // HALO
    n_halo = n // HALO
    staged = lambda s: jnp.minimum(s, n_tiles - 1)
    done = lambda s: (jnp.maximum(s - 1, 0), 0)
    kern = functools.partial(_ffn_kernel, tiles_per_seq=seq // FFN_TM, n_tiles=n_tiles)
    return pl.pallas_call(
        kern,
        grid=(n_tiles + 1,),
        in_specs=[
            pl.BlockSpec((FFN_TM, D_MODEL), lambda s: (staged(s), 0)),
            pl.BlockSpec((HALO, D_MODEL), lambda s: (jnp.maximum(staged(s) * halo_per_tile - 1, 0), 0)),
            pl.BlockSpec((HALO, D_MODEL),
                         lambda s: (jnp.minimum((staged(s) + 1) * halo_per_tile, n_halo - 1), 0)),
            pl.BlockSpec((FFN_TM, D_MODEL), done),
            pl.BlockSpec((FFN_TM, PLE_DIM), done),
            _resident(wup.shape), _resident(cw.shape), _resident(wd.shape),
            _resident((1, D_MODEL)), _resident(w_gate.shape), _resident(w_proj.shape),
        ],
        out_specs=pl.BlockSpec((FFN_TM, D_MODEL), done),
        out_shape=jax.ShapeDtypeStruct((n, D_MODEL), F32),
        scratch_shapes=[pltpu.VMEM((FFN_TM + HALO, D_MODEL), BF16),
                        pltpu.VMEM((2, SUB + STRANDS * (FFN_TM // STRANDS + SUB), FFN_TF), F32),
                        pltpu.VMEM((2, SUB + STRANDS * (FFN_TM // STRANDS + SUB), FFN_TF), F32),
                        pltpu.VMEM((FFN_TM, D_MODEL), F32),
                        pltpu.VMEM((D_MODEL // LANES, FFN_TM, LANES), F32)],
        compiler_params=_params(("arbitrary",)),
        name="ffn_ple",
    )(xn, xn, xn, x1, p2d, wup, cw, wd, g_ple, w_gate, w_proj)


def _t5_bucket(rel):
    nb = REL_BUCKETS // 2
    ret = jnp.where(rel > 0, nb, 0)
    n = jnp.abs(rel)
    max_exact = nb // 2
    nf = jnp.maximum(n, 1).astype(F32)
    large = max_exact + (jnp.log(nf / max_exact) / math.log(REL_MAX_DIST / max_exact)
                         * (nb - max_exact)).astype(jnp.int32)
    large = jnp.minimum(large, nb - 1)
    return ret + jnp.where(n < max_exact, n, large)


def _band_bias(table, radius, dil, tq, kw):
    cases = []
    for off in (0, radius, kw - tq):
        rel = jnp.arange(kw)[None, :] - jnp.arange(tq)[:, None] - off
        onehot = (_t5_bucket(rel * dil)[..., None] == jnp.arange(REL_BUCKETS)).astype(F32)
        bias = jnp.einsum('qkn,nh->hqk', onehot, table.astype(F32), precision=lax.Precision.HIGHEST)
        bias = jnp.where((jnp.abs(rel) <= radius)[None], bias, NEG_INF)
        cases.append(bias.reshape(N_HEADS * tq, kw))
    return jnp.stack(cases, axis=0)


def _rope_tables(seq):
    half = HEAD_DIM // 2
    inv = ROPE_THETA ** (-jnp.arange(0, half, 2, dtype=F32) / half)
    t = jnp.arange(seq)
    ang_r = (t // GRID_W).astype(F32)[:, None] * inv[None, :]
    ang_c = (t % GRID_W).astype(F32)[:, None] * inv[None, :]
    cos_h = jnp.concatenate([jnp.cos(ang_r)] * 2 + [jnp.cos(ang_c)] * 2, axis=-1)
    sin_h = jnp.concatenate([-jnp.sin(ang_r), jnp.sin(ang_r), -jnp.sin(ang_c), jnp.sin(ang_c)], axis=-1)
    return jnp.tile(cos_h, (1, N_HEADS)), jnp.tile(sin_h, (1, N_HEADS))


def kernel(x, p, rel_bias, ln_mix_g, w_in, qk_gain, sink, c_norm_g, c_norm_b, c_ws, c_bs, out_gain, w_out,
           ln_ffn_g, w_up, conv_w, conv_b, w_down, ln_ple_g, w_ple_gate, w_ple_proj):
    b, s, _ = x.shape
    n = b * s
    depth = w_in.shape[0]
    cos_t, sin_t = _rope_tables(s)
    bd = (np.arange(GROUP_WIDTH)[:, None] // HEAD_DIM == np.arange(GROUP_WIDTH)[None, :] // HEAD_DIM)
    bd = jnp.asarray(bd / HEAD_DIM, BF16)
    table_a = rel_bias[:, :N_HEADS] * LOG2E
    table_b = rel_bias[:, N_HEADS:] * LOG2E
    q_scale = ATTN_SCALE * LOG2E
    bias_a = []
    for window, dil in DILATED_CFGS:
        radius = window // (2 * dil)
        length = s // dil
        tq = min(BAND_TQ, length)
        bias_a.append(_band_bias(table_a, radius, dil, tq, min(tq + 2 * radius, length)))
    tq_b = min(BAND_TQ, s)
    bias_b = _band_bias(table_b, SWA_RADIUS, 1, tq_b, min(tq_b + 2 * SWA_RADIUS, s))

    x2d = x.reshape(n, D_MODEL)
    for i in range(depth):
        w = w_in[i].astype(BF16)
        qg = qk_gain[i]
        vec = jnp.stack([
            jnp.tile(qg[0, 0], N_HEADS) * q_scale, jnp.tile(qg[0, 1], N_HEADS),
            jnp.tile(qg[1, 0], N_HEADS) * q_scale, jnp.tile(qg[1, 1], N_HEADS),
            jnp.tile(qg[2, 0], N_HEADS) * q_scale, jnp.tile(qg[2, 1], N_HEADS),
            c_norm_g[i], c_norm_b[i], out_gain[i, 2],
        ] + [jnp.zeros((GROUP_WIDTH,), F32)] * 7, axis=0)
        wcat = jnp.concatenate([c_ws[i, g] for g in range(C_GROUPS)], axis=1).astype(BF16)
        bsx = jnp.repeat(jnp.transpose(c_bs[i]), GROUP_WIDTH // C_GROUPS, axis=1)
        proj2d, d4, d16 = _proj_call(x2d, ln_mix_g[i][None], w, vec, cos_t, sin_t, bd, wcat, bsx, b, s)

        proj3d = proj2d.reshape(b, s, N_SEC * GROUP_WIDTH)
        ya = _dilated_call(proj3d, d4, d16, bias_a, out_gain[i, 0][None])
        sink_blk = jnp.full((N_HEADS * tq_b, LANES), NEG_INF, F32).at[:, 0].set(jnp.repeat(sink[i] * LOG2E, tq_b))
        yb = _swa_call(proj3d, bias_b, sink_blk, out_gain[i, 1][None])
        yd = _dense_call(proj3d, out_gain[i, 3][None])

        x1, xn = _mix_call(x2d, ya.reshape(n, GROUP_WIDTH), yb.reshape(n, GROUP_WIDTH), proj2d,
                           yd.reshape(n, GROUP_WIDTH), w_out[i].astype(BF16), ln_ffn_g[i][None])
        wup = w_up[i].astype(BF16)
        cw = jnp.concatenate([conv_w[i], conv_b[i][None]], axis=0)
        cw = jnp.transpose(cw.reshape(4, 2 * N_CHUNKS, FFN_TF), (1, 0, 2))
        wd = w_down[i].astype(BF16).reshape(N_CHUNKS, FFN_TF, D_MODEL)
        x2d = _ffn_call(xn, x1, p[i].reshape(n, PLE_DIM), wup, cw, wd, ln_ple_g[i][None],
                        w_ple_gate[i].astype(BF16), w_ple_proj[i].astype(BF16), s)
    return x2d.reshape(b, s, D_MODEL)
```

```python
import functools
import math

import jax
import jax.numpy as jnp
import numpy as np
from jax import lax
from jax.experimental import pallas as pl
from jax.experimental.pallas import tpu as pltpu

D_MODEL = 1024
HEAD_DIM = 64
GROUP_WIDTH = 256
N_HEADS = GROUP_WIDTH // HEAD_DIM
C_GROUPS = 4
C_CHUNK = 128
DILATED_CFGS = ((128, 1), (512, 4), (2048, 16))
SWA_RADIUS = 128
GRID_W = 64
ROPE_THETA = 10000.0
REL_BUCKETS = 32
REL_MAX_DIST = 1024
D_FF = 2816
PLE_DIM = 256
EPS = 1e-6
NEG_INF = -1e30
ATTN_SCALE = HEAD_DIM ** -0.5
LOG2E = math.log2(math.e)

BF16 = jnp.bfloat16
F32 = jnp.float32

SEC_AQ, SEC_AK, SEC_AV, SEC_BQ, SEC_BK, SEC_BV, SEC_YC, SEC_DQ, SEC_DK, SEC_DV = range(10)
N_SEC = 10
A_WIDTH = 3 * GROUP_WIDTH

VMEM_LIMIT = 56 * 1024 * 1024

KIN_TM = 1024
KIN_ROWS = 1024
BAND_TQ = 128
BAND_UNROLL = 16
SWA_UNROLL = 16
DENSE_TQ = 256
DENSE_ROWS = 1024
MIX_TM = 1024
FFN_TM = 512
FFN_TF = 256
LANES = 128
SUB = 8
STRANDS = 8
HALO = 16
N_CHUNKS = D_FF // FFN_TF
FFN_UNROLL = True


def _params(sem):
    return pltpu.CompilerParams(dimension_semantics=sem, vmem_limit_bytes=VMEM_LIMIT)


def _resident(shape):
    zeros = (0,) * len(shape)
    return pl.BlockSpec(shape, lambda *_: zeros, pipeline_mode=pl.Buffered(1))


def _rms(t, gain):
    return t * lax.rsqrt(jnp.mean(t * t, axis=-1, keepdims=True) + EPS) * gain


def _head_id(shape):
    return lax.broadcasted_iota(jnp.int32, shape, 1) >> 6


def _proj_kernel(x_ref, g_ref, w_ref, vec_ref, cos_ref, sin_ref, bd_ref, wcat_ref, bsx_ref,
                 o_ref, d4_ref, d16_ref, a_ref):
    for t in range(x_ref.shape[0] // KIN_TM):
        _proj_tile(t, x_ref, g_ref, w_ref, vec_ref, cos_ref, sin_ref, bd_ref, wcat_ref, bsx_ref,
                   o_ref, d4_ref, d16_ref, a_ref)


def _proj_tile(t, x_ref, g_ref, w_ref, vec_ref, cos_ref, sin_ref, bd_ref, wcat_ref, bsx_ref,
               o_ref, d4_ref, d16_ref, a_ref):
    tm = KIN_TM
    r0 = t * tm
    x = x_ref[r0:r0 + tm, :]
    hn = _rms(x, g_ref[...]).astype(BF16)
    bd = bd_ref[...]
    slots = A_WIDTH // LANES

    def proj(sec):
        return jnp.dot(hn, w_ref[:, sec * GROUP_WIDTH:(sec + 1) * GROUP_WIDTH],
                       preferred_element_type=F32)

    def store(sec, val):
        o_ref[r0:r0 + tm, sec * GROUP_WIDTH:(sec + 1) * GROUP_WIDTH] = val.astype(BF16)

    def store_a(sec, val):
        store(sec, val)
        for half in range(GROUP_WIDTH // LANES):
            slot = sec * (GROUP_WIDTH // LANES) + half
            a_ref[t * slots + slot] = val[:, half * LANES:(half + 1) * LANES]
            cols = slice(slot * LANES, (slot + 1) * LANES)
            for dil, ref in ((4, d4_ref), (16, d16_ref)):
                for r in range(dil):
                    piece = a_ref[t * slots + slot, pl.ds(r, tm // dil, stride=dil), :]
                    ref[r, t * (tm // dil):(t + 1) * (tm // dil), cols] = piece.astype(BF16)

    def head_rms(t, row):
        width = t.shape[1]
        sq = t * t
        hi = sq.astype(BF16)
        lo = (sq - hi.astype(F32)).astype(BF16)
        blk = bd[0:width, 0:width]
        ms = (jnp.dot(hi, blk, preferred_element_type=F32) + jnp.dot(lo, blk, preferred_element_type=F32))
        return t * lax.rsqrt(ms + EPS) * vec_ref[row:row + 1, 0:width]

    def rope(t):
        blocks = []
        for c0 in range(0, t.shape[1], LANES):
            th = t[:, c0:c0 + LANES]
            lane = lax.broadcasted_iota(jnp.int32, th.shape, 1)
            partner = jnp.where((lane & 31) < 16, pltpu.roll(th, LANES - 16, axis=1), pltpu.roll(th, 16, axis=1))
            blocks.append(th * cos_ref[r0:r0 + tm, c0:c0 + LANES] + partner * sin_ref[r0:r0 + tm, c0:c0 + LANES])
        return blocks[0] if len(blocks) == 1 else jnp.concatenate(blocks, axis=-1)

    def rep_kv(t):
        swapped = pltpu.roll(t, HEAD_DIM, axis=1)
        low = lax.broadcasted_iota(jnp.int32, t.shape, 1) < HEAD_DIM
        return jnp.concatenate([jnp.where(low, t, swapped), jnp.where(low, swapped, t)], axis=-1)

    store_a(SEC_AQ, head_rms(proj(0), 0))
    store_a(SEC_AK, head_rms(proj(1), 1))
    store_a(SEC_AV, proj(2))
    store(SEC_BQ, head_rms(proj(3), 2))
    b_kv = proj(4)
    store(SEC_BK, rep_kv(head_rms(b_kv[:, 0:LANES], 3)))
    store(SEC_BV, rep_kv(b_kv[:, LANES:]))
    store(SEC_DQ, rope(head_rms(proj(7), 4)))
    d_kv = proj(8)
    store(SEC_DK, rep_kv(rope(head_rms(d_kv[:, 0:LANES], 5))))
    store(SEC_DV, jnp.concatenate([d_kv[:, LANES:]] * 2, axis=-1))

    u = jax.nn.gelu(proj(5))
    v = jax.nn.gelu(proj(6))
    mu = jnp.mean(v, axis=-1, keepdims=True)
    vc = v - mu
    var = jnp.mean(vc * vc, axis=-1, keepdims=True)
    vn = (vc * lax.rsqrt(var + EPS) * vec_ref[6:7, :] + vec_ref[7:8, :]).astype(BF16)
    gid = _head_id((C_CHUNK, GROUP_WIDTH))
    zero = jnp.zeros((C_CHUNK, GROUP_WIDTH), BF16)
    for c in range(tm // C_CHUNK):
        rows = slice(c * C_CHUNK, (c + 1) * C_CHUNK)
        vchunk = vn[rows]
        stacked = jnp.concatenate([jnp.where(gid == g, vchunk, zero) for g in range(C_GROUPS)], axis=0)
        mixed = jnp.dot(wcat_ref[...], stacked, preferred_element_type=F32) + bsx_ref[...]
        yc = _rms(u[rows] * mixed, vec_ref[8:9, :])
        o_ref[r0 + c * C_CHUNK:r0 + (c + 1) * C_CHUNK, SEC_YC * GROUP_WIDTH:(SEC_YC + 1) * GROUP_WIDTH] = yc.astype(BF16)


def _proj_call(x2d, g, w, vec, cos_t, sin_t, bd, wcat, bsx, batch, seq):
    n = x2d.shape[0]
    tps = seq // KIN_ROWS
    return pl.pallas_call(
        _proj_kernel,
        grid=(n // KIN_ROWS,),
        in_specs=[
            pl.BlockSpec((KIN_ROWS, D_MODEL), lambda i: (i, 0)),
            _resident((1, D_MODEL)),
            _resident(w.shape),
            _resident(vec.shape),
            pl.BlockSpec((KIN_ROWS, GROUP_WIDTH), lambda i: (i % tps, 0)),
            pl.BlockSpec((KIN_ROWS, GROUP_WIDTH), lambda i: (i % tps, 0)),
            _resident(bd.shape),
            _resident(wcat.shape),
            _resident(bsx.shape),
        ],
        out_specs=[
            pl.BlockSpec((KIN_ROWS, N_SEC * GROUP_WIDTH), lambda i: (i, 0)),
            pl.BlockSpec((None, 4, KIN_ROWS // 4, A_WIDTH), lambda i: (i // tps, 0, i % tps, 0)),
            pl.BlockSpec((None, 16, KIN_ROWS // 16, A_WIDTH), lambda i: (i // tps, 0, i % tps, 0)),
        ],
        out_shape=[
            jax.ShapeDtypeStruct((n, N_SEC * GROUP_WIDTH), BF16),
            jax.ShapeDtypeStruct((batch, 4, seq // 4, A_WIDTH), BF16),
            jax.ShapeDtypeStruct((batch, 16, seq // 16, A_WIDTH), BF16),
        ],
        scratch_shapes=[pltpu.VMEM((KIN_ROWS // KIN_TM * (A_WIDTH // LANES), KIN_TM, LANES), F32)],
        compiler_params=_params(("parallel",)),
        name="proj",
    )(x2d, g, w, vec, cos_t, sin_t, bd, wcat, bsx)


def _attend(q, kwin, vwin, bias_ref, case, sink_ref=None):
    tq = q.shape[0]
    hid = _head_id((tq, GROUP_WIDTH))
    qzero = jnp.zeros_like(q)
    q4 = jnp.concatenate([jnp.where(hid == h, q, qzero) for h in range(N_HEADS)], axis=0)
    s = lax.dot_general(q4, kwin, (((1,), (1,)), ((), ())), preferred_element_type=F32)
    s = s + bias_ref[case]
    if sink_ref is None:
        m = jnp.max(s, axis=-1, keepdims=True)
        p = jnp.exp2(s - m)
        den = jnp.sum(p, axis=-1, keepdims=True)
    else:
        sink = sink_ref[...]
        blocks = [s[:, c:c + LANES] for c in range(0, s.shape[1], LANES)]
        m = jnp.max(functools.reduce(jnp.maximum, blocks + [sink]), axis=-1, keepdims=True)
        p = jnp.exp2(s - m)
        pblocks = [p[:, c:c + LANES] for c in range(0, s.shape[1], LANES)]
        den = jnp.sum(functools.reduce(jnp.add, pblocks + [jnp.exp2(sink - m)]), axis=-1, keepdims=True)
    o4 = jnp.dot(p.astype(BF16), vwin, preferred_element_type=F32) * (1.0 / den)
    l4 = jnp.broadcast_to(m + jnp.log2(den), o4.shape)
    out = o4[0:tq]
    lse = l4[0:tq]
    for h in range(1, N_HEADS):
        out = jnp.where(hid == h, o4[h * tq:(h + 1) * tq], out)
        lse = jnp.where(hid == h, l4[h * tq:(h + 1) * tq], lse)
    return out, lse


def _band_window(j, tq, radius, length, kw):
    n_tiles = length // tq
    if n_tiles == 1:
        return 0, 0
    ks = pl.multiple_of(jnp.clip(j * tq - radius, 0, length - kw), 64)
    case = jnp.where(j == 0, 0, jnp.where(j == n_tiles - 1, 2, 1))
    return ks, case


def _dilated_kernel(q_ref, k_ref, v_ref, d4_ref, d16_ref, b1_ref, b4_ref, b16_ref, gain_ref, o_ref,
                    acc_ref, lse_ref, *, radii):
    seq = q_ref.shape[0]
    tq = BAND_TQ

    kw1 = min(tq + 2 * radii[0], seq)

    def body1(j, carry):
        q0 = pl.multiple_of(j * tq, tq)
        ks, case = _band_window(j, tq, radii[0], seq, kw1)
        o, l = _attend(q_ref[pl.ds(q0, tq), :], k_ref[pl.ds(ks, kw1), :], v_ref[pl.ds(ks, kw1), :], b1_ref, case)
        for half in range(GROUP_WIDTH // LANES):
            cols = slice(half * LANES, (half + 1) * LANES)
            acc_ref[half, pl.ds(q0, tq), :] = o[:, cols]
            lse_ref[half, pl.ds(q0, tq), :] = l[:, cols]
        return carry

    lax.fori_loop(0, seq // tq, body1, 0, unroll=BAND_UNROLL)

    for dil, ref, bias_ref, radius in ((4, d4_ref, b4_ref, radii[1]), (16, d16_ref, b16_ref, radii[2])):
        length = seq // dil
        tiles = length // tq
        kw = min(tq + 2 * radius, length)

        def body(it, carry, dil=dil, ref=ref, bias_ref=bias_ref, radius=radius, length=length, tiles=tiles, kw=kw):
            r = it // tiles
            j = it % tiles
            q0 = pl.multiple_of(j * tq, tq)
            ks, case = _band_window(j, tq, radius, length, kw)
            o, l = _attend(ref[r, pl.ds(q0, tq), 0:GROUP_WIDTH],
                           ref[r, pl.ds(ks, kw), GROUP_WIDTH:2 * GROUP_WIDTH],
                           ref[r, pl.ds(ks, kw), 2 * GROUP_WIDTH:3 * GROUP_WIDTH], bias_ref, case)
            rows = pl.ds(r + dil * q0, tq, stride=dil)
            for half in range(GROUP_WIDTH // LANES):
                cols = slice(half * LANES, (half + 1) * LANES)
                l_new = l[:, cols]
                l_old = lse_ref[half, rows, :]
                m = jnp.maximum(l_old, l_new)
                wa = jnp.exp2(l_old - m)
                wb = jnp.exp2(l_new - m)
                tot = wa + wb
                acc_ref[half, rows, :] = (wa * acc_ref[half, rows, :] + wb * o[:, cols]) * (1.0 / tot)
                lse_ref[half, rows, :] = m + jnp.log2(tot)
            return carry

        lax.fori_loop(0, dil * tiles, body, 0, unroll=BAND_UNROLL)

    acc = jnp.concatenate([acc_ref[0], acc_ref[1]], axis=-1)
    o_ref[...] = _rms(acc, gain_ref[...]).astype(BF16)


def _dilated_call(proj3d, d4, d16, biases, gain):
    b, seq, _ = proj3d.shape
    radii = tuple(window // (2 * dil) for window, dil in DILATED_CFGS)
    kern = functools.partial(_dilated_kernel, radii=radii)
    return pl.pallas_call(
        kern,
        grid=(b,),
        in_specs=[
            pl.BlockSpec((None, seq, GROUP_WIDTH), lambda bi: (bi, 0, SEC_AQ)),
            pl.BlockSpec((None, seq, GROUP_WIDTH), lambda bi: (bi, 0, SEC_AK)),
            pl.BlockSpec((None, seq, GROUP_WIDTH), lambda bi: (bi, 0, SEC_AV)),
            pl.BlockSpec((None,) + d4.shape[1:], lambda bi: (bi, 0, 0, 0)),
            pl.BlockSpec((None,) + d16.shape[1:], lambda bi: (bi, 0, 0, 0)),
            _resident(biases[0].shape), _resident(biases[1].shape), _resident(biases[2].shape),
            _resident((1, GROUP_WIDTH)),
        ],
        out_specs=pl.BlockSpec((None, seq, GROUP_WIDTH), lambda bi: (bi, 0, 0)),
        out_shape=jax.ShapeDtypeStruct((b, seq, GROUP_WIDTH), BF16),
        scratch_shapes=[pltpu.VMEM((GROUP_WIDTH // LANES, seq, LANES), F32)] * 2,
        compiler_params=_params(("parallel",)),
        name="dilated",
    )(proj3d, proj3d, proj3d, d4, d16, biases[0], biases[1], biases[2], gain)


def _swa_kernel(q_ref, k_ref, v_ref, bias_ref, sink_ref, gain_ref, o_ref):
    seq = q_ref.shape[0]
    tq = BAND_TQ
    kw = min(tq + 2 * SWA_RADIUS, seq)

    def body(j, carry):
        q0 = pl.multiple_of(j * tq, tq)
        ks, case = _band_window(j, tq, SWA_RADIUS, seq, kw)
        o, _ = _attend(q_ref[pl.ds(q0, tq), :], k_ref[pl.ds(ks, kw), :], v_ref[pl.ds(ks, kw), :],
                       bias_ref, case, sink_ref)
        o_ref[pl.ds(q0, tq), :] = _rms(o, gain_ref[...]).astype(BF16)
        return carry

    lax.fori_loop(0, seq // tq, body, 0, unroll=SWA_UNROLL)


def _swa_call(proj3d, bias, sink, gain):
    b, seq, _ = proj3d.shape
    return pl.pallas_call(
        _swa_kernel,
        grid=(b,),
        in_specs=[
            pl.BlockSpec((None, seq, GROUP_WIDTH), lambda bi: (bi, 0, SEC_BQ)),
            pl.BlockSpec((None, seq, GROUP_WIDTH), lambda bi: (bi, 0, SEC_BK)),
            pl.BlockSpec((None, seq, GROUP_WIDTH), lambda bi: (bi, 0, SEC_BV)),
            _resident(bias.shape),
            _resident(sink.shape),
            _resident((1, GROUP_WIDTH)),
        ],
        out_specs=pl.BlockSpec((None, seq, GROUP_WIDTH), lambda bi: (bi, 0, 0)),
        out_shape=jax.ShapeDtypeStruct((b, seq, GROUP_WIDTH), BF16),
        compiler_params=_params(("parallel",)),
        name="swa",
    )(proj3d, proj3d, proj3d, bias, sink, gain)


def _dense_kernel(q_ref, k_ref, v_ref, gain_ref, o_ref):
    k = k_ref[...]
    vext = jnp.concatenate([v_ref[...], jnp.ones((k.shape[0], LANES), BF16)], axis=-1)
    tq = DENSE_TQ
    hid = _head_id((tq, GROUP_WIDTH))
    low = lax.broadcasted_iota(jnp.int32, (tq, LANES), 1) < HEAD_DIM
    for t in range(q_ref.shape[0] // tq):
        q = q_ref[t * tq:(t + 1) * tq, :]
        qzero = jnp.zeros_like(q)
        heads = []
        for h in range(N_HEADS):
            qh = jnp.where(hid == h, q, qzero)
            s = lax.dot_general(qh, k, (((1,), (1,)), ((), ())), preferred_element_type=F32)
            m = jnp.max(s, axis=-1, keepdims=True)
            p = jnp.exp2((s - m).astype(BF16))
            o = jnp.dot(p, vext, preferred_element_type=F32)
            heads.append(o[:, 0:LANES] * (1.0 / o[:, LANES:]))
        blk0 = jnp.where(low, heads[0], pltpu.roll(heads[1], HEAD_DIM, axis=1))
        blk1 = jnp.where(low, pltpu.roll(heads[2], HEAD_DIM, axis=1), heads[3])
        acc = jnp.concatenate([blk0, blk1], axis=-1)
        o_ref[t * tq:(t + 1) * tq, :] = _rms(acc, gain_ref[...]).astype(BF16)


def _dense_call(proj3d, gain):
    b, seq, _ = proj3d.shape
    return pl.pallas_call(
        _dense_kernel,
        grid=(b, seq // DENSE_ROWS),
        in_specs=[
            pl.BlockSpec((None, DENSE_ROWS, GROUP_WIDTH), lambda bi, j: (bi, j, SEC_DQ)),
            pl.BlockSpec((None, seq, GROUP_WIDTH), lambda bi, j: (bi, 0, SEC_DK)),
            pl.BlockSpec((None, seq, LANES), lambda bi, j: (bi, 0, SEC_DV * (GROUP_WIDTH // LANES))),
            _resident((1, GROUP_WIDTH)),
        ],
        out_specs=pl.BlockSpec((None, DENSE_ROWS, GROUP_WIDTH), lambda bi, j: (bi, j, 0)),
        out_shape=jax.ShapeDtypeStruct((b, seq, GROUP_WIDTH), BF16),
        compiler_params=_params(("parallel", "arbitrary")),
        name="dense",
    )(proj3d, proj3d, proj3d, gain)


def _mix_kernel(x_ref, ya_ref, yb_ref, yc_ref, yd_ref, w_ref, g_ref, x1_ref, xn_ref, xs_ref):
    mixed = jnp.concatenate([ya_ref[...], yb_ref[...], yc_ref[...], yd_ref[...]], axis=-1)
    x1 = x_ref[...] + jnp.dot(mixed, w_ref[...], preferred_element_type=F32)
    x1_ref[...] = x1
    xn = _rms(x1, g_ref[...])
    sl = FFN_TM // STRANDS
    for c in range(D_MODEL // LANES):
        xs_ref[c] = xn[:, c * LANES:(c + 1) * LANES]
        col = jnp.concatenate([xs_ref[c, pl.ds(r0 + b, sl, stride=STRANDS), :]
                               for r0 in range(0, xn.shape[0], FFN_TM) for b in range(STRANDS)], axis=0)
        xn_ref[:, c * LANES:(c + 1) * LANES] = col.astype(BF16)


def _mix_call(x2d, ya, yb, proj2d, yd, w_out, g_ffn):
    n = x2d.shape[0]
    row = lambda i: (i, 0)
    grp = pl.BlockSpec((MIX_TM, GROUP_WIDTH), row)
    return pl.pallas_call(
        _mix_kernel,
        grid=(n // MIX_TM,),
        in_specs=[pl.BlockSpec((MIX_TM, D_MODEL), row), grp, grp,
                  pl.BlockSpec((MIX_TM, GROUP_WIDTH), lambda i: (i, SEC_YC)), grp,
                  _resident(w_out.shape), _resident((1, D_MODEL))],
        out_specs=[pl.BlockSpec((MIX_TM, D_MODEL), row), pl.BlockSpec((MIX_TM, D_MODEL), row)],
        out_shape=[jax.ShapeDtypeStruct((n, D_MODEL), F32), jax.ShapeDtypeStruct((n, D_MODEL), BF16)],
        scratch_shapes=[pltpu.VMEM((D_MODEL // LANES, MIX_TM, LANES), F32)],
        compiler_params=_params(("parallel",)),
        name="mix_out",
    )(x2d, ya, yb, proj2d, yd, w_out, g_ffn)


def _ffn_kernel(xn_ref, prev_ref, next_ref, x1_ref, p_ref, wup_ref, cw_ref, wd_ref, gple_ref, wgate_ref,
                wproj_ref, o_ref, xcat_ref, h0_ref, h1_ref, acc_ref, tok_ref, *, tiles_per_seq, n_tiles):
    step = pl.program_id(0)
    tm = xn_ref.shape[0]
    rows = tm + HALO
    sl = tm // STRANDS
    pitch = sl + SUB

    def up(k, h_ref):
        xcat = xcat_ref[...]
        for part in range(2):
            col = (k + part * N_CHUNKS) * FFN_TF
            if not isinstance(col, int):
                col = pl.multiple_of(col, FFN_TF)
            h = jnp.dot(xcat, wup_ref[:, pl.ds(col, FFN_TF)], preferred_element_type=F32)
            for b in range(STRANDS):
                h_ref[part, SUB + b * pitch:SUB + b * pitch + sl, :] = h[b * sl:(b + 1) * sl]
            h_ref[part, SUB + sl:SUB + pitch, :] = h[tm:tm + SUB]
            h_ref[part, (STRANDS - 1) * pitch:SUB + (STRANDS - 1) * pitch, :] = h[rows - SUB:rows]

    def strand(h_ref, part, b, shift=0):
        start = SUB + b * pitch + shift
        return h_ref[part, start:start + sl, :]

    def conv(h_ref, part, cw):
        out = []
        for b in range(STRANDS):
            before = strand(h_ref, part, b - 1) if b > 0 else strand(h_ref, part, STRANDS - 1, -1)
            after = strand(h_ref, part, b + 1) if b < STRANDS - 1 else strand(h_ref, part, 0, 1)
            out.append(before * cw[0:1, :] + strand(h_ref, part, b) * cw[1:2, :] + after * cw[2:3, :] + cw[3:4, :])
        return jnp.concatenate(out, axis=0)

    def gated(k, h_ref):
        gt = conv(h_ref, 0, cw_ref[k])
        uu = conv(h_ref, 1, cw_ref[k + N_CHUNKS])
        return (gt / (1.0 + jnp.exp(-gt)) * uu).astype(BF16)

    def down(k, h_ref):
        acc_ref[...] += jnp.dot(gated(k, h_ref), wd_ref[k], preferred_element_type=F32)

    def stage():
        pos = jnp.minimum(step, n_tiles - 1) % tiles_per_seq
        prev_blk = jnp.where(pos == 0, jnp.zeros_like(prev_ref[...]), prev_ref[...])
        next_blk = jnp.where(pos == tiles_per_seq - 1, jnp.zeros_like(next_ref[...]), next_ref[...])
        halo_row = lax.broadcasted_iota(jnp.int32, (HALO, D_MODEL), 0)
        xcat_ref[0:tm, :] = xn_ref[...]
        xcat_ref[tm:rows, :] = jnp.where(halo_row == 0, next_blk, prev_blk)
        up(0, h0_ref)

    @pl.when(step == 0)
    def _():
        stage()

    @pl.when(step > 0)
    def _():
        acc_ref[...] = jnp.zeros_like(acc_ref)

        def pair(t, carry):
            k = 2 * t
            up(k + 1, h1_ref)
            act_a = gated(k, h0_ref)
            up(k + 2, h0_ref)
            act_b = gated(k + 1, h1_ref)
            wd_pair = wd_ref[pl.ds(k, 2)].reshape(2 * FFN_TF, D_MODEL)
            acc_ref[...] += jnp.dot(jnp.concatenate([act_a, act_b], axis=-1), wd_pair,
                                    preferred_element_type=F32)
            return carry

        lax.fori_loop(0, (N_CHUNKS - 1) // 2, pair, 0, unroll=FFN_UNROLL)
        down(N_CHUNKS - 1, h0_ref)

        stage()

        for c in range(D_MODEL // LANES):
            for b in range(STRANDS):
                tok_ref[c, pl.ds(b, sl, stride=STRANDS), :] = acc_ref[b * sl:(b + 1) * sl, c * LANES:(c + 1) * LANES]
        ffn = jnp.concatenate([tok_ref[c] for c in range(D_MODEL // LANES)], axis=-1)
        x2 = x1_ref[...] + ffn
        z = jnp.dot(_rms(x2, gple_ref[...]).astype(BF16), wgate_ref[...], preferred_element_type=F32)
        gate = 1.0 / (1.0 + jnp.exp(-z))
        inj = jnp.dot(p_ref[...].astype(BF16), wproj_ref[...], preferred_element_type=F32)
        o_ref[...] = x2 + inj * gate


def _ffn_call(xn, x1, p2d, wup, cw, wd, g_ple, w_gate, w_proj, seq):
    assert N_CHUNKS % 2 == 1
    assert MIX_TM % FFN_TM == 0
    n = xn.shape[0]
    n_tiles = n // FFN_TM
    halo_per_tile = FFN_TM // HALO
    n_halo = n // HALO
    staged = lambda s: jnp.minimum(s, n_tiles - 1)
    done = lambda s: (jnp.maximum(s - 1, 0), 0)
    kern = functools.partial(_ffn_kernel, tiles_per_seq=seq // FFN_TM, n_tiles=n_tiles)
    return pl.pallas_call(
        kern,
        grid=(n_tiles + 1,),
        in_specs=[
            pl.BlockSpec((FFN_TM, D_MODEL), lambda s: (staged(s), 0)),
            pl.BlockSpec((HALO, D_MODEL), lambda s: (jnp.maximum(staged(s) * halo_per_tile - 1, 0), 0)),
            pl.BlockSpec((HALO, D_MODEL),
                         lambda s: (jnp.minimum((staged(s) + 1) * halo_per_tile, n_halo - 1), 0)),
            pl.BlockSpec((FFN_TM, D_MODEL), done),
            pl.BlockSpec((FFN_TM, PLE_DIM), done),
            _resident(wup.shape), _resident(cw.shape), _resident(wd.shape),
            _resident((1, D_MODEL)), _resident(w_gate.shape), _resident(w_proj.shape),
        ],
        out_specs=pl.BlockSpec((FFN_TM, D_MODEL), done),
        out_shape=jax.ShapeDtypeStruct((n, D_MODEL), F32),
        scratch_shapes=[pltpu.VMEM((FFN_TM + HALO, D_MODEL), BF16),
                        pltpu.VMEM((2, SUB + STRANDS * (FFN_TM // STRANDS + SUB), FFN_TF), F32),
                        pltpu.VMEM((2, SUB + STRANDS * (FFN_TM // STRANDS + SUB), FFN_TF), F32),
                        pltpu.VMEM((FFN_TM, D_MODEL), F32),
                        pltpu.VMEM((D_MODEL // LANES, FFN_TM, LANES), F32)],
        compiler_params=_params(("arbitrary",)),
        name="ffn_ple",
    )(xn, xn, xn, x1, p2d, wup, cw, wd, g_ple, w_gate, w_proj)


def _t5_bucket(rel):
    nb = REL_BUCKETS // 2
    ret = jnp.where(rel > 0, nb, 0)
    n = jnp.abs(rel)
    max_exact = nb // 2
    nf = jnp.maximum(n, 1).astype(F32)
    large = max_exact + (jnp.log(nf / max_exact) / math.log(REL_MAX_DIST / max_exact)
                         * (nb - max_exact)).astype(jnp.int32)
    large = jnp.minimum(large, nb - 1)
    return ret + jnp.where(n < max_exact, n, large)


def _band_bias(table, radius, dil, tq, kw):
    cases = []
    for off in (0, radius, kw - tq):
        rel = jnp.arange(kw)[None, :] - jnp.arange(tq)[:, None] - off
        onehot = (_t5_bucket(rel * dil)[..., None] == jnp.arange(REL_BUCKETS)).astype(F32)
        bias = jnp.einsum('qkn,nh->hqk', onehot, table.astype(F32), precision=lax.Precision.HIGHEST)
        bias = jnp.where((jnp.abs(rel) <= radius)[None], bias, NEG_INF)
        cases.append(bias.reshape(N_HEADS * tq, kw))
    return jnp.stack(cases, axis=0)


def _rope_tables(seq):
    half = HEAD_DIM // 2
    inv = ROPE_THETA ** (-jnp.arange(0, half, 2, dtype=F32) / half)
    t = jnp.arange(seq)
    ang_r = (t // GRID_W).astype(F32)[:, None] * inv[None, :]
    ang_c = (t % GRID_W).astype(F32)[:, None] * inv[None, :]
    cos_h = jnp.concatenate([jnp.cos(ang_r)] * 2 + [jnp.cos(ang_c)] * 2, axis=-1)
    sin_h = jnp.concatenate([-jnp.sin(ang_r), jnp.sin(ang_r), -jnp.sin(ang_c), jnp.sin(ang_c)], axis=-1)
    return jnp.tile(cos_h, (1, N_HEADS)), jnp.tile(sin_h, (1, N_HEADS))


def kernel(x, p, rel_bias, ln_mix_g, w_in, qk_gain, sink, c_norm_g, c_norm_b, c_ws, c_bs, out_gain, w_out,
           ln_ffn_g, w_up, conv_w, conv_b, w_down, ln_ple_g, w_ple_gate, w_ple_proj):
    b, s, _ = x.shape
    n = b * s
    depth = w_in.shape[0]
    cos_t, sin_t = _rope_tables(s)
    bd = (np.arange(GROUP_WIDTH)[:, None] // HEAD_DIM == np.arange(GROUP_WIDTH)[None, :] // HEAD_DIM)
    bd = jnp.asarray(bd / HEAD_DIM, BF16)
    table_a = rel_bias[:, :N_HEADS] * LOG2E
    table_b = rel_bias[:, N_HEADS:] * LOG2E
    q_scale = ATTN_SCALE * LOG2E
    bias_a = []
    for window, dil in DILATED_CFGS:
        radius = window // (2 * dil)
        length = s // dil
        tq = min(BAND_TQ, length)
        bias_a.append(_band_bias(table_a, radius, dil, tq, min(tq + 2 * radius, length)))
    tq_b = min(BAND_TQ, s)
    bias_b = _band_bias(table_b, SWA_RADIUS, 1, tq_b, min(tq_b + 2 * SWA_RADIUS, s))

    x2d = x.reshape(n, D_MODEL)
    for i in range(depth):
        w = w_in[i].astype(BF16)
        qg = qk_gain[i]
        vec = jnp.stack([
            jnp.tile(qg[0, 0], N_HEADS) * q_scale, jnp.tile(qg[0, 1], N_HEADS),
            jnp.tile(qg[1, 0], N_HEADS) * q_scale, jnp.tile(qg[1, 1], N_HEADS),
            jnp.tile(qg[2, 0], N_HEADS) * q_scale, jnp.tile(qg[2, 1], N_HEADS),
            c_norm_g[i], c_norm_b[i], out_gain[i, 2],
        ] + [jnp.zeros((GROUP_WIDTH,), F32)] * 7, axis=0)
        wcat = jnp.concatenate([c_ws[i, g] for g in range(C_GROUPS)], axis=1).astype(BF16)
        bsx = jnp.repeat(jnp.transpose(c_bs[i]), GROUP_WIDTH // C_GROUPS, axis=1)
        proj2d, d4, d16 = _proj_call(x2d, ln_mix_g[i][None], w, vec, cos_t, sin_t, bd, wcat, bsx, b, s)

        proj3d = proj2d.reshape(b, s, N_SEC * GROUP_WIDTH)
        ya = _dilated_call(proj3d, d4, d16, bias_a, out_gain[i, 0][None])
        sink_blk = jnp.full((N_HEADS * tq_b, LANES), NEG_INF, F32).at[:, 0].set(jnp.repeat(sink[i] * LOG2E, tq_b))
        yb = _swa_call(proj3d, bias_b, sink_blk, out_gain[i, 1][None])
        yd = _dense_call(proj3d, out_gain[i, 3][None])

        x1, xn = _mix_call(x2d, ya.reshape(n, GROUP_WIDTH), yb.reshape(n, GROUP_WIDTH), proj2d,
                           yd.reshape(n, GROUP_WIDTH), w_out[i].astype(BF16), ln_ffn_g[i][None])
        wup = w_up[i].astype(BF16)
        cw = jnp.concatenate([conv_w[i], conv_b[i][None]], axis=0)
        cw = jnp.transpose(cw.reshape(4, 2 * N_CHUNKS, FFN_TF), (1, 0, 2))
        wd = w_down[i].astype(BF16).reshape(N_CHUNKS, FFN_TF, D_MODEL)
        x2d = _ffn_call(xn, x1, p[i].reshape(n, PLE_DIM), wup, cw, wd, ln_ple_g[i][None],
                        w_ple_gate[i].astype(BF16), w_ple_proj[i].astype(BF16), s)
    return x2d.reshape(b, s, D_MODEL)
```

```python
import functools
import math

import jax
import jax.numpy as jnp
import numpy as np
from jax import lax
from jax.experimental import pallas as pl
from jax.experimental.pallas import tpu as pltpu

D_MODEL = 1024
HEAD_DIM = 64
GROUP_WIDTH = 256
N_HEADS = GROUP_WIDTH // HEAD_DIM
C_GROUPS = 4
C_CHUNK = 128
DILATED_CFGS = ((128, 1), (512, 4), (2048, 16))
SWA_RADIUS = 128
GRID_W = 64
ROPE_THETA = 10000.0
REL_BUCKETS = 32
REL_MAX_DIST = 1024
D_FF = 2816
PLE_DIM = 256
EPS = 1e-6
NEG_INF = -1e30
ATTN_SCALE = HEAD_DIM ** -0.5
LOG2E = math.log2(math.e)

BF16 = jnp.bfloat16
F32 = jnp.float32

SEC_AQ, SEC_AK, SEC_AV, SEC_BQ, SEC_BK, SEC_BV, SEC_YC, SEC_DQ, SEC_DK, SEC_DV = range(10)
N_SEC = 10
A_WIDTH = 3 * GROUP_WIDTH

VMEM_LIMIT = 56 * 1024 * 1024

KIN_TM = 1024
KIN_ROWS = 1024
BAND_TQ = 128
BAND_UNROLL = 16
SWA_UNROLL = 16
DENSE_TQ = 256
DENSE_ROWS = 1024
MIX_TM = 1024
FFN_TM = 512
FFN_TF = 256
LANES = 128
SUB = 8
STRANDS = 8
HALO = 16
N_CHUNKS = D_FF // FFN_TF
FFN_GROUP = 4


def _params(sem):
    return pltpu.CompilerParams(dimension_semantics=sem, vmem_limit_bytes=VMEM_LIMIT)


def _resident(shape):
    zeros = (0,) * len(shape)
    return pl.BlockSpec(shape, lambda *_: zeros, pipeline_mode=pl.Buffered(1))


def _rms(t, gain):
    return t * lax.rsqrt(jnp.mean(t * t, axis=-1, keepdims=True) + EPS) * gain


def _head_id(shape):
    return lax.broadcasted_iota(jnp.int32, shape, 1) >> 6


def _proj_kernel(x_ref, g_ref, w_ref, vec_ref, cos_ref, sin_ref, bd_ref, wcat_ref, bsx_ref,
                 o_ref, d4_ref, d16_ref, a_ref):
    for t in range(x_ref.shape[0] // KIN_TM):
        _proj_tile(t, x_ref, g_ref, w_ref, vec_ref, cos_ref, sin_ref, bd_ref, wcat_ref, bsx_ref,
                   o_ref, d4_ref, d16_ref, a_ref)


def _proj_tile(t, x_ref, g_ref, w_ref, vec_ref, cos_ref, sin_ref, bd_ref, wcat_ref, bsx_ref,
               o_ref, d4_ref, d16_ref, a_ref):
    tm = KIN_TM
    r0 = t * tm
    x = x_ref[r0:r0 + tm, :]
    hn = _rms(x, g_ref[...]).astype(BF16)
    bd = bd_ref[...]
    slots = A_WIDTH // LANES

    def proj(sec):
        return jnp.dot(hn, w_ref[:, sec * GROUP_WIDTH:(sec + 1) * GROUP_WIDTH],
                       preferred_element_type=F32)

    def store(sec, val):
        o_ref[r0:r0 + tm, sec * GROUP_WIDTH:(sec + 1) * GROUP_WIDTH] = val.astype(BF16)

    def store_a(sec, val):
        store(sec, val)
        for half in range(GROUP_WIDTH // LANES):
            slot = sec * (GROUP_WIDTH // LANES) + half
            a_ref[t * slots + slot] = val[:, half * LANES:(half + 1) * LANES]
            cols = slice(slot * LANES, (slot + 1) * LANES)
            for dil, ref in ((4, d4_ref), (16, d16_ref)):
                for r in range(dil):
                    piece = a_ref[t * slots + slot, pl.ds(r, tm // dil, stride=dil), :]
                    ref[r, t * (tm // dil):(t + 1) * (tm // dil), cols] = piece.astype(BF16)

    def head_rms(t, row):
        width = t.shape[1]
        sq = t * t
        hi = sq.astype(BF16)
        lo = (sq - hi.astype(F32)).astype(BF16)
        blk = bd[0:width, 0:width]
        ms = (jnp.dot(hi, blk, preferred_element_type=F32) + jnp.dot(lo, blk, preferred_element_type=F32))
        return t * lax.rsqrt(ms + EPS) * vec_ref[row:row + 1, 0:width]

    def rope(t):
        blocks = []
        for c0 in range(0, t.shape[1], LANES):
            th = t[:, c0:c0 + LANES]
            lane = lax.broadcasted_iota(jnp.int32, th.shape, 1)
            partner = jnp.where((lane & 31) < 16, pltpu.roll(th, LANES - 16, axis=1), pltpu.roll(th, 16, axis=1))
            blocks.append(th * cos_ref[r0:r0 + tm, c0:c0 + LANES] + partner * sin_ref[r0:r0 + tm, c0:c0 + LANES])
        return blocks[0] if len(blocks) == 1 else jnp.concatenate(blocks, axis=-1)

    def rep_kv(t):
        swapped = pltpu.roll(t, HEAD_DIM, axis=1)
        low = lax.broadcasted_iota(jnp.int32, t.shape, 1) < HEAD_DIM
        return jnp.concatenate([jnp.where(low, t, swapped), jnp.where(low, swapped, t)], axis=-1)

    store_a(SEC_AQ, head_rms(proj(0), 0))
    store_a(SEC_AK, head_rms(proj(1), 1))
    store_a(SEC_AV, proj(2))
    store(SEC_BQ, head_rms(proj(3), 2))
    b_kv = proj(4)
    store(SEC_BK, rep_kv(head_rms(b_kv[:, 0:LANES], 3)))
    store(SEC_BV, rep_kv(b_kv[:, LANES:]))
    store(SEC_DQ, rope(head_rms(proj(7), 4)))
    d_kv = proj(8)
    store(SEC_DK, rep_kv(rope(head_rms(d_kv[:, 0:LANES], 5))))
    store(SEC_DV, jnp.concatenate([d_kv[:, LANES:]] * 2, axis=-1))

    u = jax.nn.gelu(proj(5))
    v = jax.nn.gelu(proj(6))
    mu = jnp.mean(v, axis=-1, keepdims=True)
    vc = v - mu
    var = jnp.mean(vc * vc, axis=-1, keepdims=True)
    vn = (vc * lax.rsqrt(var + EPS) * vec_ref[6:7, :] + vec_ref[7:8, :]).astype(BF16)
    gid = _head_id((C_CHUNK, GROUP_WIDTH))
    zero = jnp.zeros((C_CHUNK, GROUP_WIDTH), BF16)
    for c in range(tm // C_CHUNK):
        rows = slice(c * C_CHUNK, (c + 1) * C_CHUNK)
        vchunk = vn[rows]
        stacked = jnp.concatenate([jnp.where(gid == g, vchunk, zero) for g in range(C_GROUPS)], axis=0)
        mixed = jnp.dot(wcat_ref[...], stacked, preferred_element_type=F32) + bsx_ref[...]
        yc = _rms(u[rows] * mixed, vec_ref[8:9, :])
        o_ref[r0 + c * C_CHUNK:r0 + (c + 1) * C_CHUNK, SEC_YC * GROUP_WIDTH:(SEC_YC + 1) * GROUP_WIDTH] = yc.astype(BF16)


def _proj_call(x2d, g, w, vec, cos_t, sin_t, bd, wcat, bsx, batch, seq):
    n = x2d.shape[0]
    tps = seq // KIN_ROWS
    return pl.pallas_call(
        _proj_kernel,
        grid=(n // KIN_ROWS,),
        in_specs=[
            pl.BlockSpec((KIN_ROWS, D_MODEL), lambda i: (i, 0)),
            _resident((1, D_MODEL)),
            _resident(w.shape),
            _resident(vec.shape),
            pl.BlockSpec((KIN_ROWS, GROUP_WIDTH), lambda i: (i % tps, 0)),
            pl.BlockSpec((KIN_ROWS, GROUP_WIDTH), lambda i: (i % tps, 0)),
            _resident(bd.shape),
            _resident(wcat.shape),
            _resident(bsx.shape),
        ],
        out_specs=[
            pl.BlockSpec((KIN_ROWS, N_SEC * GROUP_WIDTH), lambda i: (i, 0)),
            pl.BlockSpec((None, 4, KIN_ROWS // 4, A_WIDTH), lambda i: (i // tps, 0, i % tps, 0)),
            pl.BlockSpec((None, 16, KIN_ROWS // 16, A_WIDTH), lambda i: (i // tps, 0, i % tps, 0)),
        ],
        out_shape=[
            jax.ShapeDtypeStruct((n, N_SEC * GROUP_WIDTH), BF16),
            jax.ShapeDtypeStruct((batch, 4, seq // 4, A_WIDTH), BF16),
            jax.ShapeDtypeStruct((batch, 16, seq // 16, A_WIDTH), BF16),
        ],
        scratch_shapes=[pltpu.VMEM((KIN_ROWS // KIN_TM * (A_WIDTH // LANES), KIN_TM, LANES), F32)],
        compiler_params=_params(("parallel",)),
        name="proj",
    )(x2d, g, w, vec, cos_t, sin_t, bd, wcat, bsx)


def _attend(q, kwin, vwin, bias_ref, case, sink_ref=None):
    tq = q.shape[0]
    hid = _head_id((tq, GROUP_WIDTH))
    qzero = jnp.zeros_like(q)
    q4 = jnp.concatenate([jnp.where(hid == h, q, qzero) for h in range(N_HEADS)], axis=0)
    s = lax.dot_general(q4, kwin, (((1,), (1,)), ((), ())), preferred_element_type=F32)
    s = s + bias_ref[case]
    if sink_ref is None:
        m = jnp.max(s, axis=-1, keepdims=True)
        p = jnp.exp2(s - m)
        den = jnp.sum(p, axis=-1, keepdims=True)
    else:
        sink = sink_ref[...]
        blocks = [s[:, c:c + LANES] for c in range(0, s.shape[1], LANES)]
        m = jnp.max(functools.reduce(jnp.maximum, blocks + [sink]), axis=-1, keepdims=True)
        p = jnp.exp2(s - m)
        pblocks = [p[:, c:c + LANES] for c in range(0, s.shape[1], LANES)]
        den = jnp.sum(functools.reduce(jnp.add, pblocks + [jnp.exp2(sink - m)]), axis=-1, keepdims=True)
    o4 = jnp.dot(p.astype(BF16), vwin, preferred_element_type=F32) * (1.0 / den)
    l4 = jnp.broadcast_to(m + jnp.log2(den), o4.shape)
    out = o4[0:tq]
    lse = l4[0:tq]
    for h in range(1, N_HEADS):
        out = jnp.where(hid == h, o4[h * tq:(h + 1) * tq], out)
        lse = jnp.where(hid == h, l4[h * tq:(h + 1) * tq], lse)
    return out, lse


def _band_window(j, tq, radius, length, kw):
    n_tiles = length // tq
    if n_tiles == 1:
        return 0, 0
    ks = pl.multiple_of(jnp.clip(j * tq - radius, 0, length - kw), 64)
    case = jnp.where(j == 0, 0, jnp.where(j == n_tiles - 1, 2, 1))
    return ks, case


def _dilated_kernel(q_ref, k_ref, v_ref, d4_ref, d16_ref, b1_ref, b4_ref, b16_ref, gain_ref, o_ref,
                    acc_ref, lse_ref, *, radii):
    seq = q_ref.shape[0]
    tq = BAND_TQ

    kw1 = min(tq + 2 * radii[0], seq)

    def body1(j, carry):
        q0 = pl.multiple_of(j * tq, tq)
        ks, case = _band_window(j, tq, radii[0], seq, kw1)
        o, l = _attend(q_ref[pl.ds(q0, tq), :], k_ref[pl.ds(ks, kw1), :], v_ref[pl.ds(ks, kw1), :], b1_ref, case)
        for half in range(GROUP_WIDTH // LANES):
            cols = slice(half * LANES, (half + 1) * LANES)
            acc_ref[half, pl.ds(q0, tq), :] = o[:, cols]
            lse_ref[half, pl.ds(q0, tq), :] = l[:, cols]
        return carry

    lax.fori_loop(0, seq // tq, body1, 0, unroll=BAND_UNROLL)

    for dil, ref, bias_ref, radius in ((4, d4_ref, b4_ref, radii[1]), (16, d16_ref, b16_ref, radii[2])):
        length = seq // dil
        tiles = length // tq
        kw = min(tq + 2 * radius, length)

        def body(it, carry, dil=dil, ref=ref, bias_ref=bias_ref, radius=radius, length=length, tiles=tiles, kw=kw):
            r = it // tiles
            j = it % tiles
            q0 = pl.multiple_of(j * tq, tq)
            ks, case = _band_window(j, tq, radius, length, kw)
            o, l = _attend(ref[r, pl.ds(q0, tq), 0:GROUP_WIDTH],
                           ref[r, pl.ds(ks, kw), GROUP_WIDTH:2 * GROUP_WIDTH],
                           ref[r, pl.ds(ks, kw), 2 * GROUP_WIDTH:3 * GROUP_WIDTH], bias_ref, case)
            rows = pl.ds(r + dil * q0, tq, stride=dil)
            for half in range(GROUP_WIDTH // LANES):
                cols = slice(half * LANES, (half + 1) * LANES)
                l_new = l[:, cols]
                l_old = lse_ref[half, rows, :]
                m = jnp.maximum(l_old, l_new)
                wa = jnp.exp2(l_old - m)
                wb = jnp.exp2(l_new - m)
                tot = wa + wb
                acc_ref[half, rows, :] = (wa * acc_ref[half, rows, :] + wb * o[:, cols]) * (1.0 / tot)
                lse_ref[half, rows, :] = m + jnp.log2(tot)
            return carry

        lax.fori_loop(0, dil * tiles, body, 0, unroll=BAND_UNROLL)

    acc = jnp.concatenate([acc_ref[0], acc_ref[1]], axis=-1)
    o_ref[...] = _rms(acc, gain_ref[...]).astype(BF16)


def _dilated_call(proj3d, d4, d16, biases, gain):
    b, seq, _ = proj3d.shape
    radii = tuple(window // (2 * dil) for window, dil in DILATED_CFGS)
    kern = functools.partial(_dilated_kernel, radii=radii)
    return pl.pallas_call(
        kern,
        grid=(b,),
        in_specs=[
            pl.BlockSpec((None, seq, GROUP_WIDTH), lambda bi: (bi, 0, SEC_AQ)),
            pl.BlockSpec((None, seq, GROUP_WIDTH), lambda bi: (bi, 0, SEC_AK)),
            pl.BlockSpec((None, seq, GROUP_WIDTH), lambda bi: (bi, 0, SEC_AV)),
            pl.BlockSpec((None,) + d4.shape[1:], lambda bi: (bi, 0, 0, 0)),
            pl.BlockSpec((None,) + d16.shape[1:], lambda bi: (bi, 0, 0, 0)),
            _resident(biases[0].shape), _resident(biases[1].shape), _resident(biases[2].shape),
            _resident((1, GROUP_WIDTH)),
        ],
        out_specs=pl.BlockSpec((None, seq, GROUP_WIDTH), lambda bi: (bi, 0, 0)),
        out_shape=jax.ShapeDtypeStruct((b, seq, GROUP_WIDTH), BF16),
        scratch_shapes=[pltpu.VMEM((GROUP_WIDTH // LANES, seq, LANES), F32)] * 2,
        compiler_params=_params(("parallel",)),
        name="dilated",
    )(proj3d, proj3d, proj3d, d4, d16, biases[0], biases[1], biases[2], gain)


def _swa_kernel(q_ref, k_ref, v_ref, bias_ref, sink_ref, gain_ref, o_ref):
    seq = q_ref.shape[0]
    tq = BAND_TQ
    kw = min(tq + 2 * SWA_RADIUS, seq)

    def body(j, carry):
        q0 = pl.multiple_of(j * tq, tq)
        ks, case = _band_window(j, tq, SWA_RADIUS, seq, kw)
        o, _ = _attend(q_ref[pl.ds(q0, tq), :], k_ref[pl.ds(ks, kw), :], v_ref[pl.ds(ks, kw), :],
                       bias_ref, case, sink_ref)
        o_ref[pl.ds(q0, tq), :] = _rms(o, gain_ref[...]).astype(BF16)
        return carry

    lax.fori_loop(0, seq // tq, body, 0, unroll=SWA_UNROLL)


def _swa_call(proj3d, bias, sink, gain):
    b, seq, _ = proj3d.shape
    return pl.pallas_call(
        _swa_kernel,
        grid=(b,),
        in_specs=[
            pl.BlockSpec((None, seq, GROUP_WIDTH), lambda bi: (bi, 0, SEC_BQ)),
            pl.BlockSpec((None, seq, GROUP_WIDTH), lambda bi: (bi, 0, SEC_BK)),
            pl.BlockSpec((None, seq, GROUP_WIDTH), lambda bi: (bi, 0, SEC_BV)),
            _resident(bias.shape),
            _resident(sink.shape),
            _resident((1, GROUP_WIDTH)),
        ],
        out_specs=pl.BlockSpec((None, seq, GROUP_WIDTH), lambda bi: (bi, 0, 0)),
        out_shape=jax.ShapeDtypeStruct((b, seq, GROUP_WIDTH), BF16),
        compiler_params=_params(("parallel",)),
        name="swa",
    )(proj3d, proj3d, proj3d, bias, sink, gain)


def _dense_kernel(q_ref, k_ref, v_ref, gain_ref, o_ref):
    k = k_ref[...]
    vext = jnp.concatenate([v_ref[...], jnp.ones((k.shape[0], LANES), BF16)], axis=-1)
    tq = DENSE_TQ
    hid = _head_id((tq, GROUP_WIDTH))
    low = lax.broadcasted_iota(jnp.int32, (tq, LANES), 1) < HEAD_DIM
    for t in range(q_ref.shape[0] // tq):
        q = q_ref[t * tq:(t + 1) * tq, :]
        qzero = jnp.zeros_like(q)
        heads = []
        for h in range(N_HEADS):
            qh = jnp.where(hid == h, q, qzero)
            s = lax.dot_general(qh, k, (((1,), (1,)), ((), ())), preferred_element_type=F32)
            m = jnp.max(s, axis=-1, keepdims=True)
            p = jnp.exp2((s - m).astype(BF16))
            o = jnp.dot(p, vext, preferred_element_type=F32)
            heads.append(o[:, 0:LANES] * (1.0 / o[:, LANES:]))
        blk0 = jnp.where(low, heads[0], pltpu.roll(heads[1], HEAD_DIM, axis=1))
        blk1 = jnp.where(low, pltpu.roll(heads[2], HEAD_DIM, axis=1), heads[3])
        acc = jnp.concatenate([blk0, blk1], axis=-1)
        o_ref[t * tq:(t + 1) * tq, :] = _rms(acc, gain_ref[...]).astype(BF16)


def _dense_call(proj3d, gain):
    b, seq, _ = proj3d.shape
    return pl.pallas_call(
        _dense_kernel,
        grid=(b, seq // DENSE_ROWS),
        in_specs=[
            pl.BlockSpec((None, DENSE_ROWS, GROUP_WIDTH), lambda bi, j: (bi, j, SEC_DQ)),
            pl.BlockSpec((None, seq, GROUP_WIDTH), lambda bi, j: (bi, 0, SEC_DK)),
            pl.BlockSpec((None, seq, LANES), lambda bi, j: (bi, 0, SEC_DV * (GROUP_WIDTH // LANES))),
            _resident((1, GROUP_WIDTH)),
        ],
        out_specs=pl.BlockSpec((None, DENSE_ROWS, GROUP_WIDTH), lambda bi, j: (bi, j, 0)),
        out_shape=jax.ShapeDtypeStruct((b, seq, GROUP_WIDTH), BF16),
        compiler_params=_params(("parallel", "arbitrary")),
        name="dense",
    )(proj3d, proj3d, proj3d, gain)


def _mix_kernel(x_ref, ya_ref, yb_ref, yc_ref, yd_ref, w_ref, g_ref, x1_ref, xn_ref, xs_ref):
    mixed = jnp.concatenate([ya_ref[...], yb_ref[...], yc_ref[...], yd_ref[...]], axis=-1)
    x1 = x_ref[...] + jnp.dot(mixed, w_ref[...], preferred_element_type=F32)
    x1_ref[...] = x1
    xn = _rms(x1, g_ref[...])
    sl = FFN_TM // STRANDS
    for c in range(D_MODEL // LANES):
        xs_ref[c] = xn[:, c * LANES:(c + 1) * LANES]
        col = jnp.concatenate([xs_ref[c, pl.ds(r0 + b, sl, stride=STRANDS), :]
                               for r0 in range(0, xn.shape[0], FFN_TM) for b in range(STRANDS)], axis=0)
        xn_ref[:, c * LANES:(c + 1) * LANES] = col.astype(BF16)


def _mix_call(x2d, ya, yb, proj2d, yd, w_out, g_ffn):
    n = x2d.shape[0]
    row = lambda i: (i, 0)
    grp = pl.BlockSpec((MIX_TM, GROUP_WIDTH), row)
    return pl.pallas_call(
        _mix_kernel,
        grid=(n // MIX_TM,),
        in_specs=[pl.BlockSpec((MIX_TM, D_MODEL), row), grp, grp,
                  pl.BlockSpec((MIX_TM, GROUP_WIDTH), lambda i: (i, SEC_YC)), grp,
                  _resident(w_out.shape), _resident((1, D_MODEL))],
        out_specs=[pl.BlockSpec((MIX_TM, D_MODEL), row), pl.BlockSpec((MIX_TM, D_MODEL), row)],
        out_shape=[jax.ShapeDtypeStruct((n, D_MODEL), F32), jax.ShapeDtypeStruct((n, D_MODEL), BF16)],
        scratch_shapes=[pltpu.VMEM((D_MODEL // LANES, MIX_TM, LANES), F32)],
        compiler_params=_params(("parallel",)),
        name="mix_out",
    )(x2d, ya, yb, proj2d, yd, w_out, g_ffn)


def _ffn_kernel(xn_ref, prev_ref, next_ref, x1_ref, p_ref, wup_ref, cw_ref, wd_ref, gple_ref, wgate_ref,
                wproj_ref, o_ref, xcat_ref, h0_ref, h1_ref, acc_ref, tok_ref, *, tiles_per_seq, n_tiles):
    step = pl.program_id(0)
    tm = xn_ref.shape[0]
    rows = tm + HALO
    sl = tm // STRANDS
    pitch = sl + SUB

    def up(k, h_ref):
        xcat = xcat_ref[...]
        for part in range(2):
            col = (k + part * N_CHUNKS) * FFN_TF
            if not isinstance(col, int):
                col = pl.multiple_of(col, FFN_TF)
            h = jnp.dot(xcat, wup_ref[:, pl.ds(col, FFN_TF)], preferred_element_type=F32)
            for b in range(STRANDS):
                h_ref[part, SUB + b * pitch:SUB + b * pitch + sl, :] = h[b * sl:(b + 1) * sl]
            h_ref[part, SUB + sl:SUB + pitch, :] = h[tm:tm + SUB]
            h_ref[part, (STRANDS - 1) * pitch:SUB + (STRANDS - 1) * pitch, :] = h[rows - SUB:rows]

    def strand(h_ref, part, b, shift=0):
        start = SUB + b * pitch + shift
        return h_ref[part, start:start + sl, :]

    def conv(h_ref, part, cw):
        out = []
        for b in range(STRANDS):
            before = strand(h_ref, part, b - 1) if b > 0 else strand(h_ref, part, STRANDS - 1, -1)
            after = strand(h_ref, part, b + 1) if b < STRANDS - 1 else strand(h_ref, part, 0, 1)
            out.append(before * cw[0:1, :] + strand(h_ref, part, b) * cw[1:2, :] + after * cw[2:3, :] + cw[3:4, :])
        return jnp.concatenate(out, axis=0)

    def gated(k, h_ref):
        gt = conv(h_ref, 0, cw_ref[k])
        uu = conv(h_ref, 1, cw_ref[k + N_CHUNKS])
        return (gt / (1.0 + jnp.exp(-gt)) * uu).astype(BF16)

    def stage():
        pos = jnp.minimum(step, n_tiles - 1) % tiles_per_seq
        prev_blk = jnp.where(pos == 0, jnp.zeros_like(prev_ref[...]), prev_ref[...])
        next_blk = jnp.where(pos == tiles_per_seq - 1, jnp.zeros_like(next_ref[...]), next_ref[...])
        halo_row = lax.broadcasted_iota(jnp.int32, (HALO, D_MODEL), 0)
        xcat_ref[0:tm, :] = xn_ref[...]
        xcat_ref[tm:rows, :] = jnp.where(halo_row == 0, next_blk, prev_blk)
        up(0, h0_ref)

    @pl.when(step == 0)
    def _():
        stage()

    @pl.when(step > 0)
    def _():
        bufs = (h0_ref, h1_ref)
        acts = []
        for k in range(N_CHUNKS):
            if k + 1 < N_CHUNKS:
                up(k + 1, bufs[(k + 1) % 2])
            acts.append(gated(k, bufs[k % 2]))
            if len(acts) == FFN_GROUP or k == N_CHUNKS - 1:
                k0 = k + 1 - len(acts)
                wd_group = wd_ref[k0:k + 1].reshape(len(acts) * FFN_TF, D_MODEL)
                part = jnp.dot(jnp.concatenate(acts, axis=-1), wd_group, preferred_element_type=F32)
                if k0 == 0:
                    acc_ref[...] = part
                else:
                    acc_ref[...] += part
                acts = []

        stage()

        for c in range(D_MODEL // LANES):
            for b in range(STRANDS):
                tok_ref[c, pl.ds(b, sl, stride=STRANDS), :] = acc_ref[b * sl:(b + 1) * sl, c * LANES:(c + 1) * LANES]
        ffn = jnp.concatenate([tok_ref[c] for c in range(D_MODEL // LANES)], axis=-1)
        x2 = x1_ref[...] + ffn
        z = jnp.dot(_rms(x2, gple_ref[...]).astype(BF16), wgate_ref[...], preferred_element_type=F32)
        gate = 1.0 / (1.0 + jnp.exp(-z))
        inj = jnp.dot(p_ref[...].astype(BF16), wproj_ref[...], preferred_element_type=F32)
        o_ref[...] = x2 + inj * gate


def _ffn_call(xn, x1, p2d, wup, cw, wd, g_ple, w_gate, w_proj, seq):
    assert N_CHUNKS % 2 == 1
    assert MIX_TM % FFN_TM == 0
    n = xn.shape[0]
    n_tiles = n // FFN_TM
    halo_per_tile = FFN_TM // HALO
    n_halo = n // HALO
    staged = lambda s: jnp.minimum(s, n_tiles - 1)
    done = lambda s: (jnp.maximum(s - 1, 0), 0)
    kern = functools.partial(_ffn_kernel, tiles_per_seq=seq // FFN_TM, n_tiles=n_tiles)
    return pl.pallas_call(
        kern,
        grid=(n_tiles + 1,),
        in_specs=[
            pl.BlockSpec((FFN_TM, D_MODEL), lambda s: (staged(s), 0)),
            pl.BlockSpec((HALO, D_MODEL), lambda s: (jnp.maximum(staged(s) * halo_per_tile - 1, 0), 0)),
            pl.BlockSpec((HALO, D_MODEL),
                         lambda s: (jnp.minimum((staged(s) + 1) * halo_per_tile, n_halo - 1), 0)),
            pl.BlockSpec((FFN_TM, D_MODEL), done),
            pl.BlockSpec((FFN_TM, PLE_DIM), done),
            _resident(wup.shape), _resident(cw.shape), _resident(wd.shape),
            _resident((1, D_MODEL)), _resident(w_gate.shape), _resident(w_proj.shape),
        ],
        out_specs=pl.BlockSpec((FFN_TM, D_MODEL), done),
        out_shape=jax.ShapeDtypeStruct((n, D_MODEL), F32),
        scratch_shapes=[pltpu.VMEM((FFN_TM + HALO, D_MODEL), BF16),
                        pltpu.VMEM((2, SUB + STRANDS * (FFN_TM // STRANDS + SUB), FFN_TF), F32),
                        pltpu.VMEM((2, SUB + STRANDS * (FFN_TM // STRANDS + SUB), FFN_TF), F32),
                        pltpu.VMEM((FFN_TM, D_MODEL), F32),
                        pltpu.VMEM((D_MODEL // LANES, FFN_TM, LANES), F32)],
        compiler_params=_params(("arbitrary",)),
        name="ffn_ple",
    )(xn, xn, xn, x1, p2d, wup, cw, wd, g_ple, w_gate, w_proj)


def _t5_bucket(rel):
    nb = REL_BUCKETS // 2
    ret = jnp.where(rel > 0, nb, 0)
    n = jnp.abs(rel)
    max_exact = nb // 2
    nf = jnp.maximum(n, 1).astype(F32)
    large = max_exact + (jnp.log(nf / max_exact) / math.log(REL_MAX_DIST / max_exact)
                         * (nb - max_exact)).astype(jnp.int32)
    large = jnp.minimum(large, nb - 1)
    return ret + jnp.where(n < max_exact, n, large)


def _band_bias(table, radius, dil, tq, kw):
    cases = []
    for off in (0, radius, kw - tq):
        rel = jnp.arange(kw)[None, :] - jnp.arange(tq)[:, None] - off
        onehot = (_t5_bucket(rel * dil)[..., None] == jnp.arange(REL_BUCKETS)).astype(F32)
        bias = jnp.einsum('qkn,nh->hqk', onehot, table.astype(F32), precision=lax.Precision.HIGHEST)
        bias = jnp.where((jnp.abs(rel) <= radius)[None], bias, NEG_INF)
        cases.append(bias.reshape(N_HEADS * tq, kw))
    return jnp.stack(cases, axis=0)


def _rope_tables(seq):
    half = HEAD_DIM // 2
    inv = ROPE_THETA ** (-jnp.arange(0, half, 2, dtype=F32) / half)
    t = jnp.arange(seq)
    ang_r = (t // GRID_W).astype(F32)[:, None] * inv[None, :]
    ang_c = (t % GRID_W).astype(F32)[:, None] * inv[None, :]
    cos_h = jnp.concatenate([jnp.cos(ang_r)] * 2 + [jnp.cos(ang_c)] * 2, axis=-1)
    sin_h = jnp.concatenate([-jnp.sin(ang_r), jnp.sin(ang_r), -jnp.sin(ang_c), jnp.sin(ang_c)], axis=-1)
    return jnp.tile(cos_h, (1, N_HEADS)), jnp.tile(sin_h, (1, N_HEADS))


def kernel(x, p, rel_bias, ln_mix_g, w_in, qk_gain, sink, c_norm_g, c_norm_b, c_ws, c_bs, out_gain, w_out,
           ln_ffn_g, w_up, conv_w, conv_b, w_down, ln_ple_g, w_ple_gate, w_ple_proj):
    b, s, _ = x.shape
    n = b * s
    depth = w_in.shape[0]
    cos_t, sin_t = _rope_tables(s)
    bd = (np.arange(GROUP_WIDTH)[:, None] // HEAD_DIM == np.arange(GROUP_WIDTH)[None, :] // HEAD_DIM)
    bd = jnp.asarray(bd / HEAD_DIM, BF16)
    table_a = rel_bias[:, :N_HEADS] * LOG2E
    table_b = rel_bias[:, N_HEADS:] * LOG2E
    q_scale = ATTN_SCALE * LOG2E
    bias_a = []
    for window, dil in DILATED_CFGS:
        radius = window // (2 * dil)
        length = s // dil
        tq = min(BAND_TQ, length)
        bias_a.append(_band_bias(table_a, radius, dil, tq, min(tq + 2 * radius, length)))
    tq_b = min(BAND_TQ, s)
    bias_b = _band_bias(table_b, SWA_RADIUS, 1, tq_b, min(tq_b + 2 * SWA_RADIUS, s))

    x2d = x.reshape(n, D_MODEL)
    for i in range(depth):
        w = w_in[i].astype(BF16)
        qg = qk_gain[i]
        vec = jnp.stack([
            jnp.tile(qg[0, 0], N_HEADS) * q_scale, jnp.tile(qg[0, 1], N_HEADS),
            jnp.tile(qg[1, 0], N_HEADS) * q_scale, jnp.tile(qg[1, 1], N_HEADS),
            jnp.tile(qg[2, 0], N_HEADS) * q_scale, jnp.tile(qg[2, 1], N_HEADS),
            c_norm_g[i], c_norm_b[i], out_gain[i, 2],
        ] + [jnp.zeros((GROUP_WIDTH,), F32)] * 7, axis=0)
        wcat = jnp.concatenate([c_ws[i, g] for g in range(C_GROUPS)], axis=1).astype(BF16)
        bsx = jnp.repeat(jnp.transpose(c_bs[i]), GROUP_WIDTH // C_GROUPS, axis=1)
        proj2d, d4, d16 = _proj_call(x2d, ln_mix_g[i][None], w, vec, cos_t, sin_t, bd, wcat, bsx, b, s)

        proj3d = proj2d.reshape(b, s, N_SEC * GROUP_WIDTH)
        ya = _dilated_call(proj3d, d4, d16, bias_a, out_gain[i, 0][None])
        sink_blk = jnp.full((N_HEADS * tq_b, LANES), NEG_INF, F32).at[:, 0].set(jnp.repeat(sink[i] * LOG2E, tq_b))
        yb = _swa_call(proj3d, bias_b, sink_blk, out_gain[i, 1][None])
        yd = _dense_call(proj3d, out_gain[i, 3][None])

        x1, xn = _mix_call(x2d, ya.reshape(n, GROUP_WIDTH), yb.reshape(n, GROUP_WIDTH), proj2d,
                           yd.reshape(n, GROUP_WIDTH), w_out[i].astype(BF16), ln_ffn_g[i][None])
        wup = w_up[i].astype(BF16)
        cw = jnp.concatenate([conv_w[i], conv_b[i][None]], axis=0)
        cw = jnp.transpose(cw.reshape(4, 2 * N_CHUNKS, FFN_TF), (1, 0, 2))
        wd = w_down[i].astype(BF16).reshape(N_CHUNKS, FFN_TF, D_MODEL)
        x2d = _ffn_call(xn, x1, p[i].reshape(n, PLE_DIM), wup, cw, wd, ln_ple_g[i][None],
                        w_ple_gate[i].astype(BF16), w_ple_proj[i].astype(BF16), s)
    return x2d.reshape(b, s, D_MODEL)
```
